```python
import jax, jax.numpy as jnp
from jax import lax
import numpy as np

D_MODEL = 1024
BATCH = 4
SEQ = 4096
DEPTH = 2

MIX_WIDTH = D_MODEL
GROUP_WIDTH = MIX_WIDTH // 4
ROPE_THETA = 500000.0
NORM_EPS = 1e-6

MLA_HEADS = 4
MLA_V = GROUP_WIDTH // MLA_HEADS
MLA_NOPE = 64
MLA_ROPE = 32
MLA_Q_RANK = D_MODEL // 4
MLA_KV_RANK = D_MODEL // 8
MLA_Q_BLOCK = 128

GLA_HEADS = 4
GLA_DV = GROUP_WIDTH // GLA_HEADS
GLA_DK = GLA_DV // 2
GLA_GATE_RANK = 16
GLA_TAU = 16.0
GLA_CHUNK = 64

MOBA_HEADS = 4
MOBA_HD = GROUP_WIDTH // MOBA_HEADS
MOBA_ROT = MOBA_HD // 4
MOBA_BLOCK = 256
MOBA_TOPK = 3
MOBA_Q_CHUNK = 64

LRU_WIDTH = GROUP_WIDTH
LRU_BLOCKS = 4
LRU_BW = LRU_WIDTH // LRU_BLOCKS
LRU_CONV = 4
LRU_C = 8.0

FFN_HIDDEN = -(-8 * D_MODEL // (3 * 256)) * 256

IN_SPLITS = (
    MLA_Q_RANK, MLA_KV_RANK, MLA_ROPE,
    GLA_HEADS * GLA_DK, GLA_HEADS * GLA_DK, GROUP_WIDTH, GLA_GATE_RANK, GROUP_WIDTH,
    GROUP_WIDTH, GROUP_WIDTH, GROUP_WIDTH,
    LRU_WIDTH, LRU_WIDTH,
)
IN_WIDTH = int(sum(IN_SPLITS))
SPLIT_POINTS = [int(s) for s in np.cumsum(IN_SPLITS)[:-1]]

kernel_name = "hybrid_parallel_mla_gla_moba_rglru"


def rms_norm(x, g):
    xf = x.astype(jnp.float32)
    y = xf * lax.rsqrt(jnp.mean(xf * xf, axis=-1, keepdims=True) + NORM_EPS)
    return (y * g.astype(jnp.float32)).astype(x.dtype)


def rope_tables(seq_len, dim):
    inv_freq = ROPE_THETA ** (-jnp.arange(0, dim, 2, dtype=jnp.float32) / dim)
    ang = jnp.arange(seq_len, dtype=jnp.float32)[:, None] * inv_freq[None, :]
    return jnp.cos(ang), jnp.sin(ang)


def apply_rope(x, cos, sin):
    half = x.shape[-1] // 2
    x1 = x[..., :half].astype(jnp.float32)
    x2 = x[..., half:].astype(jnp.float32)
    return jnp.concatenate([x1 * cos - x2 * sin, x2 * cos + x1 * sin], axis=-1).astype(x.dtype)


def causal_attention(q, k, v, scale):
    B, H, S, _ = q.shape
    dv = v.shape[-1]
    kpos = jnp.arange(S)

    def one(i):
        start = i * MLA_Q_BLOCK
        qb = lax.dynamic_slice_in_dim(q, start, MLA_Q_BLOCK, axis=2)
        s = jnp.einsum('bhqd,bhkd->bhqk', qb, k).astype(jnp.float32) * scale
        qpos = start + jnp.arange(MLA_Q_BLOCK)
        s = jnp.where(kpos[None, :] <= qpos[:, None], s, -jnp.inf)
        p = jax.nn.softmax(s, axis=-1).astype(v.dtype)
        return jnp.einsum('bhqk,bhkd->bhqd', p, v)

    o = lax.map(one, jnp.arange(S // MLA_Q_BLOCK))
    return jnp.moveaxis(o, 0, 2).reshape(B, H, S, dv)


def mla_mixer(c_q, c_kv, k_pe, q_norm_g, w_uq, kv_norm_g, w_ukv, cos, sin):
    B, S, _ = c_q.shape
    q = (rms_norm(c_q, q_norm_g) @ w_uq).reshape(B, S, MLA_HEADS, MLA_NOPE + MLA_ROPE).transpose(0, 2, 1, 3)
    q = jnp.concatenate([q[..., :MLA_NOPE], apply_rope(q[..., MLA_NOPE:], cos, sin)], axis=-1)
    kv = (rms_norm(c_kv, kv_norm_g) @ w_ukv).reshape(B, S, MLA_HEADS, MLA_NOPE + MLA_V).transpose(0, 2, 1, 3)
    k_pe = apply_rope(k_pe[:, None], cos, sin)
    k = jnp.concatenate([kv[..., :MLA_NOPE], jnp.broadcast_to(k_pe, (B, MLA_HEADS, S, MLA_ROPE))], axis=-1)
    v = kv[..., MLA_NOPE:]
    o = causal_attention(q, k, v, (MLA_NOPE + MLA_ROPE) ** -0.5)
    return o.transpose(0, 2, 1, 3).reshape(B, S, MLA_HEADS * MLA_V)


def gla_mixer(q, k, v, a_low, g, w_a2, b_a2, head_norm_g):
    B, S, _ = q.shape
    H, C = GLA_HEADS, GLA_CHUNK
    N = S // C

    def heads(t, d):
        return t.reshape(B, S, H, d).transpose(0, 2, 1, 3).reshape(B, H, N, C, d).astype(jnp.float32)

    log_a = jax.nn.log_sigmoid((a_low @ w_a2 + b_a2).astype(jnp.float32)) / GLA_TAU
    qf = heads(q, GLA_DK) * GLA_DK ** -0.5
    kf = heads(k, GLA_DK)
    vf = heads(v, GLA_DV)
    b = jnp.cumsum(heads(log_a, GLA_DK), axis=3)
    b_last = b[:, :, :, -1:, :]
    b_ref = 0.5 * b_last
    att = jnp.einsum('bhncd,bhnsd->bhncs', qf * jnp.exp(b - b_ref), kf * jnp.exp(b_ref - b))
    causal = jnp.tril(jnp.ones((C, C), dtype=bool))
    o_intra = jnp.einsum('bhncs,bhnsv->bhncv', jnp.where(causal, att, 0.0), vf)
    d_state = jnp.einsum('bhncd,bhncv->bhndv', kf * jnp.exp(b_last - b), vf)
    decay = jnp.exp(b_last[:, :, :, 0, :])

    def step(state, inp):
        dcy, ds = inp
        return dcy[..., None] * state + ds, state

    _, s_before = lax.scan(step, jnp.zeros((B, H, GLA_DK, GLA_DV), jnp.float32),
                           (jnp.moveaxis(decay, 2, 0), jnp.moveaxis(d_state, 2, 0)))
    s_before = jnp.moveaxis(s_before, 0, 2)
    o_inter = jnp.einsum('bhncd,bhndv->bhncv', qf * jnp.exp(b), s_before)
    o = (o_intra + o_inter).reshape(B, H, S, GLA_DV)
    o = rms_norm(o, head_norm_g).transpose(0, 2, 1, 3).reshape(B, S, H * GLA_DV)
    return (o * jax.nn.silu(g.astype(jnp.float32))).astype(g.dtype)


def moba_mixer(q, k, v, cos, sin):
    B, S, _ = q.shape
    H, D, BS, QC = MOBA_HEADS, MOBA_HD, MOBA_BLOCK, MOBA_Q_CHUNK

    def heads(t):
        return t.reshape(B, S, H, D).transpose(0, 2, 1, 3)

    def partial_rope(t):
        return jnp.concatenate([apply_rope(t[..., :MOBA_ROT], cos, sin), t[..., MOBA_ROT:]], axis=-1)

    q = partial_rope(heads(q))
    k = partial_rope(heads(k))
    v = heads(v)
    pad = (-S) % BS
    Sp = S + pad
    padw = ((0, 0), (0, 0), (0, pad), (0, 0))
    q, k, v = jnp.pad(q, padw), jnp.pad(k, padw), jnp.pad(v, padw)
    NB = Sp // BS
    topk = min(MOBA_TOPK, NB)
    k_blocks = k.reshape(B, H, NB, BS, D)
    v_blocks = v.reshape(B, H, NB, BS, D)
    k_mean = jnp.mean(k_blocks.astype(jnp.float32), axis=3)
    scale = D ** -0.5
    bi = jnp.arange(B)[:, None, None, None]
    hi = jnp.arange(H)[None, :, None, None]

    def one(i):
        start = i * QC
        blk = start // BS
        qb = lax.dynamic_slice_in_dim(q, start, QC, axis=2)
        gate = jnp.einsum('bhqd,bhnd->bhqn', qb.astype(jnp.float32), k_mean)
        gate = jnp.where(jnp.arange(NB) < blk, gate, -jnp.inf)
        _, idx = lax.top_k(gate, topk)
        valid = idx < blk
        k_sel = k_blocks[bi, hi, idx]
        v_sel = v_blocks[bi, hi, idx]
        s_sel = jnp.einsum('bhqd,bhqjkd->bhqjk', qb, k_sel).astype(jnp.float32) * scale
        s_sel = jnp.where(valid[..., None], s_sel, -jnp.inf).reshape(B, H, QC, topk * BS)
        k_own = lax.dynamic_slice_in_dim(k, blk * BS, BS, axis=2)
        v_own = lax.dynamic_slice_in_dim(v, blk * BS, BS, axis=2)
        s_own = jnp.einsum('bhqd,bhkd->bhqk', qb, k_own).astype(jnp.float32) * scale
        causal = (blk * BS + jnp.arange(BS))[None, :] <= (start + jnp.arange(QC))[:, None]
        s_own = jnp.where(causal, s_own, -jnp.inf)
        p = jax.nn.softmax(jnp.concatenate([s_own, s_sel], axis=-1), axis=-1).astype(v.dtype)
        p_own = p[..., :BS]
        p_sel = p[..., BS:].reshape(B, H, QC, topk, BS)
        return (jnp.einsum('bhqk,bhkd->bhqd', p_own, v_own)
                + jnp.einsum('bhqjk,bhqjkd->bhqd', p_sel, v_sel))

    o = lax.map(one, jnp.arange(Sp // QC))
    o = jnp.moveaxis(o, 0, 2).reshape(B, H, Sp, D)[:, :, :S]
    return o.transpose(0, 2, 1, 3).reshape(B, S, H * D)


def rglru_mixer(xb, gate, conv_w, conv_b, w_a, b_a, w_x, b_x, lam):
    B, S, W = xb.shape
    xc = lax.conv_general_dilated(xb, conv_w, window_strides=(1,), padding=[(LRU_CONV - 1, 0)],
                                  dimension_numbers=('NWC', 'WIO', 'NWC'),
                                  feature_group_count=W) + conv_b
    xh = xc.reshape(B, S, LRU_BLOCKS, LRU_BW)
    r = jax.nn.sigmoid(jnp.einsum('bsnc,ncd->bsnd', xh, w_a).reshape(B, S, W) + b_a)
    i = jax.nn.sigmoid(jnp.einsum('bsnc,ncd->bsnd', xh, w_x).reshape(B, S, W) + b_x)
    log_a = (-LRU_C * r.astype(jnp.float32)) * jax.nn.softplus(-lam.astype(jnp.float32))
    a = jnp.exp(log_a)
    u = jnp.sqrt(-jnp.expm1(2.0 * log_a)) * (i * xc).astype(jnp.float32)

    def combine(left, right):
        a_l, b_l = left
        a_r, b_r = right
        return a_l * a_r, a_r * b_l + b_r

    _, h = lax.associative_scan(combine, (a, u), axis=1)
    return (h * jax.nn.gelu(gate.astype(jnp.float32))).astype(xb.dtype)


def setup_inputs(seed: int = 0) -> dict:
    key = jax.random.key(seed)
    keys = iter(jax.random.split(key, 32))
    L = DEPTH

    def dense(shape, fan_in):
        return jax.random.normal(next(keys), shape, jnp.float32) * fan_in ** -0.5

    def gain(shape):
        return 1.0 + 0.02 * jax.random.normal(next(keys), shape, jnp.float32)

    def bias(shape):
        return 0.02 * jax.random.normal(next(keys), shape, jnp.float32)

    x = jax.random.normal(next(keys), (BATCH, SEQ, D_MODEL), jnp.float32)
    attn_norm = gain((L, D_MODEL))
    w_in = dense((L, D_MODEL, IN_WIDTH), D_MODEL)
    mla_q_norm = gain((L, MLA_Q_RANK))
    mla_w_uq = dense((L, MLA_Q_RANK, MLA_HEADS * (MLA_NOPE + MLA_ROPE)), MLA_Q_RANK)
    mla_kv_norm = gain((L, MLA_KV_RANK))
    mla_w_ukv = dense((L, MLA_KV_RANK, MLA_HEADS * (MLA_NOPE + MLA_V)), MLA_KV_RANK)
    gla_w_a2 = dense((L, GLA_GATE_RANK, GLA_HEADS * GLA_DK), GLA_GATE_RANK)
    gla_b_a2 = bias((L, GLA_HEADS * GLA_DK))
    gla_head_norm = gain((L, GLA_DV))
    lru_conv_w = dense((L, LRU_CONV, 1, LRU_WIDTH), LRU_CONV)
    lru_conv_b = bias((L, LRU_WIDTH))
    lru_w_a = dense((L, LRU_BLOCKS, LRU_BW, LRU_BW), LRU_BW)
    lru_b_a = bias((L, LRU_WIDTH))
    lru_w_x = dense((L, LRU_BLOCKS, LRU_BW, LRU_BW), LRU_BW)
    lru_b_x = bias((L, LRU_WIDTH))
    u = jax.random.uniform(next(keys), (L, LRU_WIDTH), jnp.float32, 0.9, 0.999)
    a0 = u ** (1.0 / LRU_C)
    lru_lambda = jnp.log(a0) - jnp.log1p(-a0)
    w_out = dense((L, MIX_WIDTH, D_MODEL), MIX_WIDTH)
    ffn_norm = gain((L, D_MODEL))
    w_ffn_in = dense((L, D_MODEL, 2 * FFN_HIDDEN), D_MODEL)
    w_ffn_out = dense((L, FFN_HIDDEN, D_MODEL), FFN_HIDDEN)
    final_norm = gain((D_MODEL,))
    return {
        "x": x, "attn_norm": attn_norm, "w_in": w_in,
        "mla_q_norm": mla_q_norm, "mla_w_uq": mla_w_uq, "mla_kv_norm": mla_kv_norm, "mla_w_ukv": mla_w_ukv,
        "gla_w_a2": gla_w_a2, "gla_b_a2": gla_b_a2, "gla_head_norm": gla_head_norm,
        "lru_conv_w": lru_conv_w, "lru_conv_b": lru_conv_b, "lru_w_a": lru_w_a, "lru_b_a": lru_b_a,
        "lru_w_x": lru_w_x, "lru_b_x": lru_b_x, "lru_lambda": lru_lambda,
        "w_out": w_out, "ffn_norm": ffn_norm, "w_ffn_in": w_ffn_in, "w_ffn_out": w_ffn_out,
        "final_norm": final_norm,
    }


def reference(x, attn_norm, w_in, mla_q_norm, mla_w_uq, mla_kv_norm, mla_w_ukv,
              gla_w_a2, gla_b_a2, gla_head_norm,
              lru_conv_w, lru_conv_b, lru_w_a, lru_b_a, lru_w_x, lru_b_x, lru_lambda,
              w_out, ffn_norm, w_ffn_in, w_ffn_out, final_norm):
    S = x.shape[1]
    cos_mla, sin_mla = rope_tables(S, MLA_ROPE)
    cos_moba, sin_moba = rope_tables(S, MOBA_ROT)
    for l in range(DEPTH):
        h = rms_norm(x, attn_norm[l])
        (mq, mkv, mkr, gq, gk, gv, ga, gg, bq, bk, bv, rx, rg) = jnp.split(h @ w_in[l], SPLIT_POINTS, axis=-1)
        y_a = mla_mixer(mq, mkv, mkr, mla_q_norm[l], mla_w_uq[l], mla_kv_norm[l], mla_w_ukv[l], cos_mla, sin_mla)
        y_b = gla_mixer(gq, gk, gv, ga, gg, gla_w_a2[l], gla_b_a2[l], gla_head_norm[l])
        y_c = moba_mixer(bq, bk, bv, cos_moba, sin_moba)
        y_d = rglru_mixer(rx, rg, lru_conv_w[l], lru_conv_b[l], lru_w_a[l], lru_b_a[l],
                          lru_w_x[l], lru_b_x[l], lru_lambda[l])
        y = jnp.concatenate([y_a.astype(x.dtype), y_b.astype(x.dtype), y_c.astype(x.dtype), y_d.astype(x.dtype)], axis=-1)
        x = x + y @ w_out[l]
        h = rms_norm(x, ffn_norm[l])
        g, up = jnp.split(h @ w_ffn_in[l], 2, axis=-1)
        x = x + (jax.nn.silu(g) * up) @ w_ffn_out[l]
    return rms_norm(x, final_norm)
```

```python
import functools

import numpy as np
import jax
import jax.numpy as jnp
from jax import lax
from jax.experimental import pallas as pl
from jax.experimental.pallas import tpu as pltpu

F32 = jnp.float32
BF16 = jnp.bfloat16

D_MODEL = 1024
GROUP_WIDTH = 256
ROPE_THETA = 500000.0
NORM_EPS = 1e-6
MLA_HEADS = 4
MLA_V = 64
MLA_NOPE = 64
MLA_ROPE = 32
MLA_Q_RANK = 256
MLA_KV_RANK = 128
GLA_HEADS = 4
GLA_DV = 64
GLA_DK = 32
GLA_GATE_RANK = 16
GLA_TAU = 16.0
GLA_CHUNK = 64
MOBA_HEADS = 4
MOBA_HD = 64
MOBA_ROT = 16
MOBA_BLOCK = 256
MOBA_TOPK = 3
LRU_WIDTH = 256
LRU_BLOCKS = 4
LRU_BW = 64
LRU_CONV = 4
LRU_C = 8.0
FFN_HIDDEN = 2816

LANES = 128
SUBLANES = 8
VMEM_LIMIT = 56 * 1024 * 1024

W_MLA = 512
W_GLA = 896
W_MOBA = 1280
W_LRU = 512
NEG_BIG = -1e30


def _rms(x, g):
    return x * lax.rsqrt(jnp.mean(x * x, axis=-1, keepdims=True) + NORM_EPS) * g


def _dot(a, b):
    return jnp.dot(a, b, preferred_element_type=F32)


def _dot_nt(a, b):
    return lax.dot_general(a, b, (((1,), (1,)), ((), ())), preferred_element_type=F32)


def _dot_tn(a, b):
    return lax.dot_general(a, b, (((0,), (0,)), ((), ())), preferred_element_type=F32)


def _const_spec(shape):
    nd = len(shape)
    return pl.BlockSpec(shape, lambda *_: (0,) * nd)


def _params(sem):
    return pltpu.CompilerParams(dimension_semantics=sem, vmem_limit_bytes=VMEM_LIMIT)


def _inproj_kernel(x_ref, g_ref, w_ref, mla_ref, gla_ref, moba_ref, lru_ref):
    h = _rms(x_ref[...], g_ref[...]).astype(BF16)
    off = 0
    for ref, width in ((mla_ref, W_MLA), (gla_ref, W_GLA), (moba_ref, W_MOBA), (lru_ref, W_LRU)):
        ref[...] = _dot(h, w_ref[:, off:off + width])
        off += width


def _inproj(x, g, w, tm):
    T = x.shape[0]
    nw = w.shape[1]
    return pl.pallas_call(
        _inproj_kernel,
        grid=(T // tm,),
        in_specs=[pl.BlockSpec((tm, D_MODEL), lambda i: (i, 0)), _const_spec((1, D_MODEL)),
                  _const_spec((D_MODEL, nw))],
        out_specs=[pl.BlockSpec((tm, wd), lambda i: (i, 0)) for wd in (W_MLA, W_GLA, W_MOBA, W_LRU)],
        out_shape=[jax.ShapeDtypeStruct((T, wd), F32) for wd in (W_MLA, W_GLA, W_MOBA, W_LRU)],
        compiler_params=_params(("parallel",)),
        name="inproj",
    )(x, g, w)


def _mla_prep_kernel(p_ref, qg_ref, wq_ref, kvg_ref, wkv_ref, place_ref, cq_ref, sq_ref, r32_ref,
                     q_ref, k_ref, v_ref):
    p = p_ref[...]
    nq = _rms(p[:, :MLA_Q_RANK], qg_ref[...]).astype(BF16)
    q2 = _dot(nq, wq_ref[...])
    hw = MLA_HEADS * LANES
    q_ref[...] = (q2[:, :hw] * cq_ref[...] + q2[:, hw:] * sq_ref[...]).astype(BF16)
    nkv = _rms(p[:, MLA_Q_RANK:MLA_Q_RANK + MLA_KV_RANK], kvg_ref[...]).astype(BF16)
    kv = _dot(nkv, wkv_ref[...])
    c0 = MLA_Q_RANK + MLA_KV_RANK
    kr = p[:, c0:c0 + MLA_ROPE]
    kr_sw = p[:, c0 + MLA_ROPE:c0 + 2 * MLA_ROPE]
    r32 = r32_ref[...]
    k_pe = kr * r32[:, :MLA_ROPE] + kr_sw * r32[:, MLA_ROPE:]
    k_ref[...] = (kv[:, :hw] + _dot(k_pe.astype(BF16), place_ref[...])).astype(BF16)
    v_ref[...] = kv[:, hw:].astype(BF16)


def _mla_prep(p_mla, qg, wq, kvg, wkv, place, cq, sq, r32, S, tm):
    T = p_mla.shape[0]
    ns = S // tm
    hw = MLA_HEADS * LANES
    return pl.pallas_call(
        _mla_prep_kernel,
        grid=(T // tm,),
        in_specs=[pl.BlockSpec((tm, W_MLA), lambda i: (i, 0)),
                  _const_spec(qg.shape), _const_spec(wq.shape), _const_spec(kvg.shape),
                  _const_spec(wkv.shape), _const_spec(place.shape),
                  pl.BlockSpec((tm, hw), lambda i: (i % ns, 0)),
                  pl.BlockSpec((tm, hw), lambda i: (i % ns, 0)),
                  pl.BlockSpec((tm, 2 * MLA_ROPE), lambda i: (i % ns, 0))],
        out_specs=[pl.BlockSpec((tm, hw), lambda i: (i, 0)), pl.BlockSpec((tm, hw), lambda i: (i, 0)),
                   pl.BlockSpec((tm, GROUP_WIDTH), lambda i: (i, 0))],
        out_shape=[jax.ShapeDtypeStruct((T, hw), BF16), jax.ShapeDtypeStruct((T, hw), BF16),
                   jax.ShapeDtypeStruct((T, GROUP_WIDTH), BF16)],
        compiler_params=_params(("parallel",)),
        name="mla_prep",
    )(p_mla, qg, wq, kvg, wkv, place, cq, sq, r32)


def _softmax_step(s, v, m, l, acc):
    m_new = jnp.maximum(m, jnp.max(s, axis=-1, keepdims=True))
    alpha = jnp.exp(m - m_new)
    p = jnp.exp(s - m_new)
    l_new = alpha * l + jnp.sum(p, axis=-1, keepdims=True)
    acc_new = alpha * acc + _dot(p.astype(BF16), v)
    return m_new, l_new, acc_new


def _mla_attn_kernel(q_ref, k_ref, v_ref, o_ref, *, tq):
    i = pl.program_id(2)
    q = q_ref[...]
    qa, qb = q[:, :LANES], q[:, LANES:]

    def init():
        return (jnp.full((tq, 1), NEG_BIG, F32), jnp.zeros((tq, 1), F32), jnp.zeros((tq, LANES), F32))

    def body(j, carry):
        ca, cb = carry
        rows = pl.ds(pl.multiple_of(j * tq, tq), tq)
        k = k_ref[rows, :]
        v = v_ref[rows, :]
        ca = _softmax_step(_dot_nt(qa, k[:, :LANES]), v, *ca)
        cb = _softmax_step(_dot_nt(qb, k[:, LANES:]), v, *cb)
        return ca, cb

    ca, cb = lax.fori_loop(0, i, body, (init(), init()))
    rows = pl.ds(pl.multiple_of(i * tq, tq), tq)
    k = k_ref[rows, :]
    v = v_ref[rows, :]
    causal = lax.broadcasted_iota(jnp.int32, (tq, tq), 1) <= lax.broadcasted_iota(jnp.int32, (tq, tq), 0)
    ca = _softmax_step(jnp.where(causal, _dot_nt(qa, k[:, :LANES]), NEG_BIG), v, *ca)
    cb = _softmax_step(jnp.where(causal, _dot_nt(qb, k[:, LANES:]), NEG_BIG), v, *cb)
    oa = ca[2] / ca[1]
    ob = cb[2] / cb[1]
    lane = lax.broadcasted_iota(jnp.int32, (tq, LANES), 1)
    o_ref[...] = jnp.where(lane < MLA_V, oa, ob).astype(o_ref.dtype)


def _mla_attn(q, k, v, B, S, tq):
    T = q.shape[0]
    nq = S // tq
    return pl.pallas_call(
        functools.partial(_mla_attn_kernel, tq=tq),
        grid=(B, MLA_HEADS // 2, nq),
        in_specs=[pl.BlockSpec((tq, 2 * LANES), lambda b, pr, i: (b * nq + i, pr)),
                  pl.BlockSpec((S, 2 * LANES), lambda b, pr, i: (b, pr)),
                  pl.BlockSpec((S, LANES), lambda b, pr, i: (b, pr))],
        out_specs=pl.BlockSpec((tq, LANES), lambda b, pr, i: (b * nq + i, pr)),
        out_shape=jax.ShapeDtypeStruct((T, GROUP_WIDTH), BF16),
        compiler_params=_params(("parallel", "parallel", "arbitrary")),
        name="mla_attn",
    )(q, k, v)


def _gla_kernel(p_ref, wa_ref, ba_ref, hn_ref, o_ref, state_ref, *, tm):
    C = GLA_CHUNK
    hk = GLA_HEADS * GLA_DK
    hv = GLA_HEADS * GLA_DV

    @pl.when(pl.program_id(1) == 0)
    def _():
        state_ref[...] = jnp.zeros_like(state_ref)

    rc = lax.broadcasted_iota(jnp.int32, (C, hk), 0)
    lane_k = lax.broadcasted_iota(jnp.int32, (C, hk), 1)
    lane_v = lax.broadcasted_iota(jnp.int32, (C, hv), 1)
    row4 = lax.broadcasted_iota(jnp.int32, (GLA_HEADS * C, C), 0)
    col4 = lax.broadcasted_iota(jnp.int32, (GLA_HEADS * C, C), 1)
    causal4 = (row4 % C) >= col4
    sr = lax.broadcasted_iota(jnp.int32, (hk, hv), 0)
    sc = lax.broadcasted_iota(jnp.int32, (hk, hv), 1)
    blockdiag = (sr // GLA_DK) == (sc // GLA_DV)
    er = lax.broadcasted_iota(jnp.int32, (hk, hk), 0)
    ec = lax.broadcasted_iota(jnp.int32, (hk, hk), 1)
    eye = er == ec
    wa = wa_ref[...]
    ba = ba_ref[...]
    hn = hn_ref[...]
    scale = GLA_DK ** -0.5

    def chunk(c, _):
        rows = pl.ds(pl.multiple_of(c * C, C), C)
        q = p_ref[rows, 0:hk]
        k = p_ref[rows, hk:2 * hk]
        v = p_ref[rows, 2 * hk:2 * hk + hv]
        g = p_ref[rows, 2 * hk + hv:2 * hk + 2 * hv]
        a_low = p_ref[rows, 2 * hk + 2 * hv:2 * hk + 2 * hv + LANES]
        a_lin = _dot(a_low.astype(BF16), wa) + ba
        la = (jnp.minimum(a_lin, 0.0) - jnp.log1p(jnp.exp(-jnp.abs(a_lin)))) * (1.0 / GLA_TAU)
        b = la
        for s in (1, 2, 4, 8, 16, 32):
            b = b + jnp.where(rc >= s, pltpu.roll(b, s, 0), 0.0)
        bl = b[C - 1:C, :]
        bref = 0.5 * bl
        qf = q * scale
        qs = qf * jnp.exp(b - bref)
        ks = (k * jnp.exp(bref - b)).astype(BF16)
        kd = (k * jnp.exp(bl - b)).astype(BF16)
        qe = (qf * jnp.exp(b)).astype(BF16)
        vb = v.astype(BF16)
        qs4 = jnp.concatenate(
            [jnp.where((lane_k // GLA_DK) == h, qs, 0.0) for h in range(GLA_HEADS)], axis=0).astype(BF16)
        att = jnp.where(causal4, _dot_nt(qs4, ks), 0.0)
        res = _dot(att.astype(BF16), vb)
        o = _dot(qe, state_ref[...].astype(BF16))
        for h in range(GLA_HEADS):
            o = o + jnp.where((lane_v // GLA_DV) == h, res[h * C:(h + 1) * C, :], 0.0)
        dst = _dot_tn(kd, vb)
        decay_row = jnp.exp(bl)
        decay_col = jnp.sum(jnp.where(eye, jnp.broadcast_to(decay_row, (hk, hk)), 0.0), axis=1, keepdims=True)
        state_ref[...] = decay_col * state_ref[...] + jnp.where(blockdiag, dst, 0.0)
        osq = o * o
        inv = jnp.zeros_like(o)
        for h in range(GLA_HEADS):
            mh = (lane_v // GLA_DV) == h
            ms = jnp.sum(jnp.where(mh, osq, 0.0), axis=-1, keepdims=True) * (1.0 / GLA_DV)
            inv = jnp.where(mh, lax.rsqrt(ms + NORM_EPS), inv)
        sig = 1.0 / (1.0 + jnp.exp(-g))
        o_ref[rows, :] = (o * inv * hn * (g * sig)).astype(o_ref.dtype)
        return 0

    lax.fori_loop(0, tm // C, chunk, 0)


def _gla(p_gla, wa, ba, hn, B, S, tm):
    T = p_gla.shape[0]
    ns = S // tm
    return pl.pallas_call(
        functools.partial(_gla_kernel, tm=tm),
        grid=(B, ns),
        in_specs=[pl.BlockSpec((tm, W_GLA), lambda b, s: (b * ns + s, 0)),
                  _const_spec(wa.shape), _const_spec(ba.shape), _const_spec(hn.shape)],
        out_specs=pl.BlockSpec((tm, GROUP_WIDTH), lambda b, s: (b * ns + s, 0)),
        out_shape=jax.ShapeDtypeStruct((T, GROUP_WIDTH), BF16),
        scratch_shapes=[pltpu.VMEM((GLA_HEADS * GLA_DK, GLA_HEADS * GLA_DV), F32)],
        compiler_params=_params(("arbitrary", "arbitrary")),
        name="gla",
    )(p_gla, wa, ba, hn)


def _moba_prep_kernel(p_ref, cm_ref, sm_ref, q_ref, k_ref, v_ref, sel_ref, km_ref, *, tm):
    W = GROUP_WIDTH
    si = pl.program_id(1)

    @pl.when(si == 0)
    def _():
        km_ref[...] = jnp.zeros_like(km_ref)

    cm = cm_ref[...]
    sm = sm_ref[...]
    q = p_ref[:, 0:W] * cm + p_ref[:, 3 * W:4 * W] * sm
    k = p_ref[:, W:2 * W] * cm + p_ref[:, 4 * W:5 * W] * sm
    q = q * (MOBA_HD ** -0.5)
    q_ref[...] = q.astype(BF16)
    k_ref[...] = k.astype(BF16)
    v_ref[...] = p_ref[:, 2 * W:3 * W].astype(BF16)
    nblk = tm // MOBA_BLOCK
    for j in range(nblk):
        mean = jnp.sum(k[j * MOBA_BLOCK:(j + 1) * MOBA_BLOCK, :], axis=0, keepdims=True) * (1.0 / MOBA_BLOCK)
        km_ref[pl.ds(si * nblk + j, 1), :] = mean
    km = km_ref[...]
    lane = lax.broadcasted_iota(jnp.int32, (tm, W), 1)
    n_iota = lax.broadcasted_iota(jnp.int32, (tm, LANES), 1)
    pos = si * tm + lax.broadcasted_iota(jnp.int32, (tm, LANES), 0)
    past = n_iota < (pos // MOBA_BLOCK)
    n_f = n_iota.astype(F32)
    for h in range(MOBA_HEADS):
        qh = jnp.where((lane // MOBA_HD) == h, q, 0.0)
        gate = lax.dot_general(qh, km, (((1,), (1,)), ((), ())), preferred_element_type=F32,
                               precision=lax.Precision.HIGHEST)
        gate = jnp.where(past, gate, -jnp.inf)
        sel = jnp.zeros((tm, LANES), F32)
        for _ in range(MOBA_TOPK):
            mx = jnp.max(gate, axis=-1, keepdims=True)
            first = jnp.min(jnp.where(gate == mx, n_f, float(LANES)), axis=-1, keepdims=True)
            pick = (n_f == first) & (mx > -jnp.inf)
            sel = jnp.where(pick, 1.0, sel)
            gate = jnp.where(pick, -jnp.inf, gate)
        sel_ref[:, h * LANES:(h + 1) * LANES] = sel.astype(BF16)


def _moba_prep(p_moba, cm, sm, B, S, tm):
    T = p_moba.shape[0]
    ns = S // tm
    W = GROUP_WIDTH
    tok = lambda b, s: (b * ns + s, 0)
    return pl.pallas_call(
        functools.partial(_moba_prep_kernel, tm=tm),
        grid=(B, ns),
        in_specs=[pl.BlockSpec((tm, W_MOBA), tok),
                  pl.BlockSpec((tm, W), lambda b, s: (s, 0)), pl.BlockSpec((tm, W), lambda b, s: (s, 0))],
        out_specs=[pl.BlockSpec((tm, W), tok), pl.BlockSpec((tm, W), tok), pl.BlockSpec((tm, W), tok),
                   pl.BlockSpec((tm, MOBA_HEADS * LANES), tok)],
        out_shape=[jax.ShapeDtypeStruct((T, W), BF16)] * 3 + [jax.ShapeDtypeStruct((T, MOBA_HEADS * LANES), BF16)],
        scratch_shapes=[pltpu.VMEM((LANES, W), F32)],
        compiler_params=_params(("arbitrary", "arbitrary")),
        name="moba_prep",
    )(p_moba, cm, sm)


def _moba_attn_kernel(q_ref, k_ref, v_ref, sel_ref, o_ref):
    tq = MOBA_BLOCK
    i = pl.program_id(2)
    q = q_ref[...]
    lane = lax.broadcasted_iota(jnp.int32, (tq, LANES), 1)
    first_head = lane < MOBA_HD
    qa = jnp.where(first_head, q, jnp.zeros_like(q))
    qb = jnp.where(first_head, jnp.zeros_like(q), q)
    sela = sel_ref[:, :LANES].astype(F32)
    selb = sel_ref[:, LANES:].astype(F32)

    def init():
        return (jnp.full((tq, 1), NEG_BIG, F32), jnp.zeros((tq, 1), F32), jnp.zeros((tq, LANES), F32))

    rows = pl.ds(pl.multiple_of(i * tq, tq), tq)
    k = k_ref[rows, :]
    v = v_ref[rows, :]
    causal = lax.broadcasted_iota(jnp.int32, (tq, tq), 1) <= lax.broadcasted_iota(jnp.int32, (tq, tq), 0)
    ca = _softmax_step(jnp.where(causal, _dot_nt(qa, k), NEG_BIG), v, *init())
    cb = _softmax_step(jnp.where(causal, _dot_nt(qb, k), NEG_BIG), v, *init())

    def body(j, carry):
        ca, cb = carry
        rows = pl.ds(pl.multiple_of(j * tq, tq), tq)
        k = k_ref[rows, :]
        v = v_ref[rows, :]
        hit = lane == j
        ma = jnp.sum(jnp.where(hit, sela, 0.0), axis=-1, keepdims=True) > 0.5
        mb = jnp.sum(jnp.where(hit, selb, 0.0), axis=-1, keepdims=True) > 0.5
        ca = _softmax_step(jnp.where(ma, _dot_nt(qa, k), NEG_BIG), v, *ca)
        cb = _softmax_step(jnp.where(mb, _dot_nt(qb, k), NEG_BIG), v, *cb)
        return ca, cb

    ca, cb = lax.fori_loop(0, i, body, (ca, cb))
    oa = ca[2] / ca[1]
    ob = cb[2] / cb[1]
    o_ref[...] = jnp.where(first_head, oa, ob).astype(o_ref.dtype)


def _moba_attn(q, k, v, sel, B, S):
    T = q.shape[0]
    tq = MOBA_BLOCK
    nq = S // tq
    return pl.pallas_call(
        _moba_attn_kernel,
        grid=(B, MOBA_HEADS // 2, nq),
        in_specs=[pl.BlockSpec((tq, LANES), lambda b, pr, i: (b * nq + i, pr)),
                  pl.BlockSpec((S, LANES), lambda b, pr, i: (b, pr)),
                  pl.BlockSpec((S, LANES), lambda b, pr, i: (b, pr)),
                  pl.BlockSpec((tq, 2 * LANES), lambda b, pr, i: (b * nq + i, pr))],
        out_specs=pl.BlockSpec((tq, LANES), lambda b, pr, i: (b * nq + i, pr)),
        out_shape=jax.ShapeDtypeStruct((T, GROUP_WIDTH), BF16),
        compiler_params=_params(("parallel", "parallel", "arbitrary")),
        name="moba_attn",
    )(q, k, v, sel)


def _lru_kernel(p_ref, cw_ref, cb_ref, wax_ref, bax_ref, lam_ref, o_ref, tail_ref, h_ref, *, tm):
    W = LRU_WIDTH

    @pl.when(pl.program_id(1) == 0)
    def _():
        tail_ref[...] = jnp.zeros_like(tail_ref)
        h_ref[...] = jnp.zeros_like(h_ref)

    x = p_ref[:, 0:W]
    gate = p_ref[:, W:2 * W]
    xx = jnp.concatenate([tail_ref[...], x], axis=0)
    cw = cw_ref[...]
    xc = cb_ref[...] + cw[LRU_CONV - 1:LRU_CONV, :] * x
    for d in range(1, LRU_CONV):
        xc = xc + cw[LRU_CONV - 1 - d:LRU_CONV - d, :] * xx[SUBLANES - d:SUBLANES - d + tm, :]
    tail_ref[...] = x[tm - SUBLANES:, :]
    ri = _dot(xc.astype(BF16), wax_ref[...]) + bax_ref[...]
    r = 1.0 / (1.0 + jnp.exp(-ri[:, :W]))
    ig = 1.0 / (1.0 + jnp.exp(-ri[:, W:]))
    lam = lam_ref[...]
    softplus_neg = jnp.maximum(-lam, 0.0) + jnp.log1p(jnp.exp(-jnp.abs(lam)))
    log_a = (-LRU_C * r) * softplus_neg
    a = jnp.exp(log_a)
    u = jnp.sqrt(-jnp.tanh(log_a) * (a * a + 1.0)) * (ig * xc)
    row = lax.broadcasted_iota(jnp.int32, (tm, W), 0)
    s = 1
    while s < tm:
        live = row >= s
        u = u + a * jnp.where(live, pltpu.roll(u, s, 0), 0.0)
        a = a * jnp.where(live, pltpu.roll(a, s, 0), 1.0)
        s *= 2
    h = u + a * h_ref[...]
    h_ref[...] = h[tm - 1:tm, :]
    gelu = 0.5 * gate * (1.0 + jnp.tanh(0.7978845608028654 * (gate + 0.044715 * gate * gate * gate)))
    o_ref[...] = (h * gelu).astype(o_ref.dtype)


def _lru(p_lru, cw, cb, wax, bax, lam, B, S, tm):
    T = p_lru.shape[0]
    ns = S // tm
    tok = lambda b, s: (b * ns + s, 0)
    return pl.pallas_call(
        functools.partial(_lru_kernel, tm=tm),
        grid=(B, ns),
        in_specs=[pl.BlockSpec((tm, W_LRU), tok), _const_spec(cw.shape), _const_spec(cb.shape),
                  _const_spec(wax.shape), _const_spec(bax.shape), _const_spec(lam.shape)],
        out_specs=pl.BlockSpec((tm, LRU_WIDTH), tok),
        out_shape=jax.ShapeDtypeStruct((T, LRU_WIDTH), BF16),
        scratch_shapes=[pltpu.VMEM((SUBLANES, LRU_WIDTH), F32), pltpu.VMEM((1, LRU_WIDTH), F32)],
        compiler_params=_params(("arbitrary", "arbitrary")),
        name="lru",
    )(p_lru, cw, cb, wax, bax, lam)


def _ffn_kernel(x_ref, ya_ref, yb_ref, yc_ref, yd_ref, wo_ref, fg_ref, wi_ref, w2_ref, ng_ref, o_ref,
                *, final, chunk):
    W = GROUP_WIDTH
    x1 = x_ref[...]
    for n, y_ref in enumerate((ya_ref, yb_ref, yc_ref, yd_ref)):
        x1 = x1 + _dot(y_ref[...], wo_ref[n * W:(n + 1) * W, :])
    h = _rms(x1, fg_ref[...]).astype(BF16)
    parts = []
    for c in range(FFN_HIDDEN // chunk):
        g = _dot(h, wi_ref[:, c * chunk:(c + 1) * chunk])
        up = _dot(h, wi_ref[:, FFN_HIDDEN + c * chunk:FFN_HIDDEN + (c + 1) * chunk])
        act = (g * (1.0 / (1.0 + jnp.exp(-g))) * up).astype(BF16)
        parts.append(_dot(act, w2_ref[c * chunk:(c + 1) * chunk, :]))
    acc = x1 + sum(parts[1:], parts[0])
    if final:
        acc = _rms(acc, ng_ref[...])
    o_ref[...] = acc


def _ffn(x, ya, yb, yc, yd, wo, fg, wi, w2, ng, final, tm, chunk):
    T = x.shape[0]
    tok = lambda i: (i, 0)
    ysp = pl.BlockSpec((tm, GROUP_WIDTH), tok)
    return pl.pallas_call(
        functools.partial(_ffn_kernel, final=final, chunk=chunk),
        grid=(T // tm,),
        in_specs=[pl.BlockSpec((tm, D_MODEL), tok), ysp, ysp, ysp, ysp,
                  _const_spec(wo.shape), _const_spec(fg.shape), _const_spec(wi.shape),
                  _const_spec(w2.shape), _const_spec(ng.shape)],
        out_specs=pl.BlockSpec((tm, D_MODEL), tok),
        out_shape=jax.ShapeDtypeStruct((T, D_MODEL), F32),
        compiler_params=_params(("parallel",)),
        name="outproj_ffn",
    )(x, ya, yb, yc, yd, wo, fg, wi, w2, ng)


def _swap_cols(w, half):
    return jnp.concatenate([-w[:, half:2 * half], w[:, :half]], axis=1)


def _layout_w_in(w):
    sp = np.cumsum([0, 256, 128, 32, 128, 128, 256, 16, 256, 256, 256, 256, 256, 256])
    col = lambda n: w[:, sp[n]:sp[n + 1]]
    mq, mkv, mkr, gq, gk, gv, ga, gg, bq, bk, bv, rx, rg = [col(n) for n in range(13)]
    z = lambda n: jnp.zeros((w.shape[0], n), w.dtype)

    def moba_sw(m):
        parts = []
        half = MOBA_ROT // 2
        for h in range(MOBA_HEADS):
            parts += [_swap_cols(m[:, h * MOBA_HD:h * MOBA_HD + MOBA_ROT], half), z(MOBA_HD - MOBA_ROT)]
        return jnp.concatenate(parts, axis=1)

    mla = [mq, mkv, mkr, _swap_cols(mkr, MLA_ROPE // 2), z(W_MLA - 256 - 128 - 64)]
    gla = [gq, gk, gv, gg, ga, z(LANES - GLA_GATE_RANK)]
    moba = [bq, bk, bv, moba_sw(bq), moba_sw(bk)]
    return jnp.concatenate(mla + gla + moba + [rx, rg], axis=1).astype(BF16)


def _layout_mla(w_uq, w_ukv):
    zq = jnp.zeros((w_uq.shape[0], LANES - MLA_NOPE - MLA_ROPE), w_uq.dtype)
    zr = jnp.zeros((w_uq.shape[0], MLA_NOPE), w_uq.dtype)
    qd = MLA_NOPE + MLA_ROPE
    plain, swapped = [], []
    for h in range(MLA_HEADS):
        nope = w_uq[:, h * qd:h * qd + MLA_NOPE]
        rope = w_uq[:, h * qd + MLA_NOPE:(h + 1) * qd]
        plain += [nope, rope, zq]
        swapped += [zr, _swap_cols(rope, MLA_ROPE // 2), zq]
    wq = jnp.concatenate(plain + swapped, axis=1).astype(BF16)
    zk = jnp.zeros((w_ukv.shape[0], LANES - MLA_NOPE), w_ukv.dtype)
    kd = MLA_NOPE + MLA_V
    kparts = []
    vparts = []
    for h in range(MLA_HEADS):
        kparts += [w_ukv[:, h * kd:h * kd + MLA_NOPE], zk]
        vparts += [w_ukv[:, h * kd + MLA_NOPE:(h + 1) * kd]]
    wkv = jnp.concatenate(kparts + vparts, axis=1).astype(BF16)
    place = np.zeros((MLA_ROPE, MLA_HEADS * LANES), np.float32)
    for h in range(MLA_HEADS):
        place[np.arange(MLA_ROPE), h * LANES + MLA_NOPE + np.arange(MLA_ROPE)] = 1.0
    return wq, wkv, jnp.asarray(place, BF16)


def _rope_tables(S):
    def cs(dim):
        inv_freq = ROPE_THETA ** (-jnp.arange(0, dim, 2, dtype=F32) / dim)
        ang = jnp.arange(S, dtype=F32)[:, None] * inv_freq[None, :]
        return jnp.cos(ang), jnp.sin(ang)

    c, s = cs(MLA_ROPE)
    scale = (MLA_NOPE + MLA_ROPE) ** -0.5
    one = jnp.ones((S, MLA_NOPE), F32)
    zero = jnp.zeros((S, MLA_NOPE), F32)
    pad1 = jnp.ones((S, LANES - MLA_NOPE - MLA_ROPE), F32)
    pad0 = jnp.zeros((S, LANES - MLA_NOPE - MLA_ROPE), F32)
    cq = jnp.concatenate([one, c, c, pad1] * MLA_HEADS, axis=1) * scale
    sq = jnp.concatenate([zero, s, s, pad0] * MLA_HEADS, axis=1) * scale
    r32 = jnp.concatenate([c, c, s, s], axis=1)
    c, s = cs(MOBA_ROT)
    one = jnp.ones((S, MOBA_HD - MOBA_ROT), F32)
    zero = jnp.zeros((S, MOBA_HD - MOBA_ROT), F32)
    cm = jnp.concatenate([c, c, one] * MOBA_HEADS, axis=1)
    sm = jnp.concatenate([s, s, zero] * MOBA_HEADS, axis=1)
    return cq, sq, r32, cm, sm


def _block_diag(w):
    n, c, d = w.shape
    out = jnp.zeros((n * c, n * d), w.dtype)
    for j in range(n):
        out = out.at[j * c:(j + 1) * c, j * d:(j + 1) * d].set(w[j])
    return out


def kernel(x, attn_norm, w_in, mla_q_norm, mla_w_uq, mla_kv_norm, mla_w_ukv, gla_w_a2, gla_b_a2, gla_head_norm,
           lru_conv_w, lru_conv_b, lru_w_a, lru_b_a, lru_w_x, lru_b_x, lru_lambda, w_out, ffn_norm, w_ffn_in,
           w_ffn_out, final_norm):
    B, S, D = x.shape
    depth = w_in.shape[0]
    T = B * S
    assert D == D_MODEL and S % 512 == 0 and S // MOBA_BLOCK <= LANES
    tm = 512
    cq, sq, r32, cm, sm = _rope_tables(S)
    row = lambda v: v.reshape(1, -1).astype(F32)
    xt = x.reshape(T, D)
    for l in range(depth):
        last = l == depth - 1
        p_mla, p_gla, p_moba, p_lru = _inproj(xt, row(attn_norm[l]), _layout_w_in(w_in[l]), tm)

        wq, wkv, place = _layout_mla(mla_w_uq[l], mla_w_ukv[l])
        mq, mk, mv = _mla_prep(p_mla, row(mla_q_norm[l]), wq, row(mla_kv_norm[l]), wkv, place, cq, sq, r32, S, tm)
        y_a = _mla_attn(mq, mk, mv, B, S, 256)

        wa = jnp.zeros((LANES, GLA_HEADS * GLA_DK), F32).at[:GLA_GATE_RANK].set(gla_w_a2[l]).astype(BF16)
        hn = jnp.tile(row(gla_head_norm[l]), (1, GLA_HEADS))
        y_b = _gla(p_gla, wa, row(gla_b_a2[l]), hn, B, S, tm)

        bq, bk, bv, sel = _moba_prep(p_moba, cm, sm, B, S, tm)
        y_c = _moba_attn(bq, bk, bv, sel, B, S)

        wax = jnp.concatenate([_block_diag(lru_w_a[l]), _block_diag(lru_w_x[l])], axis=1).astype(BF16)
        bax = jnp.concatenate([row(lru_b_a[l]), row(lru_b_x[l])], axis=1)
        y_d = _lru(p_lru, lru_conv_w[l].reshape(LRU_CONV, LRU_WIDTH), row(lru_conv_b[l]), wax, bax,
                   row(lru_lambda[l]), B, S, tm)

        xt = _ffn(xt, y_a, y_b, y_c, y_d, w_out[l].astype(BF16), row(ffn_norm[l]), w_ffn_in[l].astype(BF16),
                  w_ffn_out[l].astype(BF16), row(final_norm), last, tm, FFN_HIDDEN // 2)
    return xt.reshape(B, S, D)
```

```python
import functools

import numpy as np
import jax
import jax.numpy as jnp
from jax import lax
from jax.experimental import pallas as pl
from jax.experimental.pallas import tpu as pltpu

F32 = jnp.float32
BF16 = jnp.bfloat16

D_MODEL = 1024
GROUP_WIDTH = 256
ROPE_THETA = 500000.0
NORM_EPS = 1e-6
MLA_HEADS = 4
MLA_V = 64
MLA_NOPE = 64
MLA_ROPE = 32
MLA_Q_RANK = 256
MLA_KV_RANK = 128
GLA_HEADS = 4
GLA_DV = 64
GLA_DK = 32
GLA_GATE_RANK = 16
GLA_TAU = 16.0
GLA_CHUNK = 64
MOBA_HEADS = 4
MOBA_HD = 64
MOBA_ROT = 16
MOBA_BLOCK = 256
MOBA_TOPK = 3
LRU_WIDTH = 256
LRU_BLOCKS = 4
LRU_BW = 64
LRU_CONV = 4
LRU_C = 8.0
FFN_HIDDEN = 2816

LANES = 128
SUBLANES = 8
VMEM_LIMIT = 56 * 1024 * 1024

W_MLA = 512
W_GLA = 896
W_MOBA = 1280
W_LRU = 512
NEG_BIG = -1e30
LOG2E = 1.4426950408889634


def _rms(x, g):
    return x * lax.rsqrt(jnp.mean(x * x, axis=-1, keepdims=True) + NORM_EPS) * g


def _dot(a, b):
    return jnp.dot(a, b, preferred_element_type=F32)


def _dot_nt(a, b):
    return lax.dot_general(a, b, (((1,), (1,)), ((), ())), preferred_element_type=F32)


def _dot_tn(a, b):
    return lax.dot_general(a, b, (((0,), (0,)), ((), ())), preferred_element_type=F32)


def _const_spec(shape):
    nd = len(shape)
    return pl.BlockSpec(shape, lambda *_: (0,) * nd)


def _params(sem):
    return pltpu.CompilerParams(dimension_semantics=sem, vmem_limit_bytes=VMEM_LIMIT)


def _inproj_kernel(x_ref, g_ref, w_ref, mla_ref, gla_ref, moba_ref, lru_ref):
    h = _rms(x_ref[...], g_ref[...]).astype(BF16)
    off = 0
    for ref, width in ((mla_ref, W_MLA), (gla_ref, W_GLA), (moba_ref, W_MOBA), (lru_ref, W_LRU)):
        ref[...] = _dot(h, w_ref[:, off:off + width])
        off += width


def _inproj(x, g, w, tm):
    T = x.shape[0]
    nw = w.shape[1]
    return pl.pallas_call(
        _inproj_kernel,
        grid=(T // tm,),
        in_specs=[pl.BlockSpec((tm, D_MODEL), lambda i: (i, 0)), _const_spec((1, D_MODEL)),
                  _const_spec((D_MODEL, nw))],
        out_specs=[pl.BlockSpec((tm, wd), lambda i: (i, 0)) for wd in (W_MLA, W_GLA, W_MOBA, W_LRU)],
        out_shape=[jax.ShapeDtypeStruct((T, wd), F32) for wd in (W_MLA, W_GLA, W_MOBA, W_LRU)],
        compiler_params=_params(("parallel",)),
        name="inproj",
    )(x, g, w)


def _mla_prep_kernel(p_ref, qg_ref, wq_ref, kvg_ref, wkv_ref, place_ref, cq_ref, sq_ref, r32_ref,
                     q_ref, k_ref, v_ref):
    p = p_ref[...]
    nq = _rms(p[:, :MLA_Q_RANK], qg_ref[...]).astype(BF16)
    q2 = _dot(nq, wq_ref[...])
    hw = MLA_HEADS * LANES
    q_ref[...] = (q2[:, :hw] * cq_ref[...] + q2[:, hw:] * sq_ref[...]).astype(BF16)
    nkv = _rms(p[:, MLA_Q_RANK:MLA_Q_RANK + MLA_KV_RANK], kvg_ref[...]).astype(BF16)
    kv = _dot(nkv, wkv_ref[...])
    c0 = MLA_Q_RANK + MLA_KV_RANK
    kr = p[:, c0:c0 + MLA_ROPE]
    kr_sw = p[:, c0 + MLA_ROPE:c0 + 2 * MLA_ROPE]
    r32 = r32_ref[...]
    k_pe = kr * r32[:, :MLA_ROPE] + kr_sw * r32[:, MLA_ROPE:]
    k_ref[...] = (kv[:, :hw] + _dot(k_pe.astype(BF16), place_ref[...])).astype(BF16)
    v_ref[...] = kv[:, hw:].astype(BF16)


def _mla_prep(p_mla, qg, wq, kvg, wkv, place, cq, sq, r32, S, tm):
    T = p_mla.shape[0]
    ns = S // tm
    hw = MLA_HEADS * LANES
    return pl.pallas_call(
        _mla_prep_kernel,
        grid=(T // tm,),
        in_specs=[pl.BlockSpec((tm, W_MLA), lambda i: (i, 0)),
                  _const_spec(qg.shape), _const_spec(wq.shape), _const_spec(kvg.shape),
                  _const_spec(wkv.shape), _const_spec(place.shape),
                  pl.BlockSpec((tm, hw), lambda i: (i % ns, 0)),
                  pl.BlockSpec((tm, hw), lambda i: (i % ns, 0)),
                  pl.BlockSpec((tm, 2 * MLA_ROPE), lambda i: (i % ns, 0))],
        out_specs=[pl.BlockSpec((tm, hw), lambda i: (i, 0)), pl.BlockSpec((tm, hw), lambda i: (i, 0)),
                   pl.BlockSpec((tm, GROUP_WIDTH), lambda i: (i, 0))],
        out_shape=[jax.ShapeDtypeStruct((T, hw), BF16), jax.ShapeDtypeStruct((T, hw), BF16),
                   jax.ShapeDtypeStruct((T, GROUP_WIDTH), BF16)],
        compiler_params=_params(("parallel",)),
        name="mla_prep",
    )(p_mla, qg, wq, kvg, wkv, place, cq, sq, r32)


ATTN_HEADS = 4
ATTN_TQ = 512
ATTN_TK = 512


def _attn_kernel(q_ref, k_ref, v_ref, o_ref, m_ref, l_ref, acc_ref, *, tq, tk):
    i = pl.program_id(1)
    m_ref[...] = jnp.full(m_ref.shape, NEG_BIG, F32)
    l_ref[...] = jnp.zeros(l_ref.shape, F32)
    acc_ref[...] = jnp.zeros(acc_ref.shape, F32)

    def tile(j, diagonal):
        rows = pl.ds(pl.multiple_of(j * tk, tk), tk)
        if diagonal:
            kpos = j * tk + lax.broadcasted_iota(jnp.int32, (tq, tk), 1)
            qpos = i * tq + lax.broadcasted_iota(jnp.int32, (tq, tk), 0)
            allowed = kpos <= qpos
        for h in range(ATTN_HEADS):
            q = q_ref[:, h * LANES:(h + 1) * LANES]
            k = k_ref[rows, h * LANES:(h + 1) * LANES]
            v = v_ref[rows, (h // 2) * LANES:(h // 2 + 1) * LANES]
            s = _dot_nt(q, k)
            if diagonal:
                s = jnp.where(allowed, s, NEG_BIG)
            m_prev = m_ref[h]
            m_next = jnp.maximum(m_prev, jnp.max(s, axis=-1, keepdims=True))
            alpha = jnp.exp2(m_prev - m_next)
            p = jnp.exp2(s - jnp.tile(m_next, (1, tk // LANES)))
            l_ref[h] = alpha * l_ref[h] + jnp.sum(p, axis=-1, keepdims=True)
            acc_ref[h] = alpha * acc_ref[h] + _dot(p.astype(BF16), v)
            m_ref[h] = m_next

    n_full = (i * tq) // tk

    def body(j, _):
        tile(j, False)
        return 0

    lax.fori_loop(0, n_full, body, 0)
    for d in range(tq // tk):
        tile(n_full + d, True)
    lane = lax.broadcasted_iota(jnp.int32, (tq, LANES), 1)
    for pr in range(ATTN_HEADS // 2):
        oa = acc_ref[2 * pr] / l_ref[2 * pr]
        ob = acc_ref[2 * pr + 1] / l_ref[2 * pr + 1]
        o_ref[:, pr * LANES:(pr + 1) * LANES] = jnp.where(lane < LANES // 2, oa, ob).astype(o_ref.dtype)


def _attn(q, k, v, B, S, tq, tk, name):
    T = q.shape[0]
    nq = S // tq
    hw = ATTN_HEADS * LANES
    stat = pltpu.VMEM((ATTN_HEADS, tq, LANES), F32)
    return pl.pallas_call(
        functools.partial(_attn_kernel, tq=tq, tk=tk),
        grid=(B, nq),
        in_specs=[pl.BlockSpec((tq, hw), lambda b, i: (b * nq + i, 0)),
                  pl.BlockSpec((S, hw), lambda b, i: (b, 0)),
                  pl.BlockSpec((S, GROUP_WIDTH), lambda b, i: (b, 0))],
        out_specs=pl.BlockSpec((tq, GROUP_WIDTH), lambda b, i: (b * nq + i, 0)),
        out_shape=jax.ShapeDtypeStruct((T, GROUP_WIDTH), BF16),
        scratch_shapes=[stat, stat, stat],
        compiler_params=_params(("parallel", "arbitrary")),
        name=name,
    )(q, k, v)


def _gla_kernel(p_ref, wa_ref, ba_ref, hn_ref, o_ref, state_ref, *, tm):
    C = GLA_CHUNK
    hk = GLA_HEADS * GLA_DK
    hv = GLA_HEADS * GLA_DV

    @pl.when(pl.program_id(1) == 0)
    def _():
        state_ref[...] = jnp.zeros_like(state_ref)

    rc = lax.broadcasted_iota(jnp.int32, (C, hk), 0)
    lane_k = lax.broadcasted_iota(jnp.int32, (C, hk), 1)
    lane_v = lax.broadcasted_iota(jnp.int32, (C, hv), 1)
    row4 = lax.broadcasted_iota(jnp.int32, (GLA_HEADS * C, C), 0)
    col4 = lax.broadcasted_iota(jnp.int32, (GLA_HEADS * C, C), 1)
    causal4 = (row4 % C) >= col4
    sr = lax.broadcasted_iota(jnp.int32, (hk, hv), 0)
    sc = lax.broadcasted_iota(jnp.int32, (hk, hv), 1)
    blockdiag = (sr // GLA_DK) == (sc // GLA_DV)
    er = lax.broadcasted_iota(jnp.int32, (hk, hk), 0)
    ec = lax.broadcasted_iota(jnp.int32, (hk, hk), 1)
    eye = er == ec
    wa = wa_ref[...]
    ba = ba_ref[...]
    hn = hn_ref[...]
    scale = GLA_DK ** -0.5

    def chunk(c, _):
        rows = pl.ds(pl.multiple_of(c * C, C), C)
        q = p_ref[rows, 0:hk]
        k = p_ref[rows, hk:2 * hk]
        v = p_ref[rows, 2 * hk:2 * hk + hv]
        g = p_ref[rows, 2 * hk + hv:2 * hk + 2 * hv]
        a_low = p_ref[rows, 2 * hk + 2 * hv:2 * hk + 2 * hv + LANES]
        a_lin = _dot(a_low.astype(BF16), wa) + ba
        la = (jnp.minimum(a_lin, 0.0) - jnp.log1p(jnp.exp(-jnp.abs(a_lin)))) * (1.0 / GLA_TAU)
        b = la
        for s in (1, 2, 4, 8, 16, 32):
            b = b + jnp.where(rc >= s, pltpu.roll(b, s, 0), 0.0)
        bl = b[C - 1:C, :]
        bref = 0.5 * bl
        qf = q * scale
        qs = qf * jnp.exp(b - bref)
        ks = (k * jnp.exp(bref - b)).astype(BF16)
        kd = (k * jnp.exp(bl - b)).astype(BF16)
        qe = (qf * jnp.exp(b)).astype(BF16)
        vb = v.astype(BF16)
        qs4 = jnp.concatenate(
            [jnp.where((lane_k // GLA_DK) == h, qs, 0.0) for h in range(GLA_HEADS)], axis=0).astype(BF16)
        att = jnp.where(causal4, _dot_nt(qs4, ks), 0.0)
        res = _dot(att.astype(BF16), vb)
        o = _dot(qe, state_ref[...].astype(BF16))
        for h in range(GLA_HEADS):
            o = o + jnp.where((lane_v // GLA_DV) == h, res[h * C:(h + 1) * C, :], 0.0)
        dst = _dot_tn(kd, vb)
        decay_row = jnp.exp(bl)
        decay_col = jnp.sum(jnp.where(eye, jnp.broadcast_to(decay_row, (hk, hk)), 0.0), axis=1, keepdims=True)
        state_ref[...] = decay_col * state_ref[...] + jnp.where(blockdiag, dst, 0.0)
        osq = o * o
        inv = jnp.zeros_like(o)
        for h in range(GLA_HEADS):
            mh = (lane_v // GLA_DV) == h
            ms = jnp.sum(jnp.where(mh, osq, 0.0), axis=-1, keepdims=True) * (1.0 / GLA_DV)
            inv = jnp.where(mh, lax.rsqrt(ms + NORM_EPS), inv)
        sig = 1.0 / (1.0 + jnp.exp(-g))
        o_ref[rows, :] = (o * inv * hn * (g * sig)).astype(o_ref.dtype)
        return 0

    lax.fori_loop(0, tm // C, chunk, 0)


def _gla(p_gla, wa, ba, hn, B, S, tm):
    T = p_gla.shape[0]
    ns = S // tm
    return pl.pallas_call(
        functools.partial(_gla_kernel, tm=tm),
        grid=(B, ns),
        in_specs=[pl.BlockSpec((tm, W_GLA), lambda b, s: (b * ns + s, 0)),
                  _const_spec(wa.shape), _const_spec(ba.shape), _const_spec(hn.shape)],
        out_specs=pl.BlockSpec((tm, GROUP_WIDTH), lambda b, s: (b * ns + s, 0)),
        out_shape=jax.ShapeDtypeStruct((T, GROUP_WIDTH), BF16),
        scratch_shapes=[pltpu.VMEM((GLA_HEADS * GLA_DK, GLA_HEADS * GLA_DV), F32)],
        compiler_params=_params(("arbitrary", "arbitrary")),
        name="gla",
    )(p_gla, wa, ba, hn)


def _moba_prep_kernel(p_ref, cm_ref, sm_ref, kone_ref, q_ref, k_ref, v_ref, km_ref, *, tm):
    W = GROUP_WIDTH
    HD = MOBA_HD
    si = pl.program_id(1)

    @pl.when(si == 0)
    def _():
        km_ref[...] = jnp.zeros_like(km_ref)

    cm = cm_ref[...]
    sm = sm_ref[...]
    q = p_ref[:, 0:W] * cm + p_ref[:, 3 * W:4 * W] * sm
    k = p_ref[:, W:2 * W] * cm + p_ref[:, 4 * W:5 * W] * sm
    v_ref[...] = p_ref[:, 2 * W:3 * W].astype(BF16)
    nblk = tm // MOBA_BLOCK
    for j in range(nblk):
        mean = jnp.sum(k[j * MOBA_BLOCK:(j + 1) * MOBA_BLOCK, :], axis=0, keepdims=True) * (1.0 / MOBA_BLOCK)
        km_ref[pl.ds(HD + si * nblk + j, 1), :] = mean
    km = km_ref[...]
    lane_w = lax.broadcasted_iota(jnp.int32, (tm, W), 1)
    lane = lax.broadcasted_iota(jnp.int32, (tm, LANES), 1)
    blk = (si * tm + lax.broadcasted_iota(jnp.int32, (tm, LANES), 0)) // MOBA_BLOCK
    n_idx = lane - HD
    past = (n_idx >= 0) & (n_idx < blk)
    own = n_idx == blk
    n_f = n_idx.astype(F32)
    head_lanes = lane < HD
    kone = kone_ref[...]
    q_scaled = q * (MOBA_HD ** -0.5 * LOG2E)
    for h in range(MOBA_HEADS):
        qh = jnp.where((lane_w // HD) == h, q, 0.0)
        gate = lax.dot_general(qh, km, (((1,), (1,)), ((), ())), preferred_element_type=F32,
                               precision=lax.Precision.HIGHEST)
        gate = jnp.where(past, gate, -jnp.inf)
        keep = own
        for _ in range(MOBA_TOPK):
            mx = jnp.max(gate, axis=-1, keepdims=True)
            first = jnp.min(jnp.where(gate == mx, n_f, float(LANES)), axis=-1, keepdims=True)
            pick = (n_f == first) & (mx > -jnp.inf)
            keep = keep | pick
            gate = jnp.where(pick, -jnp.inf, gate)
        mask_lanes = jnp.where(keep, 0.0, NEG_BIG)
        pr = h // 2
        q_pair = q_scaled[:, pr * LANES:(pr + 1) * LANES]
        k_pair = k[:, pr * LANES:(pr + 1) * LANES]
        if h % 2:
            q_pair = pltpu.roll(q_pair, HD, 1)
            k_pair = pltpu.roll(k_pair, HD, 1)
        q_ref[:, h * LANES:(h + 1) * LANES] = jnp.where(head_lanes, q_pair, mask_lanes).astype(BF16)
        k_ref[:, h * LANES:(h + 1) * LANES] = jnp.where(head_lanes, k_pair, kone).astype(BF16)


def _moba_prep(p_moba, cm, sm, kone, B, S, tm):
    T = p_moba.shape[0]
    ns = S // tm
    W = GROUP_WIDTH
    hw = MOBA_HEADS * LANES
    tok = lambda b, s: (b * ns + s, 0)
    pos = lambda b, s: (s, 0)
    return pl.pallas_call(
        functools.partial(_moba_prep_kernel, tm=tm),
        grid=(B, ns),
        in_specs=[pl.BlockSpec((tm, W_MOBA), tok), pl.BlockSpec((tm, W), pos), pl.BlockSpec((tm, W), pos),
                  pl.BlockSpec((tm, LANES), pos)],
        out_specs=[pl.BlockSpec((tm, hw), tok), pl.BlockSpec((tm, hw), tok), pl.BlockSpec((tm, W), tok)],
        out_shape=[jax.ShapeDtypeStruct((T, hw), BF16), jax.ShapeDtypeStruct((T, hw), BF16),
                   jax.ShapeDtypeStruct((T, W), BF16)],
        scratch_shapes=[pltpu.VMEM((LANES, W), F32)],
        compiler_params=_params(("arbitrary", "arbitrary")),
        name="moba_prep",
    )(p_moba, cm, sm, kone)


def _lru_kernel(p_ref, cw_ref, cb_ref, wax_ref, bax_ref, lam_ref, o_ref, tail_ref, h_ref, *, tm):
    W = LRU_WIDTH

    @pl.when(pl.program_id(1) == 0)
    def _():
        tail_ref[...] = jnp.zeros_like(tail_ref)
        h_ref[...] = jnp.zeros_like(h_ref)

    x = p_ref[:, 0:W]
    gate = p_ref[:, W:2 * W]
    xx = jnp.concatenate([tail_ref[...], x], axis=0)
    cw = cw_ref[...]
    xc = cb_ref[...] + cw[LRU_CONV - 1:LRU_CONV, :] * x
    for d in range(1, LRU_CONV):
        xc = xc + cw[LRU_CONV - 1 - d:LRU_CONV - d, :] * xx[SUBLANES - d:SUBLANES - d + tm, :]
    tail_ref[...] = x[tm - SUBLANES:, :]
    ri = _dot(xc.astype(BF16), wax_ref[...]) + bax_ref[...]
    r = 1.0 / (1.0 + jnp.exp(-ri[:, :W]))
    ig = 1.0 / (1.0 + jnp.exp(-ri[:, W:]))
    lam = lam_ref[...]
    softplus_neg = jnp.maximum(-lam, 0.0) + jnp.log1p(jnp.exp(-jnp.abs(lam)))
    log_a = (-LRU_C * r) * softplus_neg
    a = jnp.exp(log_a)
    u = jnp.sqrt(-jnp.tanh(log_a) * (a * a + 1.0)) * (ig * xc)
    row = lax.broadcasted_iota(jnp.int32, (tm, W), 0)
    s = 1
    while s < tm:
        live = row >= s
        u = u + a * jnp.where(live, pltpu.roll(u, s, 0), 0.0)
        a = a * jnp.where(live, pltpu.roll(a, s, 0), 1.0)
        s *= 2
    h = u + a * h_ref[...]
    h_ref[...] = h[tm - 1:tm, :]
    gelu = 0.5 * gate * (1.0 + jnp.tanh(0.7978845608028654 * (gate + 0.044715 * gate * gate * gate)))
    o_ref[...] = (h * gelu).astype(o_ref.dtype)


def _lru(p_lru, cw, cb, wax, bax, lam, B, S, tm):
    T = p_lru.shape[0]
    ns = S // tm
    tok = lambda b, s: (b * ns + s, 0)
    return pl.pallas_call(
        functools.partial(_lru_kernel, tm=tm),
        grid=(B, ns),
        in_specs=[pl.BlockSpec((tm, W_LRU), tok), _const_spec(cw.shape), _const_spec(cb.shape),
                  _const_spec(wax.shape), _const_spec(bax.shape), _const_spec(lam.shape)],
        out_specs=pl.BlockSpec((tm, LRU_WIDTH), tok),
        out_shape=jax.ShapeDtypeStruct((T, LRU_WIDTH), BF16),
        scratch_shapes=[pltpu.VMEM((SUBLANES, LRU_WIDTH), F32), pltpu.VMEM((1, LRU_WIDTH), F32)],
        compiler_params=_params(("arbitrary", "arbitrary")),
        name="lru",
    )(p_lru, cw, cb, wax, bax, lam)


def _ffn_kernel(x_ref, ya_ref, yb_ref, yc_ref, yd_ref, wo_ref, fg_ref, wi_ref, w2_ref, ng_ref, o_ref,
                *, final, chunk):
    W = GROUP_WIDTH
    x1 = x_ref[...]
    for n, y_ref in enumerate((ya_ref, yb_ref, yc_ref, yd_ref)):
        x1 = x1 + _dot(y_ref[...], wo_ref[n * W:(n + 1) * W, :])
    h = _rms(x1, fg_ref[...]).astype(BF16)
    parts = []
    for c in range(FFN_HIDDEN // chunk):
        g = _dot(h, wi_ref[:, c * chunk:(c + 1) * chunk])
        up = _dot(h, wi_ref[:, FFN_HIDDEN + c * chunk:FFN_HIDDEN + (c + 1) * chunk])
        act = (g * (1.0 / (1.0 + jnp.exp(-g))) * up).astype(BF16)
        parts.append(_dot(act, w2_ref[c * chunk:(c + 1) * chunk, :]))
    acc = x1 + sum(parts[1:], parts[0])
    if final:
        acc = _rms(acc, ng_ref[...])
    o_ref[...] = acc


def _ffn(x, ya, yb, yc, yd, wo, fg, wi, w2, ng, final, tm, chunk):
    T = x.shape[0]
    tok = lambda i: (i, 0)
    ysp = pl.BlockSpec((tm, GROUP_WIDTH), tok)
    return pl.pallas_call(
        functools.partial(_ffn_kernel, final=final, chunk=chunk),
        grid=(T // tm,),
        in_specs=[pl.BlockSpec((tm, D_MODEL), tok), ysp, ysp, ysp, ysp,
                  _const_spec(wo.shape), _const_spec(fg.shape), _const_spec(wi.shape),
                  _const_spec(w2.shape), _const_spec(ng.shape)],
        out_specs=pl.BlockSpec((tm, D_MODEL), tok),
        out_shape=jax.ShapeDtypeStruct((T, D_MODEL), F32),
        compiler_params=_params(("parallel",)),
        name="outproj_ffn",
    )(x, ya, yb, yc, yd, wo, fg, wi, w2, ng)


def _swap_cols(w, half):
    return jnp.concatenate([-w[:, half:2 * half], w[:, :half]], axis=1)


def _layout_w_in(w):
    sp = np.cumsum([0, 256, 128, 32, 128, 128, 256, 16, 256, 256, 256, 256, 256, 256])
    col = lambda n: w[:, sp[n]:sp[n + 1]]
    mq, mkv, mkr, gq, gk, gv, ga, gg, bq, bk, bv, rx, rg = [col(n) for n in range(13)]
    z = lambda n: jnp.zeros((w.shape[0], n), w.dtype)

    def moba_sw(m):
        parts = []
        half = MOBA_ROT // 2
        for h in range(MOBA_HEADS):
            parts += [_swap_cols(m[:, h * MOBA_HD:h * MOBA_HD + MOBA_ROT], half), z(MOBA_HD - MOBA_ROT)]
        return jnp.concatenate(parts, axis=1)

    mla = [mq, mkv, mkr, _swap_cols(mkr, MLA_ROPE // 2), z(W_MLA - 256 - 128 - 64)]
    gla = [gq, gk, gv, gg, ga, z(LANES - GLA_GATE_RANK)]
    moba = [bq, bk, bv, moba_sw(bq), moba_sw(bk)]
    return jnp.concatenate(mla + gla + moba + [rx, rg], axis=1).astype(BF16)


def _layout_mla(w_uq, w_ukv):
    zq = jnp.zeros((w_uq.shape[0], LANES - MLA_NOPE - MLA_ROPE), w_uq.dtype)
    zr = jnp.zeros((w_uq.shape[0], MLA_NOPE), w_uq.dtype)
    qd = MLA_NOPE + MLA_ROPE
    plain, swapped = [], []
    for h in range(MLA_HEADS):
        nope = w_uq[:, h * qd:h * qd + MLA_NOPE]
        rope = w_uq[:, h * qd + MLA_NOPE:(h + 1) * qd]
        plain += [nope, rope, zq]
        swapped += [zr, _swap_cols(rope, MLA_ROPE // 2), zq]
    wq = jnp.concatenate(plain + swapped, axis=1).astype(BF16)
    zk = jnp.zeros((w_ukv.shape[0], LANES - MLA_NOPE), w_ukv.dtype)
    kd = MLA_NOPE + MLA_V
    kparts = []
    vparts = []
    for h in range(MLA_HEADS):
        kparts += [w_ukv[:, h * kd:h * kd + MLA_NOPE], zk]
        vparts += [w_ukv[:, h * kd + MLA_NOPE:(h + 1) * kd]]
    wkv = jnp.concatenate(kparts + vparts, axis=1).astype(BF16)
    place = np.zeros((MLA_ROPE, MLA_HEADS * LANES), np.float32)
    for h in range(MLA_HEADS):
        place[np.arange(MLA_ROPE), h * LANES + MLA_NOPE + np.arange(MLA_ROPE)] = 1.0
    return wq, wkv, jnp.asarray(place, BF16)


def _rope_tables(S):
    def cs(dim):
        inv_freq = ROPE_THETA ** (-jnp.arange(0, dim, 2, dtype=F32) / dim)
        ang = jnp.arange(S, dtype=F32)[:, None] * inv_freq[None, :]
        return jnp.cos(ang), jnp.sin(ang)

    c, s = cs(MLA_ROPE)
    scale = (MLA_NOPE + MLA_ROPE) ** -0.5 * LOG2E
    one = jnp.ones((S, MLA_NOPE), F32)
    zero = jnp.zeros((S, MLA_NOPE), F32)
    pad1 = jnp.ones((S, LANES - MLA_NOPE - MLA_ROPE), F32)
    pad0 = jnp.zeros((S, LANES - MLA_NOPE - MLA_ROPE), F32)
    cq = jnp.concatenate([one, c, c, pad1] * MLA_HEADS, axis=1) * scale
    sq = jnp.concatenate([zero, s, s, pad0] * MLA_HEADS, axis=1) * scale
    r32 = jnp.concatenate([c, c, s, s], axis=1)
    c, s = cs(MOBA_ROT)
    one = jnp.ones((S, MOBA_HD - MOBA_ROT), F32)
    zero = jnp.zeros((S, MOBA_HD - MOBA_ROT), F32)
    cm = jnp.concatenate([c, c, one] * MOBA_HEADS, axis=1)
    sm = jnp.concatenate([s, s, zero] * MOBA_HEADS, axis=1)
    blk = jnp.arange(S, dtype=jnp.int32)[:, None] // MOBA_BLOCK
    kone = (jnp.arange(LANES, dtype=jnp.int32)[None, :] == blk + MOBA_HD).astype(F32)
    return cq, sq, r32, cm, sm, kone


def _block_diag(w):
    n, c, d = w.shape
    out = jnp.zeros((n * c, n * d), w.dtype)
    for j in range(n):
        out = out.at[j * c:(j + 1) * c, j * d:(j + 1) * d].set(w[j])
    return out


def kernel(x, attn_norm, w_in, mla_q_norm, mla_w_uq, mla_kv_norm, mla_w_ukv, gla_w_a2, gla_b_a2, gla_head_norm,
           lru_conv_w, lru_conv_b, lru_w_a, lru_b_a, lru_w_x, lru_b_x, lru_lambda, w_out, ffn_norm, w_ffn_in,
           w_ffn_out, final_norm):
    B, S, D = x.shape
    depth = w_in.shape[0]
    T = B * S
    assert D == D_MODEL and S % 512 == 0 and S // MOBA_BLOCK <= LANES - MOBA_HD
    tm = 512
    cq, sq, r32, cm, sm, kone = _rope_tables(S)
    row = lambda v: v.reshape(1, -1).astype(F32)
    xt = x.reshape(T, D)
    for l in range(depth):
        last = l == depth - 1
        p_mla, p_gla, p_moba, p_lru = _inproj(xt, row(attn_norm[l]), _layout_w_in(w_in[l]), tm)

        wq, wkv, place = _layout_mla(mla_w_uq[l], mla_w_ukv[l])
        mq, mk, mv = _mla_prep(p_mla, row(mla_q_norm[l]), wq, row(mla_kv_norm[l]), wkv, place, cq, sq, r32, S, tm)
        y_a = _attn(mq, mk, mv, B, S, ATTN_TQ, ATTN_TK, "mla_attn")

        wa = jnp.zeros((LANES, GLA_HEADS * GLA_DK), F32).at[:GLA_GATE_RANK].set(gla_w_a2[l]).astype(BF16)
        hn = jnp.tile(row(gla_head_norm[l]), (1, GLA_HEADS))
        y_b = _gla(p_gla, wa, row(gla_b_a2[l]), hn, B, S, tm)

        bq, bk, bv = _moba_prep(p_moba, cm, sm, kone, B, S, tm)
        y_c = _attn(bq, bk, bv, B, S, ATTN_TQ, ATTN_TK, "moba_attn")

        wax = jnp.concatenate([_block_diag(lru_w_a[l]), _block_diag(lru_w_x[l])], axis=1).astype(BF16)
        bax = jnp.concatenate([row(lru_b_a[l]), row(lru_b_x[l])], axis=1)
        y_d = _lru(p_lru, lru_conv_w[l].reshape(LRU_CONV, LRU_WIDTH), row(lru_conv_b[l]), wax, bax,
                   row(lru_lambda[l]), B, S, tm)

        xt = _ffn(xt, y_a, y_b, y_c, y_d, w_out[l].astype(BF16), row(ffn_norm[l]), w_ffn_in[l].astype(BF16),
                  w_ffn_out[l].astype(BF16), row(final_norm), last, tm, FFN_HIDDEN // 2)
    return xt.reshape(B, S, D)
```

```python
import functools

import numpy as np
import jax
import jax.numpy as jnp
from jax import lax
from jax.experimental import pallas as pl
from jax.experimental.pallas import tpu as pltpu

F32 = jnp.float32
BF16 = jnp.bfloat16

D_MODEL = 1024
GROUP_WIDTH = 256
ROPE_THETA = 500000.0
NORM_EPS = 1e-6
MLA_HEADS = 4
MLA_V = 64
MLA_NOPE = 64
MLA_ROPE = 32
MLA_Q_RANK = 256
MLA_KV_RANK = 128
GLA_HEADS = 4
GLA_DV = 64
GLA_DK = 32
GLA_GATE_RANK = 16
GLA_TAU = 16.0
GLA_CHUNK = 64
MOBA_HEADS = 4
MOBA_HD = 64
MOBA_ROT = 16
MOBA_BLOCK = 256
MOBA_TOPK = 3
LRU_WIDTH = 256
LRU_BLOCKS = 4
LRU_BW = 64
LRU_CONV = 4
LRU_C = 8.0
FFN_HIDDEN = 2816

LANES = 128
SUBLANES = 8
VMEM_LIMIT = 56 * 1024 * 1024

W_MLA = 512
W_GLA = 896
W_MOBA = 1280
W_LRU = 512
W_IN = W_MLA + W_GLA + W_MOBA + W_LRU
NEG_BIG = -1e30
LOG2E = 1.4426950408889634
MOBA_SLOT = 32
PREP_TM = 512
FFN_TM = 512
ATTN_HEADS = 4
ATTN_TQ = 512
ATTN_TK = 512


def _rms(x, g):
    return x * lax.rsqrt(jnp.mean(x * x, axis=-1, keepdims=True) + NORM_EPS) * g


def _dot(a, b):
    return jnp.dot(a, b, preferred_element_type=F32)


def _dot_nt(a, b):
    return lax.dot_general(a, b, (((1,), (1,)), ((), ())), preferred_element_type=F32)


def _dot_tn(a, b):
    return lax.dot_general(a, b, (((0,), (0,)), ((), ())), preferred_element_type=F32)


def _sigmoid(x):
    return 1.0 / (1.0 + jnp.exp(-x))


def _const_spec(shape):
    nd = len(shape)
    return pl.BlockSpec(shape, lambda *_: (0,) * nd)


def _params(sem):
    return pltpu.CompilerParams(dimension_semantics=sem, vmem_limit_bytes=VMEM_LIMIT)


def _mla_prep(p, qg_ref, wq_ref, kvg_ref, wkv_ref, place_ref, cq_ref, sq_ref, r32_ref, q_ref, k_ref, v_ref):
    hw = MLA_HEADS * LANES
    nq = _rms(p[:, :MLA_Q_RANK], qg_ref[...]).astype(BF16)
    q2 = _dot(nq, wq_ref[...])
    q_ref[...] = (q2[:, :hw] * cq_ref[...] + q2[:, hw:] * sq_ref[...]).astype(BF16)
    nkv = _rms(p[:, MLA_Q_RANK:MLA_Q_RANK + MLA_KV_RANK], kvg_ref[...]).astype(BF16)
    kv = _dot(nkv, wkv_ref[...])
    c0 = MLA_Q_RANK + MLA_KV_RANK
    r32 = r32_ref[...]
    k_pe = p[:, c0:c0 + MLA_ROPE] * r32[:, :MLA_ROPE] + p[:, c0 + MLA_ROPE:c0 + 2 * MLA_ROPE] * r32[:, MLA_ROPE:]
    k_ref[...] = (kv[:, :hw] + _dot(k_pe.astype(BF16), place_ref[...])).astype(BF16)
    v_ref[...] = kv[:, hw:].astype(BF16)


def _moba_prep(p, cm_ref, sm_ref, kone_ref, q_ref, k_ref, v_ref, km_ref, si, tm):
    W = GROUP_WIDTH
    HD = MOBA_HD
    nb = km_ref.shape[0]
    cm = cm_ref[...]
    sm = sm_ref[...]
    q = p[:, 0:W] * cm + p[:, 3 * W:4 * W] * sm
    k = p[:, W:2 * W] * cm + p[:, 4 * W:5 * W] * sm
    v_ref[...] = p[:, 2 * W:3 * W].astype(BF16)
    nblk = tm // MOBA_BLOCK
    for j in range(nblk):
        mean = jnp.sum(k[j * MOBA_BLOCK:(j + 1) * MOBA_BLOCK, :], axis=0, keepdims=True) * (1.0 / MOBA_BLOCK)
        km_ref[pl.ds(si * nblk + j, 1), :] = mean
    km = km_ref[...]
    lane_w = lax.broadcasted_iota(jnp.int32, (tm, W), 1)
    n_idx = lax.broadcasted_iota(jnp.int32, (nb, tm), 0)
    blk = (si * tm + lax.broadcasted_iota(jnp.int32, (nb, tm), 1)) // MOBA_BLOCK
    past = n_idx < blk
    own = n_idx == blk
    n_f = n_idx.astype(F32)
    masks = []
    for h in range(MOBA_HEADS):
        qh = jnp.where((lane_w // HD) == h, q, 0.0)
        gate = lax.dot_general(km, qh, (((1,), (1,)), ((), ())), preferred_element_type=F32,
                               precision=lax.Precision.HIGHEST)
        gate = jnp.where(past, gate, -jnp.inf)
        keep = own
        for _ in range(MOBA_TOPK):
            mx = jnp.max(gate, axis=0, keepdims=True)
            first = jnp.min(jnp.where(gate == mx, n_f, float(LANES)), axis=0, keepdims=True)
            pick = (n_f == first) & (mx > -jnp.inf)
            keep = keep | pick
            gate = jnp.where(pick, -jnp.inf, gate)
        masks.append(jnp.where(keep, 0.0, NEG_BIG))
        if nb < MOBA_SLOT:
            masks.append(jnp.zeros((MOBA_SLOT - nb, tm), F32))
    mask_hi = jnp.concatenate(masks, axis=0).T
    mask_lo = pltpu.roll(mask_hi, HD, 1)
    lane = lax.broadcasted_iota(jnp.int32, (tm, LANES), 1)
    head_lanes = lane < HD
    q_scaled = q * (MOBA_HD ** -0.5 * LOG2E)
    kone = kone_ref[...]
    for h in range(MOBA_HEADS):
        pr = h // 2
        q_pair = q_scaled[:, pr * LANES:(pr + 1) * LANES]
        k_pair = k[:, pr * LANES:(pr + 1) * LANES]
        if h % 2:
            q_pair = pltpu.roll(q_pair, HD, 1)
            k_pair = pltpu.roll(k_pair, HD, 1)
        q_ref[:, h * LANES:(h + 1) * LANES] = jnp.where(head_lanes, q_pair, mask_lo if h < 2 else mask_hi).astype(BF16)
        k_ref[:, h * LANES:(h + 1) * LANES] = jnp.where(
            head_lanes, k_pair, kone[:, (h % 2) * LANES:(h % 2 + 1) * LANES]).astype(BF16)


def _gla(p, wa_ref, ba_ref, hn_ref, o_ref, state_ref, tm):
    C = GLA_CHUNK
    hk = GLA_HEADS * GLA_DK
    hv = GLA_HEADS * GLA_DV
    nc = tm // C
    rc = lax.broadcasted_iota(jnp.int32, (tm, hk), 0) % C
    lane_k = lax.broadcasted_iota(jnp.int32, (C, hk), 1)
    lane_v = lax.broadcasted_iota(jnp.int32, (C, hv), 1)
    row4 = lax.broadcasted_iota(jnp.int32, (GLA_HEADS * C, C), 0)
    col4 = lax.broadcasted_iota(jnp.int32, (GLA_HEADS * C, C), 1)
    causal4 = (row4 % C) >= col4
    sr = lax.broadcasted_iota(jnp.int32, (hv, hk), 0)
    sc = lax.broadcasted_iota(jnp.int32, (hv, hk), 1)
    blockdiag = (sr // GLA_DV) == (sc // GLA_DK)
    scale = GLA_DK ** -0.5

    q = p[:, 0:hk]
    k = p[:, hk:2 * hk]
    vb = p[:, 2 * hk:2 * hk + hv].astype(BF16)
    a_low = p[:, 2 * hk + 2 * hv:2 * hk + 2 * hv + LANES]
    a_lin = _dot(a_low.astype(BF16), wa_ref[...]) + ba_ref[...]
    b = (jnp.minimum(a_lin, 0.0) - jnp.log1p(jnp.exp(-jnp.abs(a_lin)))) * (1.0 / GLA_TAU)
    for s in (1, 2, 4, 8, 16, 32):
        b = b + jnp.where(rc >= s, pltpu.roll(b, s, 0), 0.0)
    last = [b[c * C + C - 1:(c + 1) * C, :] for c in range(nc)]
    bl = jnp.concatenate([jnp.broadcast_to(r, (C, hk)) for r in last], axis=0)
    bref = 0.5 * bl
    qf = q * scale
    qs = qf * jnp.exp(b - bref)
    ks = (k * jnp.exp(bref - b)).astype(BF16)
    kd = (k * jnp.exp(bl - b)).astype(BF16)
    qe = (qf * jnp.exp(b)).astype(BF16)

    intra, incr = [], []
    for c in range(nc):
        sl = slice(c * C, (c + 1) * C)
        qs4 = jnp.concatenate(
            [jnp.where((lane_k // GLA_DK) == h, qs[sl], 0.0) for h in range(GLA_HEADS)], axis=0).astype(BF16)
        att = jnp.where(causal4, _dot_nt(qs4, ks[sl]), 0.0)
        res = _dot(att.astype(BF16), vb[sl])
        o = jnp.where((lane_v // GLA_DV) == 0, res[0:C, :], 0.0)
        for h in range(1, GLA_HEADS):
            o = o + jnp.where((lane_v // GLA_DV) == h, res[h * C:(h + 1) * C, :], 0.0)
        intra.append(o)
        incr.append(jnp.where(blockdiag, _dot_tn(vb[sl], kd[sl]), 0.0))

    st = state_ref[...]
    outs = []
    for c in range(nc):
        outs.append(intra[c] + _dot_nt(qe[c * C:(c + 1) * C], st.astype(BF16)))
        st = st * jnp.exp(last[c]) + incr[c]
    state_ref[...] = st
    o = jnp.concatenate(outs, axis=0)

    lane_t = lax.broadcasted_iota(jnp.int32, (tm, hv), 1)
    osq = o * o
    inv = jnp.zeros_like(o)
    for h in range(GLA_HEADS):
        mh = (lane_t // GLA_DV) == h
        ms = jnp.sum(jnp.where(mh, osq, 0.0), axis=-1, keepdims=True) * (1.0 / GLA_DV)
        inv = jnp.where(mh, lax.rsqrt(ms + NORM_EPS), inv)
    g = p[:, 2 * hk + hv:2 * hk + 2 * hv]
    o_ref[...] = (o * inv * hn_ref[...] * (g * _sigmoid(g))).astype(o_ref.dtype)


def _lru(p, cw_ref, cb_ref, wax_ref, bax_ref, lam_ref, o_ref, tail_ref, h_ref, tm):
    W = LRU_WIDTH
    G = SUBLANES
    x = p[:, 0:W]
    gate = p[:, W:2 * W]
    xx = jnp.concatenate([tail_ref[...], x], axis=0)
    cw = cw_ref[...]
    xc = cb_ref[...] + cw[LRU_CONV - 1:LRU_CONV, :] * x
    for d in range(1, LRU_CONV):
        xc = xc + cw[LRU_CONV - 1 - d:LRU_CONV - d, :] * xx[G - d:G - d + tm, :]
    tail_ref[...] = x[tm - G:, :]
    ri = _dot(xc.astype(BF16), wax_ref[...]) + bax_ref[...]
    r = _sigmoid(ri[:, :W])
    ig = _sigmoid(ri[:, W:])
    lam = lam_ref[...]
    softplus_neg = jnp.maximum(-lam, 0.0) + jnp.log1p(jnp.exp(-jnp.abs(lam)))
    log_a = (-LRU_C * r) * softplus_neg
    a = jnp.exp(log_a)
    u = jnp.sqrt(-jnp.tanh(log_a) * (a * a + 1.0)) * (ig * xc)
    sub = lax.broadcasted_iota(jnp.int32, (G, W), 0)
    h = h_ref[...]
    outs = []
    for g in range(tm // G):
        ag = a[g * G:(g + 1) * G, :]
        ug = u[g * G:(g + 1) * G, :]
        for s in (1, 2, 4):
            live = sub >= s
            ug = ug + ag * jnp.where(live, pltpu.roll(ug, s, 0), 0.0)
            ag = ag * jnp.where(live, pltpu.roll(ag, s, 0), 1.0)
        hg = ug + ag * h
        outs.append(hg)
        h = hg[G - 1:G, :]
    h_ref[...] = h
    hs = jnp.concatenate(outs, axis=0)
    gelu = 0.5 * gate * (1.0 + jnp.tanh(0.7978845608028654 * (gate + 0.044715 * gate * gate * gate)))
    o_ref[...] = (hs * gelu).astype(o_ref.dtype)


def _prep_kernel(x_ref, g_ref, w_ref,
                 qg_ref, wq_ref, kvg_ref, wkv_ref, place_ref, cq_ref, sq_ref, r32_ref,
                 cm_ref, sm_ref, kone_ref,
                 wa_ref, ba_ref, hn_ref,
                 cw_ref, cb_ref, wax_ref, bax_ref, lam_ref,
                 mq_ref, mk_ref, mv_ref, bq_ref, bk_ref, bv_ref, yb_ref, yd_ref,
                 km_ref, state_ref, tail_ref, h_ref, *, tm):
    si = pl.program_id(1)

    @pl.when(si == 0)
    def _():
        km_ref[...] = jnp.zeros_like(km_ref)
        state_ref[...] = jnp.zeros_like(state_ref)
        tail_ref[...] = jnp.zeros_like(tail_ref)
        h_ref[...] = jnp.zeros_like(h_ref)

    h = _rms(x_ref[...], g_ref[...]).astype(BF16)
    off = 0
    p_mla = _dot(h, w_ref[:, off:off + W_MLA])
    off += W_MLA
    p_gla = _dot(h, w_ref[:, off:off + W_GLA])
    off += W_GLA
    p_moba = _dot(h, w_ref[:, off:off + W_MOBA])
    off += W_MOBA
    p_lru = _dot(h, w_ref[:, off:off + W_LRU])
    _mla_prep(p_mla, qg_ref, wq_ref, kvg_ref, wkv_ref, place_ref, cq_ref, sq_ref, r32_ref, mq_ref, mk_ref, mv_ref)
    _moba_prep(p_moba, cm_ref, sm_ref, kone_ref, bq_ref, bk_ref, bv_ref, km_ref, si, tm)
    _gla(p_gla, wa_ref, ba_ref, hn_ref, yb_ref, state_ref, tm)
    _lru(p_lru, cw_ref, cb_ref, wax_ref, bax_ref, lam_ref, yd_ref, tail_ref, h_ref, tm)


def _prep(x, consts, tables, B, S, tm):
    T = x.shape[0]
    ns = S // tm
    hw = ATTN_HEADS * LANES
    tok = lambda b, s: (b * ns + s, 0)
    pos = lambda b, s: (s, 0)
    g, w, qg, wq, kvg, wkv, place, wa, ba, hn, cw, cb, wax, bax, lam = consts
    cq, sq, r32, cm, sm, kone = tables
    c = lambda a: _const_spec(a.shape)
    t = lambda a: pl.BlockSpec((tm, a.shape[1]), pos)
    nb = max(2 * SUBLANES, -(-(S // MOBA_BLOCK) // SUBLANES) * SUBLANES)
    outs = [(hw, BF16), (hw, BF16), (GROUP_WIDTH, BF16), (hw, BF16), (hw, BF16), (GROUP_WIDTH, BF16),
            (GROUP_WIDTH, BF16), (GROUP_WIDTH, BF16)]
    return pl.pallas_call(
        functools.partial(_prep_kernel, tm=tm),
        grid=(B, ns),
        in_specs=[pl.BlockSpec((tm, D_MODEL), tok), c(g), c(w),
                  c(qg), c(wq), c(kvg), c(wkv), c(place), t(cq), t(sq), t(r32),
                  t(cm), t(sm), t(kone),
                  c(wa), c(ba), c(hn),
                  c(cw), c(cb), c(wax), c(bax), c(lam)],
        out_specs=[pl.BlockSpec((tm, wd), tok) for wd, _ in outs],
        out_shape=[jax.ShapeDtypeStruct((T, wd), dt) for wd, dt in outs],
        scratch_shapes=[pltpu.VMEM((nb, GROUP_WIDTH), F32),
                        pltpu.VMEM((GLA_HEADS * GLA_DV, GLA_HEADS * GLA_DK), F32),
                        pltpu.VMEM((SUBLANES, LRU_WIDTH), F32), pltpu.VMEM((1, LRU_WIDTH), F32)],
        compiler_params=_params(("arbitrary", "arbitrary")),
        name="mixer_prep",
    )(x, g, w, qg, wq, kvg, wkv, place, cq, sq, r32, cm, sm, kone, wa, ba, hn, cw, cb, wax, bax, lam)


def _attn_kernel(q_ref, k_ref, v_ref, o_ref, m_ref, l_ref, acc_ref, *, tq, tk):
    i = pl.program_id(1)
    m_ref[...] = jnp.full(m_ref.shape, NEG_BIG, F32)
    l_ref[...] = jnp.zeros(l_ref.shape, F32)
    acc_ref[...] = jnp.zeros(acc_ref.shape, F32)

    def tile(j, diagonal):
        rows = pl.ds(pl.multiple_of(j * tk, tk), tk)
        if diagonal:
            kpos = j * tk + lax.broadcasted_iota(jnp.int32, (tq, tk), 1)
            qpos = i * tq + lax.broadcasted_iota(jnp.int32, (tq, tk), 0)
            allowed = kpos <= qpos
        for h in range(ATTN_HEADS):
            q = q_ref[:, h * LANES:(h + 1) * LANES]
            k = k_ref[rows, h * LANES:(h + 1) * LANES]
            v = v_ref[rows, (h // 2) * LANES:(h // 2 + 1) * LANES]
            s = _dot_nt(q, k)
            if diagonal:
                s = jnp.where(allowed, s, NEG_BIG)
            m_prev = m_ref[h]
            m_next = jnp.maximum(m_prev, jnp.max(s, axis=-1, keepdims=True))
            alpha = jnp.exp2(m_prev - m_next)
            p = jnp.exp2(s - jnp.tile(m_next, (1, tk // LANES)))
            l_ref[h] = alpha * l_ref[h] + jnp.sum(p, axis=-1, keepdims=True)
            acc_ref[h] = alpha * acc_ref[h] + _dot(p.astype(BF16), v)
            m_ref[h] = m_next

    n_full = (i * tq) // tk

    def body(j, _):
        tile(j, False)
        return 0

    lax.fori_loop(0, n_full, body, 0)
    for d in range(tq // tk):
        tile(n_full + d, True)
    lane = lax.broadcasted_iota(jnp.int32, (tq, LANES), 1)
    for pr in range(ATTN_HEADS // 2):
        oa = acc_ref[2 * pr] / l_ref[2 * pr]
        ob = acc_ref[2 * pr + 1] / l_ref[2 * pr + 1]
        o_ref[:, pr * LANES:(pr + 1) * LANES] = jnp.where(lane < LANES // 2, oa, ob).astype(o_ref.dtype)


def _attn(q, k, v, B, S, tq, tk, name):
    T = q.shape[0]
    nq = S // tq
    hw = ATTN_HEADS * LANES
    stat = pltpu.VMEM((ATTN_HEADS, tq, LANES), F32)
    return pl.pallas_call(
        functools.partial(_attn_kernel, tq=tq, tk=tk),
        grid=(B, nq),
        in_specs=[pl.BlockSpec((tq, hw), lambda b, i: (b * nq + i, 0)),
                  pl.BlockSpec((S, hw), lambda b, i: (b, 0)),
                  pl.BlockSpec((S, GROUP_WIDTH), lambda b, i: (b, 0))],
        out_specs=pl.BlockSpec((tq, GROUP_WIDTH), lambda b, i: (b * nq + i, 0)),
        out_shape=jax.ShapeDtypeStruct((T, GROUP_WIDTH), BF16),
        scratch_shapes=[stat, stat, stat],
        compiler_params=_params(("parallel", "arbitrary")),
        name=name,
    )(q, k, v)


def _ffn_kernel(x_ref, ya_ref, yb_ref, yc_ref, yd_ref, wo_ref, fg_ref, wi_ref, w2_ref, ng_ref, o_ref,
                *, final, chunk):
    W = GROUP_WIDTH
    x1 = x_ref[...]
    for n, y_ref in enumerate((ya_ref, yb_ref, yc_ref, yd_ref)):
        x1 = x1 + _dot(y_ref[...], wo_ref[n * W:(n + 1) * W, :])
    h = _rms(x1, fg_ref[...]).astype(BF16)
    parts = []
    for c in range(FFN_HIDDEN // chunk):
        g = _dot(h, wi_ref[:, c * chunk:(c + 1) * chunk])
        up = _dot(h, wi_ref[:, FFN_HIDDEN + c * chunk:FFN_HIDDEN + (c + 1) * chunk])
        act = (g * _sigmoid(g) * up).astype(BF16)
        parts.append(_dot(act, w2_ref[c * chunk:(c + 1) * chunk, :]))
    acc = x1 + sum(parts[1:], parts[0])
    if final:
        acc = _rms(acc, ng_ref[...])
    o_ref[...] = acc


def _ffn(x, ya, yb, yc, yd, wo, fg, wi, w2, ng, final, tm, chunk):
    T = x.shape[0]
    tok = lambda i: (i, 0)
    ysp = pl.BlockSpec((tm, GROUP_WIDTH), tok)
    return pl.pallas_call(
        functools.partial(_ffn_kernel, final=final, chunk=chunk),
        grid=(T // tm,),
        in_specs=[pl.BlockSpec((tm, D_MODEL), tok), ysp, ysp, ysp, ysp,
                  _const_spec(wo.shape), _const_spec(fg.shape), _const_spec(wi.shape),
                  _const_spec(w2.shape), _const_spec(ng.shape)],
        out_specs=pl.BlockSpec((tm, D_MODEL), tok),
        out_shape=jax.ShapeDtypeStruct((T, D_MODEL), F32),
        compiler_params=_params(("parallel",)),
        name="outproj_ffn",
    )(x, ya, yb, yc, yd, wo, fg, wi, w2, ng)


def _swap_cols(w, half):
    return jnp.concatenate([-w[:, half:2 * half], w[:, :half]], axis=1)


def _layout_w_in(w):
    sp = np.cumsum([0, 256, 128, 32, 128, 128, 256, 16, 256, 256, 256, 256, 256, 256])
    col = lambda n: w[:, sp[n]:sp[n + 1]]
    mq, mkv, mkr, gq, gk, gv, ga, gg, bq, bk, bv, rx, rg = [col(n) for n in range(13)]
    z = lambda n: jnp.zeros((w.shape[0], n), w.dtype)

    def moba_sw(m):
        parts = []
        half = MOBA_ROT // 2
        for h in range(MOBA_HEADS):
            parts += [_swap_cols(m[:, h * MOBA_HD:h * MOBA_HD + MOBA_ROT], half), z(MOBA_HD - MOBA_ROT)]
        return jnp.concatenate(parts, axis=1)

    mla = [mq, mkv, mkr, _swap_cols(mkr, MLA_ROPE // 2), z(W_MLA - 256 - 128 - 64)]
    gla = [gq, gk, gv, gg, ga, z(LANES - GLA_GATE_RANK)]
    moba = [bq, bk, bv, moba_sw(bq), moba_sw(bk)]
    return jnp.concatenate(mla + gla + moba + [rx, rg], axis=1).astype(BF16)


def _layout_mla(w_uq, w_ukv):
    zq = jnp.zeros((w_uq.shape[0], LANES - MLA_NOPE - MLA_ROPE), w_uq.dtype)
    zr = jnp.zeros((w_uq.shape[0], MLA_NOPE), w_uq.dtype)
    qd = MLA_NOPE + MLA_ROPE
    plain, swapped = [], []
    for h in range(MLA_HEADS):
        nope = w_uq[:, h * qd:h * qd + MLA_NOPE]
        rope = w_uq[:, h * qd + MLA_NOPE:(h + 1) * qd]
        plain += [nope, rope, zq]
        swapped += [zr, _swap_cols(rope, MLA_ROPE // 2), zq]
    wq = jnp.concatenate(plain + swapped, axis=1).astype(BF16)
    zk = jnp.zeros((w_ukv.shape[0], LANES - MLA_NOPE), w_ukv.dtype)
    kd = MLA_NOPE + MLA_V
    kparts = []
    vparts = []
    for h in range(MLA_HEADS):
        kparts += [w_ukv[:, h * kd:h * kd + MLA_NOPE], zk]
        vparts += [w_ukv[:, h * kd + MLA_NOPE:(h + 1) * kd]]
    wkv = jnp.concatenate(kparts + vparts, axis=1).astype(BF16)
    place = np.zeros((MLA_ROPE, MLA_HEADS * LANES), np.float32)
    for h in range(MLA_HEADS):
        place[np.arange(MLA_ROPE), h * LANES + MLA_NOPE + np.arange(MLA_ROPE)] = 1.0
    return wq, wkv, jnp.asarray(place, BF16)


def _tables(S):
    def cs(dim):
        inv_freq = ROPE_THETA ** (-jnp.arange(0, dim, 2, dtype=F32) / dim)
        ang = jnp.arange(S, dtype=F32)[:, None] * inv_freq[None, :]
        return jnp.cos(ang), jnp.sin(ang)

    c, s = cs(MLA_ROPE)
    scale = (MLA_NOPE + MLA_ROPE) ** -0.5 * LOG2E
    one = jnp.ones((S, MLA_NOPE), F32)
    zero = jnp.zeros((S, MLA_NOPE), F32)
    pad1 = jnp.ones((S, LANES - MLA_NOPE - MLA_ROPE), F32)
    pad0 = jnp.zeros((S, LANES - MLA_NOPE - MLA_ROPE), F32)
    cq = jnp.concatenate([one, c, c, pad1] * MLA_HEADS, axis=1) * scale
    sq = jnp.concatenate([zero, s, s, pad0] * MLA_HEADS, axis=1) * scale
    r32 = jnp.concatenate([c, c, s, s], axis=1)
    c, s = cs(MOBA_ROT)
    one = jnp.ones((S, MOBA_HD - MOBA_ROT), F32)
    zero = jnp.zeros((S, MOBA_HD - MOBA_ROT), F32)
    cm = jnp.concatenate([c, c, one] * MOBA_HEADS, axis=1)
    sm = jnp.concatenate([s, s, zero] * MOBA_HEADS, axis=1)
    blk = jnp.arange(S, dtype=jnp.int32)[:, None] // MOBA_BLOCK
    lane = jnp.arange(LANES, dtype=jnp.int32)[None, :]
    kone = jnp.concatenate([(lane == blk + MOBA_HD).astype(F32),
                            (lane == blk + MOBA_HD + MOBA_SLOT).astype(F32)], axis=1)
    return cq, sq, r32, cm, sm, kone


def _block_diag(w):
    n, c, d = w.shape
    out = jnp.zeros((n * c, n * d), w.dtype)
    for j in range(n):
        out = out.at[j * c:(j + 1) * c, j * d:(j + 1) * d].set(w[j])
    return out


def _layer_consts(l, attn_norm, w_in, mla_q_norm, mla_w_uq, mla_kv_norm, mla_w_ukv, gla_w_a2, gla_b_a2,
                  gla_head_norm, lru_conv_w, lru_conv_b, lru_w_a, lru_b_a, lru_w_x, lru_b_x, lru_lambda):
    row = lambda v: v.reshape(1, -1).astype(F32)
    wq, wkv, place = _layout_mla(mla_w_uq[l], mla_w_ukv[l])
    wa = jnp.zeros((LANES, GLA_HEADS * GLA_DK), F32).at[:GLA_GATE_RANK].set(gla_w_a2[l]).astype(BF16)
    hn = jnp.tile(row(gla_head_norm[l]), (1, GLA_HEADS))
    wax = jnp.concatenate([_block_diag(lru_w_a[l]), _block_diag(lru_w_x[l])], axis=1).astype(BF16)
    bax = jnp.concatenate([row(lru_b_a[l]), row(lru_b_x[l])], axis=1)
    return (row(attn_norm[l]), _layout_w_in(w_in[l]), row(mla_q_norm[l]), wq, row(mla_kv_norm[l]), wkv, place,
            wa, row(gla_b_a2[l]), hn, lru_conv_w[l].reshape(LRU_CONV, LRU_WIDTH), row(lru_conv_b[l]), wax, bax,
            row(lru_lambda[l]))


def kernel(x, attn_norm, w_in, mla_q_norm, mla_w_uq, mla_kv_norm, mla_w_ukv, gla_w_a2, gla_b_a2, gla_head_norm,
           lru_conv_w, lru_conv_b, lru_w_a, lru_b_a, lru_w_x, lru_b_x, lru_lambda, w_out, ffn_norm, w_ffn_in,
           w_ffn_out, final_norm):
    B, S, D = x.shape
    depth = w_in.shape[0]
    T = B * S
    assert D == D_MODEL and S % max(PREP_TM, ATTN_TQ, FFN_TM) == 0 and S // MOBA_BLOCK <= MOBA_SLOT
    tables = _tables(S)
    row = lambda v: v.reshape(1, -1).astype(F32)
    xt = x.reshape(T, D)
    for l in range(depth):
        consts = _layer_consts(l, attn_norm, w_in, mla_q_norm, mla_w_uq, mla_kv_norm, mla_w_ukv, gla_w_a2,
                               gla_b_a2, gla_head_norm, lru_conv_w, lru_conv_b, lru_w_a, lru_b_a, lru_w_x,
                               lru_b_x, lru_lambda)
        mq, mk, mv, bq, bk, bv, y_b, y_d = _prep(xt, consts, tables, B, S, PREP_TM)
        y_a = _attn(mq, mk, mv, B, S, ATTN_TQ, ATTN_TK, "mla_attn")
        y_c = _attn(bq, bk, bv, B, S, ATTN_TQ, ATTN_TK, "moba_attn")
        xt = _ffn(xt, y_a, y_b, y_c, y_d, w_out[l].astype(BF16), row(ffn_norm[l]), w_ffn_in[l].astype(BF16),
                  w_ffn_out[l].astype(BF16), row(final_norm), l == depth - 1, FFN_TM, FFN_HIDDEN // 2)
    return xt.reshape(B, S, D)
```

```python
import functools

import numpy as np
import jax
import jax.numpy as jnp
from jax import lax
from jax.experimental import pallas as pl
from jax.experimental.pallas import tpu as pltpu

F32 = jnp.float32
BF16 = jnp.bfloat16

D_MODEL = 1024
GROUP_WIDTH = 256
ROPE_THETA = 500000.0
NORM_EPS = 1e-6
MLA_HEADS = 4
MLA_V = 64
MLA_NOPE = 64
MLA_ROPE = 32
MLA_Q_RANK = 256
MLA_KV_RANK = 128
GLA_HEADS = 4
GLA_DV = 64
GLA_DK = 32
GLA_GATE_RANK = 16
GLA_TAU = 16.0
GLA_CHUNK = 64
MOBA_HEADS = 4
MOBA_HD = 64
MOBA_ROT = 16
MOBA_BLOCK = 256
MOBA_TOPK = 3
LRU_WIDTH = 256
LRU_BLOCKS = 4
LRU_BW = 64
LRU_CONV = 4
LRU_C = 8.0
FFN_HIDDEN = 2816

LANES = 128
SUBLANES = 8
VMEM_LIMIT = 56 * 1024 * 1024

W_MLA = 512
W_GLA = 896
W_MOBA = 1280
W_LRU = 512
W_IN = W_MLA + W_GLA + W_MOBA + W_LRU
NEG_BIG = -1e30
LOG2E = 1.4426950408889634
MOBA_SLOT = 32
PREP_TM = 512
FFN_TM = 512
ATTN_HEADS = 4
ATTN_TQ = 512
ATTN_TK = 512


def _rms(x, g):
    return x * lax.rsqrt(jnp.mean(x * x, axis=-1, keepdims=True) + NORM_EPS) * g


def _dot(a, b):
    return jnp.dot(a, b, preferred_element_type=F32)


def _dot_nt(a, b):
    return lax.dot_general(a, b, (((1,), (1,)), ((), ())), preferred_element_type=F32)


def _dot_tn(a, b):
    return lax.dot_general(a, b, (((0,), (0,)), ((), ())), preferred_element_type=F32)


def _sigmoid(x):
    return 1.0 / (1.0 + jnp.exp(-x))


def _const_spec(shape):
    nd = len(shape)
    return pl.BlockSpec(shape, lambda *_: (0,) * nd)


def _params(sem):
    return pltpu.CompilerParams(dimension_semantics=sem, vmem_limit_bytes=VMEM_LIMIT)


def _mla_prep(p, qg_ref, wq_ref, kvg_ref, wkv_ref, place_ref, cq_ref, sq_ref, r32_ref, qt_ref, k_ref, vt_ref):
    hw = MLA_HEADS * LANES
    nq = _rms(p[:, :MLA_Q_RANK], qg_ref[...]).astype(BF16)
    q2 = _dot(nq, wq_ref[...])
    qt_ref[...] = (q2[:, :hw] * cq_ref[...] + q2[:, hw:] * sq_ref[...]).T.astype(BF16)
    nkv = _rms(p[:, MLA_Q_RANK:MLA_Q_RANK + MLA_KV_RANK], kvg_ref[...]).astype(BF16)
    kv = _dot(nkv, wkv_ref[...])
    c0 = MLA_Q_RANK + MLA_KV_RANK
    r32 = r32_ref[...]
    k_pe = p[:, c0:c0 + MLA_ROPE] * r32[:, :MLA_ROPE] + p[:, c0 + MLA_ROPE:c0 + 2 * MLA_ROPE] * r32[:, MLA_ROPE:]
    k_ref[...] = (kv[:, :hw] + _dot(k_pe.astype(BF16), place_ref[...])).astype(BF16)
    vt_ref[...] = kv[:, hw:].T.astype(BF16)


def _moba_prep(p, cm_ref, sm_ref, kone_ref, qt_ref, k_ref, vt_ref, km_ref, si, tm):
    W = GROUP_WIDTH
    HD = MOBA_HD
    nb = km_ref.shape[0]
    cm = cm_ref[...]
    sm = sm_ref[...]
    q = p[:, 0:W] * cm + p[:, 3 * W:4 * W] * sm
    k = p[:, W:2 * W] * cm + p[:, 4 * W:5 * W] * sm
    vt_ref[...] = p[:, 2 * W:3 * W].T.astype(BF16)
    nblk = tm // MOBA_BLOCK
    for j in range(nblk):
        mean = jnp.sum(k[j * MOBA_BLOCK:(j + 1) * MOBA_BLOCK, :], axis=0, keepdims=True) * (1.0 / MOBA_BLOCK)
        km_ref[pl.ds(si * nblk + j, 1), :] = mean
    km = km_ref[...]
    lane_w = lax.broadcasted_iota(jnp.int32, (tm, W), 1)
    n_idx = lax.broadcasted_iota(jnp.int32, (nb, tm), 0)
    blk = (si * tm + lax.broadcasted_iota(jnp.int32, (nb, tm), 1)) // MOBA_BLOCK
    past = n_idx < blk
    own = n_idx == blk
    n_f = n_idx.astype(F32)
    masks = []
    for h in range(MOBA_HEADS):
        qh = jnp.where((lane_w // HD) == h, q, 0.0)
        gate = lax.dot_general(km, qh, (((1,), (1,)), ((), ())), preferred_element_type=F32,
                               precision=lax.Precision.HIGHEST)
        gate = jnp.where(past, gate, -jnp.inf)
        keep = own
        for _ in range(MOBA_TOPK):
            mx = jnp.max(gate, axis=0, keepdims=True)
            first = jnp.min(jnp.where(gate == mx, n_f, float(LANES)), axis=0, keepdims=True)
            pick = (n_f == first) & (mx > -jnp.inf)
            keep = keep | pick
            gate = jnp.where(pick, -jnp.inf, gate)
        masks.append(jnp.where(keep, 0.0, NEG_BIG))
        if nb < MOBA_SLOT:
            masks.append(jnp.zeros((MOBA_SLOT - nb, tm), F32))
    mask_hi = jnp.concatenate(masks, axis=0).T
    mask_lo = pltpu.roll(mask_hi, HD, 1)
    lane = lax.broadcasted_iota(jnp.int32, (tm, LANES), 1)
    head_lanes = lane < HD
    q_scaled = q * (MOBA_HD ** -0.5 * LOG2E)
    kone = kone_ref[...]
    q_ext = []
    for h in range(MOBA_HEADS):
        pr = h // 2
        q_pair = q_scaled[:, pr * LANES:(pr + 1) * LANES]
        k_pair = k[:, pr * LANES:(pr + 1) * LANES]
        if h % 2:
            q_pair = pltpu.roll(q_pair, HD, 1)
            k_pair = pltpu.roll(k_pair, HD, 1)
        q_ext.append(jnp.where(head_lanes, q_pair, mask_lo if h < 2 else mask_hi))
        k_ref[:, h * LANES:(h + 1) * LANES] = jnp.where(
            head_lanes, k_pair, kone[:, (h % 2) * LANES:(h % 2 + 1) * LANES]).astype(BF16)
    qt_ref[...] = jnp.concatenate(q_ext, axis=1).T.astype(BF16)


def _gla(p, wa_ref, ba_ref, hn_ref, o_ref, state_ref, tm):
    C = GLA_CHUNK
    hk = GLA_HEADS * GLA_DK
    hv = GLA_HEADS * GLA_DV
    nc = tm // C
    rc = lax.broadcasted_iota(jnp.int32, (tm, hk), 0) % C
    lane_k = lax.broadcasted_iota(jnp.int32, (C, hk), 1)
    lane_v = lax.broadcasted_iota(jnp.int32, (C, hv), 1)
    row4 = lax.broadcasted_iota(jnp.int32, (GLA_HEADS * C, C), 0)
    col4 = lax.broadcasted_iota(jnp.int32, (GLA_HEADS * C, C), 1)
    causal4 = (row4 % C) >= col4
    sr = lax.broadcasted_iota(jnp.int32, (hv, hk), 0)
    sc = lax.broadcasted_iota(jnp.int32, (hv, hk), 1)
    blockdiag = (sr // GLA_DV) == (sc // GLA_DK)
    scale = GLA_DK ** -0.5

    q = p[:, 0:hk]
    k = p[:, hk:2 * hk]
    vb = p[:, 2 * hk:2 * hk + hv].astype(BF16)
    a_low = p[:, 2 * hk + 2 * hv:2 * hk + 2 * hv + LANES]
    a_lin = _dot(a_low.astype(BF16), wa_ref[...]) + ba_ref[...]
    b = (jnp.minimum(a_lin, 0.0) - jnp.log1p(jnp.exp(-jnp.abs(a_lin)))) * (1.0 / GLA_TAU)
    for s in (1, 2, 4, 8, 16, 32):
        b = b + jnp.where(rc >= s, pltpu.roll(b, s, 0), 0.0)
    last = [b[c * C + C - 1:(c + 1) * C, :] for c in range(nc)]
    bl = jnp.concatenate([jnp.broadcast_to(r, (C, hk)) for r in last], axis=0)
    bref = 0.5 * bl
    qf = q * scale
    qs = qf * jnp.exp(b - bref)
    ks = (k * jnp.exp(bref - b)).astype(BF16)
    kd = (k * jnp.exp(bl - b)).astype(BF16)
    qe = (qf * jnp.exp(b)).astype(BF16)

    intra, incr = [], []
    for c in range(nc):
        sl = slice(c * C, (c + 1) * C)
        qs4 = jnp.concatenate(
            [jnp.where((lane_k // GLA_DK) == h, qs[sl], 0.0) for h in range(GLA_HEADS)], axis=0).astype(BF16)
        att = jnp.where(causal4, _dot_nt(qs4, ks[sl]), 0.0)
        res = _dot(att.astype(BF16), vb[sl])
        o = jnp.where((lane_v // GLA_DV) == 0, res[0:C, :], 0.0)
        for h in range(1, GLA_HEADS):
            o = o + jnp.where((lane_v // GLA_DV) == h, res[h * C:(h + 1) * C, :], 0.0)
        intra.append(o)
        incr.append(jnp.where(blockdiag, _dot_tn(vb[sl], kd[sl]), 0.0))

    st = state_ref[...]
    outs = []
    for c in range(nc):
        outs.append(intra[c] + _dot_nt(qe[c * C:(c + 1) * C], st.astype(BF16)))
        st = st * jnp.exp(last[c]) + incr[c]
    state_ref[...] = st
    o = jnp.concatenate(outs, axis=0)

    lane_t = lax.broadcasted_iota(jnp.int32, (tm, hv), 1)
    osq = o * o
    inv = jnp.zeros_like(o)
    for h in range(GLA_HEADS):
        mh = (lane_t // GLA_DV) == h
        ms = jnp.sum(jnp.where(mh, osq, 0.0), axis=-1, keepdims=True) * (1.0 / GLA_DV)
        inv = jnp.where(mh, lax.rsqrt(ms + NORM_EPS), inv)
    g = p[:, 2 * hk + hv:2 * hk + 2 * hv]
    o_ref[...] = (o * inv * hn_ref[...] * (g * _sigmoid(g))).astype(o_ref.dtype)


def _lru(p, cw_ref, cb_ref, wax_ref, bax_ref, lam_ref, o_ref, tail_ref, h_ref, tm):
    W = LRU_WIDTH
    G = SUBLANES
    x = p[:, 0:W]
    gate = p[:, W:2 * W]
    xx = jnp.concatenate([tail_ref[...], x], axis=0)
    cw = cw_ref[...]
    xc = cb_ref[...] + cw[LRU_CONV - 1:LRU_CONV, :] * x
    for d in range(1, LRU_CONV):
        xc = xc + cw[LRU_CONV - 1 - d:LRU_CONV - d, :] * xx[G - d:G - d + tm, :]
    tail_ref[...] = x[tm - G:, :]
    ri = _dot(xc.astype(BF16), wax_ref[...]) + bax_ref[...]
    r = _sigmoid(ri[:, :W])
    ig = _sigmoid(ri[:, W:])
    lam = lam_ref[...]
    softplus_neg = jnp.maximum(-lam, 0.0) + jnp.log1p(jnp.exp(-jnp.abs(lam)))
    log_a = (-LRU_C * r) * softplus_neg
    a = jnp.exp(log_a)
    u = jnp.sqrt(-jnp.tanh(log_a) * (a * a + 1.0)) * (ig * xc)
    sub = lax.broadcasted_iota(jnp.int32, (G, W), 0)
    h = h_ref[...]
    outs = []
    for g in range(tm // G):
        ag = a[g * G:(g + 1) * G, :]
        ug = u[g * G:(g + 1) * G, :]
        for s in (1, 2, 4):
            live = sub >= s
            ug = ug + ag * jnp.where(live, pltpu.roll(ug, s, 0), 0.0)
            ag = ag * jnp.where(live, pltpu.roll(ag, s, 0), 1.0)
        hg = ug + ag * h
        outs.append(hg)
        h = hg[G - 1:G, :]
    h_ref[...] = h
    hs = jnp.concatenate(outs, axis=0)
    gelu = 0.5 * gate * (1.0 + jnp.tanh(0.7978845608028654 * (gate + 0.044715 * gate * gate * gate)))
    o_ref[...] = (hs * gelu).astype(o_ref.dtype)


def _w_in_plan():
    sp = np.cumsum([0, 256, 128, 32, 128, 128, 256, 16, 256, 256, 256, 256, 256, 256])
    mq, mkv, mkr, gq, gk, gv, ga, gg, bq, bk, bv, rx, rg = [(int(sp[n]), int(sp[n + 1] - sp[n])) for n in range(13)]
    keep = lambda c: [(c[1], c[0], 1.0)]
    zeros = lambda n: [(n, None, 0.0)]

    def swapped(start, half):
        return [(half, start + half, -1.0), (half, start, 1.0)]

    def moba_swapped(c):
        out = []
        for h in range(MOBA_HEADS):
            out += swapped(c[0] + h * MOBA_HD, MOBA_ROT // 2) + zeros(MOBA_HD - MOBA_ROT)
        return out

    plan = (keep(mq) + keep(mkv) + keep(mkr) + swapped(mkr[0], MLA_ROPE // 2) + zeros(W_MLA - 256 - 128 - 64)
            + keep(gq) + keep(gk) + keep(gv) + keep(gg) + keep(ga) + zeros(LANES - GLA_GATE_RANK)
            + keep(bq) + keep(bk) + keep(bv) + moba_swapped(bq) + moba_swapped(bk)
            + keep(rx) + keep(rg))
    assert sum(p[0] for p in plan) == W_IN
    return plan, int(sp[-1])


def _build_w_in(w_ref, o_ref, rows_per_step=256):
    plan, n_src = _w_in_plan()
    blocks, cur, room = [], [], LANES
    for width, src, sign in plan:
        while width:
            take = min(width, room)
            cur.append((take, src, sign))
            src = None if src is None else src + take
            width -= take
            room -= take
            if room == 0:
                blocks.append(cur)
                cur, room = [], LANES
    for r0 in range(0, w_ref.shape[0], rows_per_step):
        rows = slice(r0, r0 + rows_per_step)
        for j, frags in enumerate(blocks):
            vals = []
            for width, src, sign in frags:
                if src is None:
                    vals.append(jnp.zeros((rows_per_step, width), F32))
                    continue
                a0 = src // LANES * LANES
                a1 = min(-(-(src + width) // LANES) * LANES, n_src)
                win = w_ref[rows, a0:a1]
                v = win[:, src - a0:src - a0 + width]
                vals.append(-v if sign < 0 else v)
            blk = vals[0] if len(vals) == 1 else jnp.concatenate(vals, axis=1)
            o_ref[rows, j * LANES:(j + 1) * LANES] = blk.astype(BF16)


def _prep_kernel(x_ref, g_ref, w_raw_ref,
                 qg_ref, wq_ref, kvg_ref, wkv_ref, place_ref, cq_ref, sq_ref, r32_ref,
                 cm_ref, sm_ref, kone_ref,
                 wa_ref, ba_ref, hn_ref,
                 cw_ref, cb_ref, wax_ref, bax_ref, lam_ref,
                 mq_ref, mk_ref, mv_ref, bq_ref, bk_ref, bv_ref, yb_ref, yd_ref,
                 w_ref, km_ref, state_ref, tail_ref, h_ref, *, tm):
    si = pl.program_id(1)

    @pl.when((pl.program_id(0) == 0) & (si == 0))
    def _():
        _build_w_in(w_raw_ref, w_ref)

    @pl.when(si == 0)
    def _():
        km_ref[...] = jnp.zeros_like(km_ref)
        state_ref[...] = jnp.zeros_like(state_ref)
        tail_ref[...] = jnp.zeros_like(tail_ref)
        h_ref[...] = jnp.zeros_like(h_ref)

    h = _rms(x_ref[...], g_ref[...]).astype(BF16)
    off = 0
    p_mla = _dot(h, w_ref[:, off:off + W_MLA])
    off += W_MLA
    p_gla = _dot(h, w_ref[:, off:off + W_GLA])
    off += W_GLA
    p_moba = _dot(h, w_ref[:, off:off + W_MOBA])
    off += W_MOBA
    p_lru = _dot(h, w_ref[:, off:off + W_LRU])
    _mla_prep(p_mla, qg_ref, wq_ref, kvg_ref, wkv_ref, place_ref, cq_ref, sq_ref, r32_ref, mq_ref, mk_ref, mv_ref)
    _moba_prep(p_moba, cm_ref, sm_ref, kone_ref, bq_ref, bk_ref, bv_ref, km_ref, si, tm)
    _gla(p_gla, wa_ref, ba_ref, hn_ref, yb_ref, state_ref, tm)
    _lru(p_lru, cw_ref, cb_ref, wax_ref, bax_ref, lam_ref, yd_ref, tail_ref, h_ref, tm)


def _prep(x, consts, tables, B, S, tm):
    T = x.shape[0]
    ns = S // tm
    hw = ATTN_HEADS * LANES
    tok = lambda b, s: (b * ns + s, 0)
    pos = lambda b, s: (s, 0)
    g, w, qg, wq, kvg, wkv, place, wa, ba, hn, cw, cb, wax, bax, lam = consts
    cq, sq, r32, cm, sm, kone = tables
    c = lambda a: _const_spec(a.shape)
    t = lambda a: pl.BlockSpec((tm, a.shape[1]), pos)
    nb = max(2 * SUBLANES, -(-(S // MOBA_BLOCK) // SUBLANES) * SUBLANES)
    tok_t = lambda b, s: (0, b * ns + s)
    outs = [(hw, True), (hw, False), (GROUP_WIDTH, True), (hw, True), (hw, False), (GROUP_WIDTH, True),
            (GROUP_WIDTH, False), (GROUP_WIDTH, False)]
    return pl.pallas_call(
        functools.partial(_prep_kernel, tm=tm),
        grid=(B, ns),
        in_specs=[pl.BlockSpec((tm, D_MODEL), tok), c(g), c(w),
                  c(qg), c(wq), c(kvg), c(wkv), c(place), t(cq), t(sq), t(r32),
                  t(cm), t(sm), t(kone),
                  c(wa), c(ba), c(hn),
                  c(cw), c(cb), c(wax), c(bax), c(lam)],
        out_specs=[pl.BlockSpec((wd, tm), tok_t) if tr else pl.BlockSpec((tm, wd), tok) for wd, tr in outs],
        out_shape=[jax.ShapeDtypeStruct((wd, T) if tr else (T, wd), BF16) for wd, tr in outs],
        scratch_shapes=[pltpu.VMEM((D_MODEL, W_IN), BF16),
                        pltpu.VMEM((nb, GROUP_WIDTH), F32),
                        pltpu.VMEM((GLA_HEADS * GLA_DV, GLA_HEADS * GLA_DK), F32),
                        pltpu.VMEM((SUBLANES, LRU_WIDTH), F32), pltpu.VMEM((1, LRU_WIDTH), F32)],
        compiler_params=_params(("arbitrary", "arbitrary")),
        name="mixer_prep",
    )(x, g, w, qg, wq, kvg, wkv, place, cq, sq, r32, cm, sm, kone, wa, ba, hn, cw, cb, wax, bax, lam)


HEAD_V = 64


def _attn_kernel(qt_ref, k_ref, vt_ref, o_ref, m_ref, l_ref, acc_ref, *, tq, tk):
    i = pl.program_id(1)
    m_ref[...] = jnp.full(m_ref.shape, NEG_BIG, F32)
    l_ref[...] = jnp.zeros(l_ref.shape, F32)
    acc_ref[...] = jnp.zeros(acc_ref.shape, F32)

    def tile(j, diagonal):
        keys = pl.ds(pl.multiple_of(j * tk, tk), tk)
        if diagonal:
            kpos = j * tk + lax.broadcasted_iota(jnp.int32, (tk, tq), 0)
            qpos = i * tq + lax.broadcasted_iota(jnp.int32, (tk, tq), 1)
            allowed = kpos <= qpos
        scores = [_dot(k_ref[keys, h * LANES:(h + 1) * LANES], qt_ref[h * LANES:(h + 1) * LANES, :])
                  for h in range(ATTN_HEADS)]
        for h in range(ATTN_HEADS):
            vt = vt_ref[h * HEAD_V:(h + 1) * HEAD_V, keys]
            s = scores[h]
            if diagonal:
                s = jnp.where(allowed, s, NEG_BIG)
            m_prev = m_ref[h]
            m_next = jnp.maximum(m_prev, jnp.max(s, axis=0, keepdims=True))
            alpha = jnp.exp2(m_prev - m_next)
            p = jnp.exp2(s - jnp.tile(m_next, (tk // SUBLANES, 1)))
            l_ref[h] = alpha * l_ref[h] + jnp.sum(p, axis=0, keepdims=True)
            acc_ref[h] = jnp.tile(alpha, (HEAD_V // SUBLANES, 1)) * acc_ref[h] + _dot(vt, p.astype(BF16))
            m_ref[h] = m_next

    n_full = (i * tq) // tk

    def body(j, _):
        tile(j, False)
        return 0

    lax.fori_loop(0, n_full, body, 0)
    for d in range(tq // tk):
        tile(n_full + d, True)
    out_t = jnp.concatenate(
        [acc_ref[h] * jnp.tile(1.0 / l_ref[h], (HEAD_V // SUBLANES, 1)) for h in range(ATTN_HEADS)], axis=0)
    o_ref[...] = out_t.T.astype(o_ref.dtype)


def _attn(qt, k, vt, B, S, tq, tk, name):
    T = k.shape[0]
    nq = S // tq
    hw = ATTN_HEADS * LANES
    stat = pltpu.VMEM((ATTN_HEADS, SUBLANES, tq), F32)
    return pl.pallas_call(
        functools.partial(_attn_kernel, tq=tq, tk=tk),
        grid=(B, nq),
        in_specs=[pl.BlockSpec((hw, tq), lambda b, i: (0, b * nq + i)),
                  pl.BlockSpec((S, hw), lambda b, i: (b, 0)),
                  pl.BlockSpec((GROUP_WIDTH, S), lambda b, i: (0, b))],
        out_specs=pl.BlockSpec((tq, GROUP_WIDTH), lambda b, i: (b * nq + i, 0)),
        out_shape=jax.ShapeDtypeStruct((T, GROUP_WIDTH), BF16),
        scratch_shapes=[stat, stat, pltpu.VMEM((ATTN_HEADS, HEAD_V, tq), F32)],
        compiler_params=_params(("parallel", "arbitrary")),
        name=name,
    )(qt, k, vt)


def _ffn_kernel(x_ref, ya_ref, yb_ref, yc_ref, yd_ref, wo_ref, fg_ref, wi_ref, w2_ref, ng_ref, o_ref,
                *, final, chunk):
    W = GROUP_WIDTH
    x1 = x_ref[...]
    for n, y_ref in enumerate((ya_ref, yb_ref, yc_ref, yd_ref)):
        x1 = x1 + _dot(y_ref[...], wo_ref[n * W:(n + 1) * W, :])
    h = _rms(x1, fg_ref[...]).astype(BF16)
    parts = []
    for c in range(FFN_HIDDEN // chunk):
        g = _dot(h, wi_ref[:, c * chunk:(c + 1) * chunk])
        up = _dot(h, wi_ref[:, FFN_HIDDEN + c * chunk:FFN_HIDDEN + (c + 1) * chunk])
        act = (g * _sigmoid(g) * up).astype(BF16)
        parts.append(_dot(act, w2_ref[c * chunk:(c + 1) * chunk, :]))
    acc = x1 + sum(parts[1:], parts[0])
    if final:
        acc = _rms(acc, ng_ref[...])
    o_ref[...] = acc


def _ffn(x, ya, yb, yc, yd, wo, fg, wi, w2, ng, final, tm, chunk):
    T = x.shape[0]
    tok = lambda i: (i, 0)
    ysp = pl.BlockSpec((tm, GROUP_WIDTH), tok)
    return pl.pallas_call(
        functools.partial(_ffn_kernel, final=final, chunk=chunk),
        grid=(T // tm,),
        in_specs=[pl.BlockSpec((tm, D_MODEL), tok), ysp, ysp, ysp, ysp,
                  _const_spec(wo.shape), _const_spec(fg.shape), _const_spec(wi.shape),
                  _const_spec(w2.shape), _const_spec(ng.shape)],
        out_specs=pl.BlockSpec((tm, D_MODEL), tok),
        out_shape=jax.ShapeDtypeStruct((T, D_MODEL), F32),
        compiler_params=_params(("parallel",)),
        name="outproj_ffn",
    )(x, ya, yb, yc, yd, wo, fg, wi, w2, ng)


def _swap_cols(w, half):
    return jnp.concatenate([-w[:, half:2 * half], w[:, :half]], axis=1)


def _layout_mla(w_uq, w_ukv):
    zq = jnp.zeros((w_uq.shape[0], LANES - MLA_NOPE - MLA_ROPE), w_uq.dtype)
    zr = jnp.zeros((w_uq.shape[0], MLA_NOPE), w_uq.dtype)
    qd = MLA_NOPE + MLA_ROPE
    plain, swapped = [], []
    for h in range(MLA_HEADS):
        nope = w_uq[:, h * qd:h * qd + MLA_NOPE]
        rope = w_uq[:, h * qd + MLA_NOPE:(h + 1) * qd]
        plain += [nope, rope, zq]
        swapped += [zr, _swap_cols(rope, MLA_ROPE // 2), zq]
    wq = jnp.concatenate(plain + swapped, axis=1).astype(BF16)
    zk = jnp.zeros((w_ukv.shape[0], LANES - MLA_NOPE), w_ukv.dtype)
    kd = MLA_NOPE + MLA_V
    kparts = []
    vparts = []
    for h in range(MLA_HEADS):
        kparts += [w_ukv[:, h * kd:h * kd + MLA_NOPE], zk]
        vparts += [w_ukv[:, h * kd + MLA_NOPE:(h + 1) * kd]]
    wkv = jnp.concatenate(kparts + vparts, axis=1).astype(BF16)
    place = np.zeros((MLA_ROPE, MLA_HEADS * LANES), np.float32)
    for h in range(MLA_HEADS):
        place[np.arange(MLA_ROPE), h * LANES + MLA_NOPE + np.arange(MLA_ROPE)] = 1.0
    return wq, wkv, jnp.asarray(place, BF16)


def _tables(S):
    def cs(dim):
        inv_freq = ROPE_THETA ** (-jnp.arange(0, dim, 2, dtype=F32) / dim)
        ang = jnp.arange(S, dtype=F32)[:, None] * inv_freq[None, :]
        return jnp.cos(ang), jnp.sin(ang)

    c, s = cs(MLA_ROPE)
    scale = (MLA_NOPE + MLA_ROPE) ** -0.5 * LOG2E
    one = jnp.ones((S, MLA_NOPE), F32)
    zero = jnp.zeros((S, MLA_NOPE), F32)
    pad1 = jnp.ones((S, LANES - MLA_NOPE - MLA_ROPE), F32)
    pad0 = jnp.zeros((S, LANES - MLA_NOPE - MLA_ROPE), F32)
    cq = jnp.concatenate([one, c, c, pad1] * MLA_HEADS, axis=1) * scale
    sq = jnp.concatenate([zero, s, s, pad0] * MLA_HEADS, axis=1) * scale
    r32 = jnp.concatenate([c, c, s, s], axis=1)
    c, s = cs(MOBA_ROT)
    one = jnp.ones((S, MOBA_HD - MOBA_ROT), F32)
    zero = jnp.zeros((S, MOBA_HD - MOBA_ROT), F32)
    cm = jnp.concatenate([c, c, one] * MOBA_HEADS, axis=1)
    sm = jnp.concatenate([s, s, zero] * MOBA_HEADS, axis=1)
    blk = jnp.arange(S, dtype=jnp.int32)[:, None] // MOBA_BLOCK
    lane = jnp.arange(LANES, dtype=jnp.int32)[None, :]
    kone = jnp.concatenate([(lane == blk + MOBA_HD).astype(F32),
                            (lane == blk + MOBA_HD + MOBA_SLOT).astype(F32)], axis=1)
    return cq, sq, r32, cm, sm, kone


def _block_diag(w):
    n, c, d = w.shape
    out = jnp.zeros((n * c, n * d), w.dtype)
    for j in range(n):
        out = out.at[j * c:(j + 1) * c, j * d:(j + 1) * d].set(w[j])
    return out


def _layer_consts(l, attn_norm, w_in, mla_q_norm, mla_w_uq, mla_kv_norm, mla_w_ukv, gla_w_a2, gla_b_a2,
                  gla_head_norm, lru_conv_w, lru_conv_b, lru_w_a, lru_b_a, lru_w_x, lru_b_x, lru_lambda):
    row = lambda v: v.reshape(1, -1).astype(F32)
    wq, wkv, place = _layout_mla(mla_w_uq[l], mla_w_ukv[l])
    wa = jnp.zeros((LANES, GLA_HEADS * GLA_DK), F32).at[:GLA_GATE_RANK].set(gla_w_a2[l]).astype(BF16)
    hn = jnp.tile(row(gla_head_norm[l]), (1, GLA_HEADS))
    wax = jnp.concatenate([_block_diag(lru_w_a[l]), _block_diag(lru_w_x[l])], axis=1).astype(BF16)
    bax = jnp.concatenate([row(lru_b_a[l]), row(lru_b_x[l])], axis=1)
    return (row(attn_norm[l]), w_in[l], row(mla_q_norm[l]), wq, row(mla_kv_norm[l]), wkv, place,
            wa, row(gla_b_a2[l]), hn, lru_conv_w[l].reshape(LRU_CONV, LRU_WIDTH), row(lru_conv_b[l]), wax, bax,
            row(lru_lambda[l]))


def kernel(x, attn_norm, w_in, mla_q_norm, mla_w_uq, mla_kv_norm, mla_w_ukv, gla_w_a2, gla_b_a2, gla_head_norm,
           lru_conv_w, lru_conv_b, lru_w_a, lru_b_a, lru_w_x, lru_b_x, lru_lambda, w_out, ffn_norm, w_ffn_in,
           w_ffn_out, final_norm):
    B, S, D = x.shape
    depth = w_in.shape[0]
    T = B * S
    assert D == D_MODEL and S % max(PREP_TM, ATTN_TQ, FFN_TM) == 0 and S // MOBA_BLOCK <= MOBA_SLOT
    tables = _tables(S)
    row = lambda v: v.reshape(1, -1).astype(F32)
    xt = x.reshape(T, D)
    for l in range(depth):
        consts = _layer_consts(l, attn_norm, w_in, mla_q_norm, mla_w_uq, mla_kv_norm, mla_w_ukv, gla_w_a2,
                               gla_b_a2, gla_head_norm, lru_conv_w, lru_conv_b, lru_w_a, lru_b_a, lru_w_x,
                               lru_b_x, lru_lambda)
        mq, mk, mv, bq, bk, bv, y_b, y_d = _prep(xt, consts, tables, B, S, PREP_TM)
        y_a = _attn(mq, mk, mv, B, S, ATTN_TQ, ATTN_TK, "mla_attn")
        y_c = _attn(bq, bk, bv, B, S, ATTN_TQ, ATTN_TK, "moba_attn")
        xt = _ffn(xt, y_a, y_b, y_c, y_d, w_out[l].astype(BF16), row(ffn_norm[l]), w_ffn_in[l].astype(BF16),
                  w_ffn_out[l].astype(BF16), row(final_norm), l == depth - 1, FFN_TM, FFN_HIDDEN // 2)
    return xt.reshape(B, S, D)
```

```python
import functools

import numpy as np
import jax
import jax.numpy as jnp
from jax import lax
from jax.experimental import pallas as pl
from jax.experimental.pallas import tpu as pltpu

F32 = jnp.float32
BF16 = jnp.bfloat16

D_MODEL = 1024
GROUP_WIDTH = 256
ROPE_THETA = 500000.0
NORM_EPS = 1e-6
MLA_HEADS = 4
MLA_V = 64
MLA_NOPE = 64
MLA_ROPE = 32
MLA_Q_RANK = 256
MLA_KV_RANK = 128
GLA_HEADS = 4
GLA_DV = 64
GLA_DK = 32
GLA_GATE_RANK = 16
GLA_TAU = 16.0
GLA_CHUNK = 64
MOBA_HEADS = 4
MOBA_HD = 64
MOBA_ROT = 16
MOBA_BLOCK = 256
MOBA_TOPK = 3
LRU_WIDTH = 256
LRU_BLOCKS = 4
LRU_BW = 64
LRU_CONV = 4
LRU_C = 8.0
FFN_HIDDEN = 2816

LANES = 128
SUBLANES = 8
VMEM_LIMIT = 56 * 1024 * 1024

W_MLA = 512
W_GLA = 896
W_MOBA = 1280
W_LRU = 512
W_IN = W_MLA + W_GLA + W_MOBA + W_LRU
NEG_BIG = -1e30
LOG2E = 1.4426950408889634
MOBA_SLOT = 32
PREP_TM = 512
FFN_TM = 1024
ATTN_HEADS = 4
ATTN_TQ = 512
ATTN_TK = 512


def _rms(x, g):
    return x * lax.rsqrt(jnp.mean(x * x, axis=-1, keepdims=True) + NORM_EPS) * g


def _dot(a, b):
    return jnp.dot(a, b, preferred_element_type=F32)


def _dot_nt(a, b):
    return lax.dot_general(a, b, (((1,), (1,)), ((), ())), preferred_element_type=F32)


def _dot_tn(a, b):
    return lax.dot_general(a, b, (((0,), (0,)), ((), ())), preferred_element_type=F32)


def _sigmoid(x):
    return 1.0 / (1.0 + jnp.exp(-x))


def _const_spec(shape):
    nd = len(shape)
    return pl.BlockSpec(shape, lambda *_: (0,) * nd, pipeline_mode=pl.Buffered(1))


def _params(sem):
    return pltpu.CompilerParams(dimension_semantics=sem, vmem_limit_bytes=VMEM_LIMIT)


def _mla_prep(p, qg_ref, wq_ref, kvg_ref, wkv_ref, place_ref, cq_ref, sq_ref, r32_ref, qt_ref, k_ref, vt_ref):
    hw = MLA_HEADS * LANES
    nq = _rms(p[:, :MLA_Q_RANK], qg_ref[...]).astype(BF16)
    q2 = _dot(nq, wq_ref[...])
    cq = cq_ref[...]
    sq = sq_ref[...]
    q = jnp.concatenate([q2[:, h * LANES:(h + 1) * LANES] * cq + q2[:, hw + h * LANES:hw + (h + 1) * LANES] * sq
                         for h in range(MLA_HEADS)], axis=1)
    qt_ref[...] = q.T.astype(BF16)
    nkv = _rms(p[:, MLA_Q_RANK:MLA_Q_RANK + MLA_KV_RANK], kvg_ref[...]).astype(BF16)
    kv = _dot(nkv, wkv_ref[...])
    c0 = MLA_Q_RANK + MLA_KV_RANK
    r32 = r32_ref[...]
    k_pe = p[:, c0:c0 + MLA_ROPE] * r32[:, :MLA_ROPE] + p[:, c0 + MLA_ROPE:c0 + 2 * MLA_ROPE] * r32[:, MLA_ROPE:]
    k_ref[...] = (kv[:, :hw] + _dot(k_pe.astype(BF16), place_ref[...])).astype(BF16)
    vt_ref[...] = kv[:, hw:].T.astype(BF16)


def _moba_prep(p, cm_ref, sm_ref, qt_ref, k_ref, vt_ref, km_ref, si, tm):
    W = GROUP_WIDTH
    HD = MOBA_HD
    nb = km_ref.shape[0]
    cm = jnp.concatenate([cm_ref[...]] * (W // LANES), axis=1)
    sm = jnp.concatenate([sm_ref[...]] * (W // LANES), axis=1)
    q = p[:, 0:W] * cm + p[:, 3 * W:4 * W] * sm
    k = p[:, W:2 * W] * cm + p[:, 4 * W:5 * W] * sm
    vt_ref[...] = p[:, 2 * W:3 * W].T.astype(BF16)
    nblk = tm // MOBA_BLOCK
    for j in range(nblk):
        mean = jnp.sum(k[j * MOBA_BLOCK:(j + 1) * MOBA_BLOCK, :], axis=0, keepdims=True) * (1.0 / MOBA_BLOCK)
        km_ref[pl.ds(si * nblk + j, 1), :] = mean
    km = km_ref[...]
    lane_w = lax.broadcasted_iota(jnp.int32, (tm, W), 1)
    n_idx = lax.broadcasted_iota(jnp.int32, (nb, tm), 0)
    blk = (si * tm + lax.broadcasted_iota(jnp.int32, (nb, tm), 1)) // MOBA_BLOCK
    past = n_idx < blk
    own = n_idx == blk
    n_f = n_idx.astype(F32)
    masks = []
    for h in range(MOBA_HEADS):
        qh = jnp.where((lane_w // HD) == h, q, 0.0)
        gate = lax.dot_general(km, qh, (((1,), (1,)), ((), ())), preferred_element_type=F32,
                               precision=lax.Precision.HIGHEST)
        gate = jnp.where(past, gate, -jnp.inf)
        keep = own
        for _ in range(MOBA_TOPK):
            mx = jnp.max(gate, axis=0, keepdims=True)
            first = jnp.min(jnp.where(gate == mx, n_f, float(LANES)), axis=0, keepdims=True)
            pick = (n_f == first) & (mx > -jnp.inf)
            keep = keep | pick
            gate = jnp.where(pick, -jnp.inf, gate)
        masks.append(jnp.where(keep, 0.0, NEG_BIG))
        if nb < MOBA_SLOT:
            masks.append(jnp.zeros((MOBA_SLOT - nb, tm), F32))
    mask_hi = jnp.concatenate(masks, axis=0).T
    mask_lo = pltpu.roll(mask_hi, HD, 1)
    lane = lax.broadcasted_iota(jnp.int32, (tm, LANES), 1)
    head_lanes = lane < HD
    q_scaled = q * (MOBA_HD ** -0.5 * LOG2E)
    own_blk = (si * tm + lax.broadcasted_iota(jnp.int32, (tm, LANES), 0)) // MOBA_BLOCK
    kone = [(lane == own_blk + HD + slot * MOBA_SLOT).astype(F32) for slot in range(2)]
    q_ext = []
    for h in range(MOBA_HEADS):
        pr = h // 2
        q_pair = q_scaled[:, pr * LANES:(pr + 1) * LANES]
        k_pair = k[:, pr * LANES:(pr + 1) * LANES]
        if h % 2:
            q_pair = pltpu.roll(q_pair, HD, 1)
            k_pair = pltpu.roll(k_pair, HD, 1)
        q_ext.append(jnp.where(head_lanes, q_pair, mask_lo if h < 2 else mask_hi))
        k_ref[:, h * LANES:(h + 1) * LANES] = jnp.where(
            head_lanes, k_pair, kone[h % 2]).astype(BF16)
    qt_ref[...] = jnp.concatenate(q_ext, axis=1).T.astype(BF16)


def _gla(p, wa_ref, ba_ref, hn_ref, o_ref, state_ref, tm):
    C = GLA_CHUNK
    hk = GLA_HEADS * GLA_DK
    hv = GLA_HEADS * GLA_DV
    nc = tm // C
    rc = lax.broadcasted_iota(jnp.int32, (tm, hk), 0) % C
    lane_k = lax.broadcasted_iota(jnp.int32, (C, hk), 1)
    lane_v = lax.broadcasted_iota(jnp.int32, (C, hv), 1)
    row4 = lax.broadcasted_iota(jnp.int32, (GLA_HEADS * C, C), 0)
    col4 = lax.broadcasted_iota(jnp.int32, (GLA_HEADS * C, C), 1)
    causal4 = (row4 % C) >= col4
    sr = lax.broadcasted_iota(jnp.int32, (hv, hk), 0)
    sc = lax.broadcasted_iota(jnp.int32, (hv, hk), 1)
    blockdiag = (sr // GLA_DV) == (sc // GLA_DK)
    scale = GLA_DK ** -0.5

    q = p[:, 0:hk]
    k = p[:, hk:2 * hk]
    vb = p[:, 2 * hk:2 * hk + hv].astype(BF16)
    a_low = p[:, 2 * hk + 2 * hv:2 * hk + 2 * hv + LANES]
    a_lin = _dot(a_low.astype(BF16), wa_ref[...]) + ba_ref[...]
    b = (jnp.minimum(a_lin, 0.0) - jnp.log1p(jnp.exp(-jnp.abs(a_lin)))) * (1.0 / GLA_TAU)
    for s in (1, 2, 4, 8, 16, 32):
        b = b + jnp.where(rc >= s, pltpu.roll(b, s, 0), 0.0)
    last = [b[c * C + C - 1:(c + 1) * C, :] for c in range(nc)]
    bl = jnp.concatenate([jnp.broadcast_to(r, (C, hk)) for r in last], axis=0)
    bref = 0.5 * bl
    qf = q * scale
    qs = qf * jnp.exp(b - bref)
    ks = (k * jnp.exp(bref - b)).astype(BF16)
    kd = (k * jnp.exp(bl - b)).astype(BF16)
    qe = (qf * jnp.exp(b)).astype(BF16)

    intra, incr = [], []
    for c in range(nc):
        sl = slice(c * C, (c + 1) * C)
        qs4 = jnp.concatenate(
            [jnp.where((lane_k // GLA_DK) == h, qs[sl], 0.0) for h in range(GLA_HEADS)], axis=0).astype(BF16)
        att = jnp.where(causal4, _dot_nt(qs4, ks[sl]), 0.0)
        res = _dot(att.astype(BF16), vb[sl])
        o = jnp.where((lane_v // GLA_DV) == 0, res[0:C, :], 0.0)
        for h in range(1, GLA_HEADS):
            o = o + jnp.where((lane_v // GLA_DV) == h, res[h * C:(h + 1) * C, :], 0.0)
        intra.append(o)
        incr.append(jnp.where(blockdiag, _dot_tn(vb[sl], kd[sl]), 0.0))

    st = state_ref[...]
    outs = []
    for c in range(nc):
        outs.append(intra[c] + _dot_nt(qe[c * C:(c + 1) * C], st.astype(BF16)))
        st = st * jnp.exp(last[c]) + incr[c]
    state_ref[...] = st
    o = jnp.concatenate(outs, axis=0)

    lane_t = lax.broadcasted_iota(jnp.int32, (tm, hv), 1)
    osq = o * o
    inv = jnp.zeros_like(o)
    for h in range(GLA_HEADS):
        mh = (lane_t // GLA_DV) == h
        ms = jnp.sum(jnp.where(mh, osq, 0.0), axis=-1, keepdims=True) * (1.0 / GLA_DV)
        inv = jnp.where(mh, lax.rsqrt(ms + NORM_EPS), inv)
    g = p[:, 2 * hk + hv:2 * hk + 2 * hv]
    o_ref[...] = (o * inv * hn_ref[...] * (g * _sigmoid(g))).astype(o_ref.dtype)


def _lru(p, cw_ref, cb_ref, wax_ref, bax_ref, lam_ref, o_ref, tail_ref, h_ref, tm):
    W = LRU_WIDTH
    G = SUBLANES
    x = p[:, 0:W]
    gate = p[:, W:2 * W]
    xx = jnp.concatenate([tail_ref[...], x], axis=0)
    cw = cw_ref[...]
    xc = cb_ref[...] + cw[LRU_CONV - 1:LRU_CONV, :] * x
    for d in range(1, LRU_CONV):
        xc = xc + cw[LRU_CONV - 1 - d:LRU_CONV - d, :] * xx[G - d:G - d + tm, :]
    tail_ref[...] = x[tm - G:, :]
    ri = _dot(xc.astype(BF16), wax_ref[...]) + bax_ref[...]
    r = _sigmoid(ri[:, :W])
    ig = _sigmoid(ri[:, W:])
    lam = lam_ref[...]
    softplus_neg = jnp.maximum(-lam, 0.0) + jnp.log1p(jnp.exp(-jnp.abs(lam)))
    log_a = (-LRU_C * r) * softplus_neg
    a = jnp.exp(log_a)
    u = jnp.sqrt(-jnp.tanh(log_a) * (a * a + 1.0)) * (ig * xc)
    sub = lax.broadcasted_iota(jnp.int32, (G, W), 0)
    h = h_ref[...]
    outs = []
    for g in range(tm // G):
        ag = a[g * G:(g + 1) * G, :]
        ug = u[g * G:(g + 1) * G, :]
        for s in (1, 2, 4):
            live = sub >= s
            ug = ug + ag * jnp.where(live, pltpu.roll(ug, s, 0), 0.0)
            ag = ag * jnp.where(live, pltpu.roll(ag, s, 0), 1.0)
        hg = ug + ag * h
        outs.append(hg)
        h = hg[G - 1:G, :]
    h_ref[...] = h
    hs = jnp.concatenate(outs, axis=0)
    gelu = 0.5 * gate * (1.0 + jnp.tanh(0.7978845608028654 * (gate + 0.044715 * gate * gate * gate)))
    o_ref[...] = (hs * gelu).astype(o_ref.dtype)


def _w_in_plan():
    sp = np.cumsum([0, 256, 128, 32, 128, 128, 256, 16, 256, 256, 256, 256, 256, 256])
    mq, mkv, mkr, gq, gk, gv, ga, gg, bq, bk, bv, rx, rg = [(int(sp[n]), int(sp[n + 1] - sp[n])) for n in range(13)]
    keep = lambda c: [(c[1], c[0], 1.0)]
    zeros = lambda n: [(n, None, 0.0)]

    def swapped(start, half):
        return [(half, start + half, -1.0), (half, start, 1.0)]

    def moba_swapped(c):
        out = []
        for h in range(MOBA_HEADS):
            out += swapped(c[0] + h * MOBA_HD, MOBA_ROT // 2) + zeros(MOBA_HD - MOBA_ROT)
        return out

    plan = (keep(mq) + keep(mkv) + keep(mkr) + swapped(mkr[0], MLA_ROPE // 2) + zeros(W_MLA - 256 - 128 - 64)
            + keep(gq) + keep(gk) + keep(gv) + keep(gg) + keep(ga) + zeros(LANES - GLA_GATE_RANK)
            + keep(bq) + keep(bk) + keep(bv) + moba_swapped(bq) + moba_swapped(bk)
            + keep(rx) + keep(rg))
    assert sum(p[0] for p in plan) == W_IN
    return plan, int(sp[-1])


def _build_w_in(w_ref, o_ref, rows_per_step=256):
    plan, n_src = _w_in_plan()
    blocks, cur, room = [], [], LANES
    for width, src, sign in plan:
        while width:
            take = min(width, room)
            cur.append((take, src, sign))
            src = None if src is None else src + take
            width -= take
            room -= take
            if room == 0:
                blocks.append(cur)
                cur, room = [], LANES
    for r0 in range(0, w_ref.shape[0], rows_per_step):
        rows = slice(r0, r0 + rows_per_step)
        for j, frags in enumerate(blocks):
            vals = []
            for width, src, sign in frags:
                if src is None:
                    vals.append(jnp.zeros((rows_per_step, width), F32))
                    continue
                a0 = src // LANES * LANES
                a1 = min(-(-(src + width) // LANES) * LANES, n_src)
                win = w_ref[rows, a0:a1]
                v = win[:, src - a0:src - a0 + width]
                vals.append(-v if sign < 0 else v)
            blk = vals[0] if len(vals) == 1 else jnp.concatenate(vals, axis=1)
            o_ref[rows, j * LANES:(j + 1) * LANES] = blk.astype(BF16)


def _prep_kernel(x_ref, g_ref, w_raw_ref,
                 qg_ref, wq_ref, kvg_ref, wkv_ref, place_ref, cq_ref, sq_ref, r32_ref,
                 cm_ref, sm_ref,
                 wa_ref, ba_ref, hn_ref,
                 cw_ref, cb_ref, wax_ref, bax_ref, lam_ref,
                 mq_ref, mk_ref, mv_ref, bq_ref, bk_ref, bv_ref, yb_ref, yd_ref,
                 w_ref, km_ref, state_ref, tail_ref, h_ref, *, tm):
    si = pl.program_id(1)

    @pl.when((pl.program_id(0) == 0) & (si == 0))
    def _():
        _build_w_in(w_raw_ref, w_ref)

    @pl.when(si == 0)
    def _():
        km_ref[...] = jnp.zeros_like(km_ref)
        state_ref[...] = jnp.zeros_like(state_ref)
        tail_ref[...] = jnp.zeros_like(tail_ref)
        h_ref[...] = jnp.zeros_like(h_ref)

    h = _rms(x_ref[...], g_ref[...]).astype(BF16)
    off = 0
    p_mla = _dot(h, w_ref[:, off:off + W_MLA])
    off += W_MLA
    p_gla = _dot(h, w_ref[:, off:off + W_GLA])
    off += W_GLA
    p_moba = _dot(h, w_ref[:, off:off + W_MOBA])
    off += W_MOBA
    p_lru = _dot(h, w_ref[:, off:off + W_LRU])
    _mla_prep(p_mla, qg_ref, wq_ref, kvg_ref, wkv_ref, place_ref, cq_ref, sq_ref, r32_ref, mq_ref, mk_ref, mv_ref)
    _moba_prep(p_moba, cm_ref, sm_ref, bq_ref, bk_ref, bv_ref, km_ref, si, tm)
    _gla(p_gla, wa_ref, ba_ref, hn_ref, yb_ref, state_ref, tm)
    _lru(p_lru, cw_ref, cb_ref, wax_ref, bax_ref, lam_ref, yd_ref, tail_ref, h_ref, tm)


def _prep(x, w_in, l, consts, tables, B, S, tm):
    T = x.shape[0]
    ns = S // tm
    hw = ATTN_HEADS * LANES
    tok = lambda b, s: (b * ns + s, 0)
    pos = lambda b, s: (s, 0)
    g, qg, wq, kvg, wkv, place, wa, ba, hn, cw, cb, wax, bax, lam = consts
    cq, sq, r32, cm, sm = tables
    w_spec = pl.BlockSpec((None,) + w_in.shape[1:], lambda b, s: (l, 0, 0))
    c = lambda a: _const_spec(a.shape)
    t = lambda a: pl.BlockSpec((tm, a.shape[1]), pos)
    nb = max(2 * SUBLANES, -(-(S // MOBA_BLOCK) // SUBLANES) * SUBLANES)
    tok_t = lambda b, s: (0, b * ns + s)
    outs = [(hw, True), (hw, False), (GROUP_WIDTH, True), (hw, True), (hw, False), (GROUP_WIDTH, True),
            (GROUP_WIDTH, False), (GROUP_WIDTH, False)]
    return pl.pallas_call(
        functools.partial(_prep_kernel, tm=tm),
        grid=(B, ns),
        in_specs=[pl.BlockSpec((tm, D_MODEL), tok), c(g), w_spec,
                  c(qg), c(wq), c(kvg), c(wkv), c(place), t(cq), t(sq), t(r32),
                  t(cm), t(sm),
                  c(wa), c(ba), c(hn),
                  c(cw), c(cb), c(wax), c(bax), c(lam)],
        out_specs=[pl.BlockSpec((wd, tm), tok_t) if tr else pl.BlockSpec((tm, wd), tok) for wd, tr in outs],
        out_shape=[jax.ShapeDtypeStruct((wd, T) if tr else (T, wd), BF16) for wd, tr in outs],
        scratch_shapes=[pltpu.VMEM((D_MODEL, W_IN), BF16),
                        pltpu.VMEM((nb, GROUP_WIDTH), F32),
                        pltpu.VMEM((GLA_HEADS * GLA_DV, GLA_HEADS * GLA_DK), F32),
                        pltpu.VMEM((SUBLANES, LRU_WIDTH), F32), pltpu.VMEM((1, LRU_WIDTH), F32)],
        compiler_params=_params(("arbitrary", "arbitrary")),
        name="mixer_prep",
    )(x, g, w_in, qg, wq, kvg, wkv, place, cq, sq, r32, cm, sm, wa, ba, hn, cw, cb, wax, bax, lam)


HEAD_V = 64
ACC_ROWS = HEAD_V + 16


def _attn_kernel(qt_ref, k_ref, vt_ref, o_ref, m_ref, acc_ref, *, tq, tk):
    i = pl.program_id(1)
    m_ref[...] = jnp.full(m_ref.shape, NEG_BIG, F32)
    acc_ref[...] = jnp.zeros(acc_ref.shape, F32)
    ones = jnp.ones((ACC_ROWS - HEAD_V, tk), BF16)

    def tile(j, diagonal):
        keys = pl.ds(pl.multiple_of(j * tk, tk), tk)
        if diagonal:
            kpos = j * tk + lax.broadcasted_iota(jnp.int32, (tk, tq), 0)
            qpos = i * tq + lax.broadcasted_iota(jnp.int32, (tk, tq), 1)
            allowed = kpos <= qpos
        scores = [_dot(k_ref[keys, h * LANES:(h + 1) * LANES], qt_ref[h * LANES:(h + 1) * LANES, :])
                  for h in range(ATTN_HEADS)]
        for h in range(ATTN_HEADS):
            vt = jnp.concatenate([vt_ref[h * HEAD_V:(h + 1) * HEAD_V, keys], ones], axis=0)
            s = scores[h]
            if diagonal:
                s = jnp.where(allowed, s, NEG_BIG)
            m_prev = m_ref[h]
            m_next = jnp.maximum(m_prev, jnp.max(s, axis=0, keepdims=True))
            alpha = jnp.exp2(m_prev - m_next)
            p = jnp.exp2(s - jnp.tile(m_next, (tk // SUBLANES, 1)))
            acc_ref[h] = jnp.tile(alpha, (ACC_ROWS // SUBLANES, 1)) * acc_ref[h] + _dot(vt, p.astype(BF16))
            m_ref[h] = m_next

    n_full = (i * tq) // tk

    def body(j, _):
        tile(j, False)
        return 0

    lax.fori_loop(0, n_full, body, 0)
    for d in range(tq // tk):
        tile(n_full + d, True)
    out_t = jnp.concatenate(
        [acc_ref[h, :HEAD_V] * jnp.tile(1.0 / acc_ref[h, HEAD_V:HEAD_V + SUBLANES], (HEAD_V // SUBLANES, 1))
         for h in range(ATTN_HEADS)], axis=0)
    o_ref[...] = out_t.T.astype(o_ref.dtype)


def _attn(qt, k, vt, B, S, tq, tk, name):
    T = k.shape[0]
    nq = S // tq
    hw = ATTN_HEADS * LANES
    stat = pltpu.VMEM((ATTN_HEADS, SUBLANES, tq), F32)
    return pl.pallas_call(
        functools.partial(_attn_kernel, tq=tq, tk=tk),
        grid=(B, nq),
        in_specs=[pl.BlockSpec((hw, tq), lambda b, i: (0, b * nq + i)),
                  pl.BlockSpec((S, hw), lambda b, i: (b, 0)),
                  pl.BlockSpec((GROUP_WIDTH, S), lambda b, i: (0, b))],
        out_specs=pl.BlockSpec((tq, GROUP_WIDTH), lambda b, i: (b * nq + i, 0)),
        out_shape=jax.ShapeDtypeStruct((T, GROUP_WIDTH), BF16),
        scratch_shapes=[stat, pltpu.VMEM((ATTN_HEADS, ACC_ROWS, tq), F32)],
        compiler_params=_params(("parallel", "arbitrary")),
        name=name,
    )(qt, k, vt)


def _ffn_kernel(x_ref, ya_ref, yb_ref, yc_ref, yd_ref, wo_ref, fg_ref, wi_ref, w2_ref, ng_ref, o_ref,
                *, final, chunk):
    W = GROUP_WIDTH
    x1 = x_ref[...]
    for n, y_ref in enumerate((ya_ref, yb_ref, yc_ref, yd_ref)):
        x1 = x1 + _dot(y_ref[...], wo_ref[n * W:(n + 1) * W, :])
    h = _rms(x1, fg_ref[...]).astype(BF16)
    parts = []
    for c in range(FFN_HIDDEN // chunk):
        g = _dot(h, wi_ref[:, c * chunk:(c + 1) * chunk])
        up = _dot(h, wi_ref[:, FFN_HIDDEN + c * chunk:FFN_HIDDEN + (c + 1) * chunk])
        act = (g * _sigmoid(g) * up).astype(BF16)
        parts.append(_dot(act, w2_ref[c * chunk:(c + 1) * chunk, :]))
    acc = x1 + sum(parts[1:], parts[0])
    if final:
        acc = _rms(acc, ng_ref[...])
    o_ref[...] = acc


def _ffn(x, ya, yb, yc, yd, wo, fg, wi, w2, ng, final, tm, chunk):
    T = x.shape[0]
    tok = lambda i: (i, 0)
    ysp = pl.BlockSpec((tm, GROUP_WIDTH), tok)
    return pl.pallas_call(
        functools.partial(_ffn_kernel, final=final, chunk=chunk),
        grid=(T // tm,),
        in_specs=[pl.BlockSpec((tm, D_MODEL), tok), ysp, ysp, ysp, ysp,
                  _const_spec(wo.shape), _const_spec(fg.shape), _const_spec(wi.shape),
                  _const_spec(w2.shape), _const_spec(ng.shape)],
        out_specs=pl.BlockSpec((tm, D_MODEL), tok),
        out_shape=jax.ShapeDtypeStruct((T, D_MODEL), F32),
        compiler_params=_params(("parallel",)),
        name="outproj_ffn",
    )(x, ya, yb, yc, yd, wo, fg, wi, w2, ng)


def _swap_cols(w, half):
    return jnp.concatenate([-w[:, half:2 * half], w[:, :half]], axis=1)


def _layout_mla(w_uq, w_ukv):
    zq = jnp.zeros((w_uq.shape[0], LANES - MLA_NOPE - MLA_ROPE), w_uq.dtype)
    zr = jnp.zeros((w_uq.shape[0], MLA_NOPE), w_uq.dtype)
    qd = MLA_NOPE + MLA_ROPE
    plain, swapped = [], []
    for h in range(MLA_HEADS):
        nope = w_uq[:, h * qd:h * qd + MLA_NOPE]
        rope = w_uq[:, h * qd + MLA_NOPE:(h + 1) * qd]
        plain += [nope, rope, zq]
        swapped += [zr, _swap_cols(rope, MLA_ROPE // 2), zq]
    wq = jnp.concatenate(plain + swapped, axis=1).astype(BF16)
    zk = jnp.zeros((w_ukv.shape[0], LANES - MLA_NOPE), w_ukv.dtype)
    kd = MLA_NOPE + MLA_V
    kparts = []
    vparts = []
    for h in range(MLA_HEADS):
        kparts += [w_ukv[:, h * kd:h * kd + MLA_NOPE], zk]
        vparts += [w_ukv[:, h * kd + MLA_NOPE:(h + 1) * kd]]
    wkv = jnp.concatenate(kparts + vparts, axis=1).astype(BF16)
    place = np.zeros((MLA_ROPE, MLA_HEADS * LANES), np.float32)
    for h in range(MLA_HEADS):
        place[np.arange(MLA_ROPE), h * LANES + MLA_NOPE + np.arange(MLA_ROPE)] = 1.0
    return wq, wkv, jnp.asarray(place, BF16)


def _tables(S):
    def cs(dim):
        inv_freq = ROPE_THETA ** (-jnp.arange(0, dim, 2, dtype=F32) / dim)
        ang = jnp.arange(S, dtype=F32)[:, None] * inv_freq[None, :]
        return jnp.cos(ang), jnp.sin(ang)

    c, s = cs(MLA_ROPE)
    scale = (MLA_NOPE + MLA_ROPE) ** -0.5 * LOG2E
    one = jnp.ones((S, MLA_NOPE), F32)
    zero = jnp.zeros((S, MLA_NOPE), F32)
    pad1 = jnp.ones((S, LANES - MLA_NOPE - MLA_ROPE), F32)
    pad0 = jnp.zeros((S, LANES - MLA_NOPE - MLA_ROPE), F32)
    cq = jnp.concatenate([one, c, c, pad1], axis=1) * scale
    sq = jnp.concatenate([zero, s, s, pad0], axis=1) * scale
    r32 = jnp.concatenate([c, c, s, s], axis=1)
    c, s = cs(MOBA_ROT)
    one = jnp.ones((S, MOBA_HD - MOBA_ROT), F32)
    zero = jnp.zeros((S, MOBA_HD - MOBA_ROT), F32)
    cm = jnp.concatenate([c, c, one] * 2, axis=1)
    sm = jnp.concatenate([s, s, zero] * 2, axis=1)
    return cq, sq, r32, cm, sm


def _block_diag(w):
    n, c, d = w.shape
    out = jnp.zeros((n * c, n * d), w.dtype)
    for j in range(n):
        out = out.at[j * c:(j + 1) * c, j * d:(j + 1) * d].set(w[j])
    return out


def _layer_consts(l, attn_norm, w_in, mla_q_norm, mla_w_uq, mla_kv_norm, mla_w_ukv, gla_w_a2, gla_b_a2,
                  gla_head_norm, lru_conv_w, lru_conv_b, lru_w_a, lru_b_a, lru_w_x, lru_b_x, lru_lambda):
    row = lambda v: v.reshape(1, -1).astype(F32)
    wq, wkv, place = _layout_mla(mla_w_uq[l], mla_w_ukv[l])
    wa = jnp.zeros((LANES, GLA_HEADS * GLA_DK), F32).at[:GLA_GATE_RANK].set(gla_w_a2[l]).astype(BF16)
    hn = jnp.tile(row(gla_head_norm[l]), (1, GLA_HEADS))
    wax = jnp.concatenate([_block_diag(lru_w_a[l]), _block_diag(lru_w_x[l])], axis=1).astype(BF16)
    bax = jnp.concatenate([row(lru_b_a[l]), row(lru_b_x[l])], axis=1)
    return (row(attn_norm[l]), row(mla_q_norm[l]), wq, row(mla_kv_norm[l]), wkv, place,
            wa, row(gla_b_a2[l]), hn, lru_conv_w[l].reshape(LRU_CONV, LRU_WIDTH), row(lru_conv_b[l]), wax, bax,
            row(lru_lambda[l]))


def kernel(x, attn_norm, w_in, mla_q_norm, mla_w_uq, mla_kv_norm, mla_w_ukv, gla_w_a2, gla_b_a2, gla_head_norm,
           lru_conv_w, lru_conv_b, lru_w_a, lru_b_a, lru_w_x, lru_b_x, lru_lambda, w_out, ffn_norm, w_ffn_in,
           w_ffn_out, final_norm):
    B, S, D = x.shape
    depth = w_in.shape[0]
    T = B * S
    assert D == D_MODEL and S % max(PREP_TM, ATTN_TQ, FFN_TM) == 0 and S // MOBA_BLOCK <= MOBA_SLOT
    tables = _tables(S)
    row = lambda v: v.reshape(1, -1).astype(F32)
    xt = x.reshape(T, D)
    for l in range(depth):
        consts = _layer_consts(l, attn_norm, w_in, mla_q_norm, mla_w_uq, mla_kv_norm, mla_w_ukv, gla_w_a2,
                               gla_b_a2, gla_head_norm, lru_conv_w, lru_conv_b, lru_w_a, lru_b_a, lru_w_x,
                               lru_b_x, lru_lambda)
        mq, mk, mv, bq, bk, bv, y_b, y_d = _prep(xt, w_in, l, consts, tables, B, S, PREP_TM)
        y_a = _attn(mq, mk, mv, B, S, ATTN_TQ, ATTN_TK, "mla_attn")
        y_c = _attn(bq, bk, bv, B, S, ATTN_TQ, ATTN_TK, "moba_attn")
        xt = _ffn(xt, y_a, y_b, y_c, y_d, w_out[l].astype(BF16), row(ffn_norm[l]), w_ffn_in[l].astype(BF16),
                  w_ffn_out[l].astype(BF16), row(final_norm), l == depth - 1, FFN_TM, FFN_HIDDEN // 2)
    return xt.reshape(B, S, D)
```

```python
import functools

import numpy as np
import jax
import jax.numpy as jnp
from jax import lax
from jax.experimental import pallas as pl
from jax.experimental.pallas import tpu as pltpu

F32 = jnp.float32
BF16 = jnp.bfloat16

D_MODEL = 1024
GROUP_WIDTH = 256
ROPE_THETA = 500000.0
NORM_EPS = 1e-6
MLA_HEADS = 4
MLA_V = 64
MLA_NOPE = 64
MLA_ROPE = 32
MLA_Q_RANK = 256
MLA_KV_RANK = 128
GLA_HEADS = 4
GLA_DV = 64
GLA_DK = 32
GLA_GATE_RANK = 16
GLA_TAU = 16.0
GLA_CHUNK = 64
MOBA_HEADS = 4
MOBA_HD = 64
MOBA_ROT = 16
MOBA_BLOCK = 256
MOBA_TOPK = 3
LRU_WIDTH = 256
LRU_BLOCKS = 4
LRU_BW = 64
LRU_CONV = 4
LRU_C = 8.0
FFN_HIDDEN = 2816

LANES = 128
SUBLANES = 8
VMEM_LIMIT = 56 * 1024 * 1024

W_MLA = 512
W_GLA = 896
W_MOBA = 1280
W_LRU = 512
W_IN = W_MLA + W_GLA + W_MOBA + W_LRU
NEG_BIG = -1e30
LOG2E = 1.4426950408889634
MOBA_SLOT = 32
PREP_TM = 512
FFN_TM = 1024
ATTN_HEADS = 4
ATTN_TQ = 1024
ATTN_TK = 512


def _rms(x, g):
    return x * lax.rsqrt(jnp.mean(x * x, axis=-1, keepdims=True) + NORM_EPS) * g


def _dot(a, b):
    return jnp.dot(a, b, preferred_element_type=F32)


def _dot_nt(a, b):
    return lax.dot_general(a, b, (((1,), (1,)), ((), ())), preferred_element_type=F32)


def _dot_tn(a, b):
    return lax.dot_general(a, b, (((0,), (0,)), ((), ())), preferred_element_type=F32)


def _sigmoid(x):
    return 1.0 / (1.0 + jnp.exp(-x))


def _const_spec(shape):
    nd = len(shape)
    return pl.BlockSpec(shape, lambda *_: (0,) * nd, pipeline_mode=pl.Buffered(1))


def _params(sem):
    return pltpu.CompilerParams(dimension_semantics=sem, vmem_limit_bytes=VMEM_LIMIT)


def _mla_prep(p, qg_ref, wq_ref, kvg_ref, wkv_ref, place_ref, cq_ref, sq_ref, r32_ref, qt_ref, k_ref, vt_ref):
    hw = MLA_HEADS * LANES
    nq = _rms(p[:, :MLA_Q_RANK], qg_ref[...]).astype(BF16)
    q2 = _dot(nq, wq_ref[...])
    cq = cq_ref[...]
    sq = sq_ref[...]
    q = jnp.concatenate([q2[:, h * LANES:(h + 1) * LANES] * cq + q2[:, hw + h * LANES:hw + (h + 1) * LANES] * sq
                         for h in range(MLA_HEADS)], axis=1)
    qt_ref[...] = q.T.astype(BF16)
    nkv = _rms(p[:, MLA_Q_RANK:MLA_Q_RANK + MLA_KV_RANK], kvg_ref[...]).astype(BF16)
    kv = _dot(nkv, wkv_ref[...])
    c0 = MLA_Q_RANK + MLA_KV_RANK
    r32 = r32_ref[...]
    k_pe = p[:, c0:c0 + MLA_ROPE] * r32[:, :MLA_ROPE] + p[:, c0 + MLA_ROPE:c0 + 2 * MLA_ROPE] * r32[:, MLA_ROPE:]
    k_ref[...] = (kv[:, :hw] + _dot(k_pe.astype(BF16), place_ref[...])).astype(BF16)
    vt_ref[...] = kv[:, hw:].T.astype(BF16)


def _moba_prep(p, cm_ref, sm_ref, qt_ref, k_ref, vt_ref, km_ref, si, tm):
    W = GROUP_WIDTH
    HD = MOBA_HD
    nb = km_ref.shape[0]
    cm = jnp.concatenate([cm_ref[...]] * (W // LANES), axis=1)
    sm = jnp.concatenate([sm_ref[...]] * (W // LANES), axis=1)
    q = p[:, 0:W] * cm + p[:, 3 * W:4 * W] * sm
    k = p[:, W:2 * W] * cm + p[:, 4 * W:5 * W] * sm
    vt_ref[...] = p[:, 2 * W:3 * W].T.astype(BF16)
    nblk = tm // MOBA_BLOCK
    for j in range(nblk):
        mean = jnp.sum(k[j * MOBA_BLOCK:(j + 1) * MOBA_BLOCK, :], axis=0, keepdims=True) * (1.0 / MOBA_BLOCK)
        km_ref[pl.ds(si * nblk + j, 1), :] = mean
    km = km_ref[...]
    lane_w = lax.broadcasted_iota(jnp.int32, (tm, W), 1)
    n_idx = lax.broadcasted_iota(jnp.int32, (nb, tm), 0)
    blk = (si * tm + lax.broadcasted_iota(jnp.int32, (nb, tm), 1)) // MOBA_BLOCK
    past = n_idx < blk
    own = n_idx == blk
    n_f = n_idx.astype(F32)
    masks = []
    for h in range(MOBA_HEADS):
        qh = jnp.where((lane_w // HD) == h, q, 0.0)
        gate = lax.dot_general(km, qh, (((1,), (1,)), ((), ())), preferred_element_type=F32,
                               precision=lax.Precision.HIGHEST)
        gate = jnp.where(past, gate, -jnp.inf)
        keep = own
        for _ in range(MOBA_TOPK):
            mx = jnp.max(gate, axis=0, keepdims=True)
            first = jnp.min(jnp.where(gate == mx, n_f, float(LANES)), axis=0, keepdims=True)
            pick = (n_f == first) & (mx > -jnp.inf)
            keep = keep | pick
            gate = jnp.where(pick, -jnp.inf, gate)
        masks.append(jnp.where(keep, 0.0, NEG_BIG))
        if nb < MOBA_SLOT:
            masks.append(jnp.zeros((MOBA_SLOT - nb, tm), F32))
    mask_hi = jnp.concatenate(masks, axis=0).T
    mask_lo = pltpu.roll(mask_hi, HD, 1)
    lane = lax.broadcasted_iota(jnp.int32, (tm, LANES), 1)
    head_lanes = lane < HD
    q_scaled = q * (MOBA_HD ** -0.5 * LOG2E)
    own_blk = (si * tm + lax.broadcasted_iota(jnp.int32, (tm, LANES), 0)) // MOBA_BLOCK
    kone = [(lane == own_blk + HD + slot * MOBA_SLOT).astype(F32) for slot in range(2)]
    q_ext = []
    for h in range(MOBA_HEADS):
        pr = h // 2
        q_pair = q_scaled[:, pr * LANES:(pr + 1) * LANES]
        k_pair = k[:, pr * LANES:(pr + 1) * LANES]
        if h % 2:
            q_pair = pltpu.roll(q_pair, HD, 1)
            k_pair = pltpu.roll(k_pair, HD, 1)
        q_ext.append(jnp.where(head_lanes, q_pair, mask_lo if h < 2 else mask_hi))
        k_ref[:, h * LANES:(h + 1) * LANES] = jnp.where(
            head_lanes, k_pair, kone[h % 2]).astype(BF16)
    qt_ref[...] = jnp.concatenate(q_ext, axis=1).T.astype(BF16)


def _gla(p, wa_ref, ba_ref, hn_ref, o_ref, state_ref, tm):
    C = GLA_CHUNK
    hk = GLA_HEADS * GLA_DK
    hv = GLA_HEADS * GLA_DV
    nc = tm // C
    rc = lax.broadcasted_iota(jnp.int32, (tm, hk), 0) % C
    lane_k = lax.broadcasted_iota(jnp.int32, (C, hk), 1)
    lane_v = lax.broadcasted_iota(jnp.int32, (C, hv), 1)
    row4 = lax.broadcasted_iota(jnp.int32, (GLA_HEADS * C, C), 0)
    col4 = lax.broadcasted_iota(jnp.int32, (GLA_HEADS * C, C), 1)
    causal4 = (row4 % C) >= col4
    sr = lax.broadcasted_iota(jnp.int32, (hv, hk), 0)
    sc = lax.broadcasted_iota(jnp.int32, (hv, hk), 1)
    blockdiag = (sr // GLA_DV) == (sc // GLA_DK)
    scale = GLA_DK ** -0.5

    q = p[:, 0:hk]
    k = p[:, hk:2 * hk]
    vb = p[:, 2 * hk:2 * hk + hv].astype(BF16)
    a_low = p[:, 2 * hk + 2 * hv:2 * hk + 2 * hv + LANES]
    a_lin = _dot(a_low.astype(BF16), wa_ref[...]) + ba_ref[...]
    b = (jnp.minimum(a_lin, 0.0) - jnp.log1p(jnp.exp(-jnp.abs(a_lin)))) * (1.0 / GLA_TAU)
    for s in (1, 2, 4, 8, 16, 32):
        b = b + jnp.where(rc >= s, pltpu.roll(b, s, 0), 0.0)
    last = [b[c * C + C - 1:(c + 1) * C, :] for c in range(nc)]
    bl = jnp.concatenate([jnp.broadcast_to(r, (C, hk)) for r in last], axis=0)
    bref = 0.5 * bl
    qf = q * scale
    qs = qf * jnp.exp(b - bref)
    ks = (k * jnp.exp(bref - b)).astype(BF16)
    kd = (k * jnp.exp(bl - b)).astype(BF16)
    qe = (qf * jnp.exp(b)).astype(BF16)

    intra, incr = [], []
    for c in range(nc):
        sl = slice(c * C, (c + 1) * C)
        qs4 = jnp.concatenate(
            [jnp.where((lane_k // GLA_DK) == h, qs[sl], 0.0) for h in range(GLA_HEADS)], axis=0).astype(BF16)
        att = jnp.where(causal4, _dot_nt(qs4, ks[sl]), 0.0)
        res = _dot(att.astype(BF16), vb[sl])
        o = jnp.where((lane_v // GLA_DV) == 0, res[0:C, :], 0.0)
        for h in range(1, GLA_HEADS):
            o = o + jnp.where((lane_v // GLA_DV) == h, res[h * C:(h + 1) * C, :], 0.0)
        intra.append(o)
        incr.append(jnp.where(blockdiag, _dot_tn(vb[sl], kd[sl]), 0.0))

    st = state_ref[...]
    outs = []
    for c in range(nc):
        outs.append(intra[c] + _dot_nt(qe[c * C:(c + 1) * C], st.astype(BF16)))
        st = st * jnp.exp(last[c]) + incr[c]
    state_ref[...] = st
    o = jnp.concatenate(outs, axis=0)

    lane_t = lax.broadcasted_iota(jnp.int32, (tm, hv), 1)
    osq = o * o
    inv = jnp.zeros_like(o)
    for h in range(GLA_HEADS):
        mh = (lane_t // GLA_DV) == h
        ms = jnp.sum(jnp.where(mh, osq, 0.0), axis=-1, keepdims=True) * (1.0 / GLA_DV)
        inv = jnp.where(mh, lax.rsqrt(ms + NORM_EPS), inv)
    g = p[:, 2 * hk + hv:2 * hk + 2 * hv]
    o_ref[...] = (o * inv * hn_ref[...] * (g * _sigmoid(g))).astype(o_ref.dtype)


def _lru(p, cw_ref, cb_ref, wax_ref, bax_ref, lam_ref, o_ref, tail_ref, h_ref, tm):
    W = LRU_WIDTH
    G = SUBLANES
    x = p[:, 0:W]
    gate = p[:, W:2 * W]
    xx = jnp.concatenate([tail_ref[...], x], axis=0)
    cw = cw_ref[...]
    xc = cb_ref[...] + cw[LRU_CONV - 1:LRU_CONV, :] * x
    for d in range(1, LRU_CONV):
        xc = xc + cw[LRU_CONV - 1 - d:LRU_CONV - d, :] * xx[G - d:G - d + tm, :]
    tail_ref[...] = x[tm - G:, :]
    ri = _dot(xc.astype(BF16), wax_ref[...]) + bax_ref[...]
    r = _sigmoid(ri[:, :W])
    ig = _sigmoid(ri[:, W:])
    lam = lam_ref[...]
    softplus_neg = jnp.maximum(-lam, 0.0) + jnp.log1p(jnp.exp(-jnp.abs(lam)))
    log_a = (-LRU_C * r) * softplus_neg
    a = jnp.exp(log_a)
    u = jnp.sqrt(-jnp.tanh(log_a) * (a * a + 1.0)) * (ig * xc)
    sub = lax.broadcasted_iota(jnp.int32, (G, W), 0)
    h = h_ref[...]
    outs = []
    for g in range(tm // G):
        ag = a[g * G:(g + 1) * G, :]
        ug = u[g * G:(g + 1) * G, :]
        for s in (1, 2, 4):
            live = sub >= s
            ug = ug + ag * jnp.where(live, pltpu.roll(ug, s, 0), 0.0)
            ag = ag * jnp.where(live, pltpu.roll(ag, s, 0), 1.0)
        hg = ug + ag * h
        outs.append(hg)
        h = hg[G - 1:G, :]
    h_ref[...] = h
    hs = jnp.concatenate(outs, axis=0)
    gelu = 0.5 * gate * (1.0 + jnp.tanh(0.7978845608028654 * (gate + 0.044715 * gate * gate * gate)))
    o_ref[...] = (hs * gelu).astype(o_ref.dtype)


def _w_in_plan():
    sp = np.cumsum([0, 256, 128, 32, 128, 128, 256, 16, 256, 256, 256, 256, 256, 256])
    mq, mkv, mkr, gq, gk, gv, ga, gg, bq, bk, bv, rx, rg = [(int(sp[n]), int(sp[n + 1] - sp[n])) for n in range(13)]
    keep = lambda c: [(c[1], c[0], 1.0)]
    zeros = lambda n: [(n, None, 0.0)]

    def swapped(start, half):
        return [(half, start + half, -1.0), (half, start, 1.0)]

    def moba_swapped(c):
        out = []
        for h in range(MOBA_HEADS):
            out += swapped(c[0] + h * MOBA_HD, MOBA_ROT // 2) + zeros(MOBA_HD - MOBA_ROT)
        return out

    plan = (keep(mq) + keep(mkv) + keep(mkr) + swapped(mkr[0], MLA_ROPE // 2) + zeros(W_MLA - 256 - 128 - 64)
            + keep(gq) + keep(gk) + keep(gv) + keep(gg) + keep(ga) + zeros(LANES - GLA_GATE_RANK)
            + keep(bq) + keep(bk) + keep(bv) + moba_swapped(bq) + moba_swapped(bk)
            + keep(rx) + keep(rg))
    assert sum(p[0] for p in plan) == W_IN
    return plan, int(sp[-1])


def _build_w_in(w_ref, o_ref, rows_per_step=256):
    plan, n_src = _w_in_plan()
    blocks, cur, room = [], [], LANES
    for width, src, sign in plan:
        while width:
            take = min(width, room)
            cur.append((take, src, sign))
            src = None if src is None else src + take
            width -= take
            room -= take
            if room == 0:
                blocks.append(cur)
                cur, room = [], LANES
    for r0 in range(0, w_ref.shape[0], rows_per_step):
        rows = slice(r0, r0 + rows_per_step)
        for j, frags in enumerate(blocks):
            vals = []
            for width, src, sign in frags:
                if src is None:
                    vals.append(jnp.zeros((rows_per_step, width), F32))
                    continue
                a0 = src // LANES * LANES
                a1 = min(-(-(src + width) // LANES) * LANES, n_src)
                win = w_ref[rows, a0:a1]
                v = win[:, src - a0:src - a0 + width]
                vals.append(-v if sign < 0 else v)
            blk = vals[0] if len(vals) == 1 else jnp.concatenate(vals, axis=1)
            o_ref[rows, j * LANES:(j + 1) * LANES] = blk.astype(BF16)


def _prep_kernel(x_ref, g_ref, w_raw_ref,
                 qg_ref, wq_ref, kvg_ref, wkv_ref, place_ref, cq_ref, sq_ref, r32_ref,
                 cm_ref, sm_ref,
                 wa_ref, ba_ref, hn_ref,
                 cw_ref, cb_ref, wax_ref, bax_ref, lam_ref,
                 mq_ref, mk_ref, mv_ref, bq_ref, bk_ref, bv_ref, yb_ref, yd_ref,
                 w_ref, km_ref, state_ref, tail_ref, h_ref, *, tm):
    si = pl.program_id(1)

    @pl.when((pl.program_id(0) == 0) & (si == 0))
    def _():
        _build_w_in(w_raw_ref, w_ref)

    @pl.when(si == 0)
    def _():
        km_ref[...] = jnp.zeros_like(km_ref)
        state_ref[...] = jnp.zeros_like(state_ref)
        tail_ref[...] = jnp.zeros_like(tail_ref)
        h_ref[...] = jnp.zeros_like(h_ref)

    h = _rms(x_ref[...], g_ref[...]).astype(BF16)
    o_mla, o_gla, o_moba, o_lru = 0, W_MLA, W_MLA + W_GLA, W_MLA + W_GLA + W_MOBA
    p_mla = _dot(h, w_ref[:, o_mla:o_mla + W_MLA])
    p_moba = _dot(h, w_ref[:, o_moba:o_moba + W_MOBA])
    _mla_prep(p_mla, qg_ref, wq_ref, kvg_ref, wkv_ref, place_ref, cq_ref, sq_ref, r32_ref, mq_ref, mk_ref, mv_ref)
    p_lru = _dot(h, w_ref[:, o_lru:o_lru + W_LRU])
    _moba_prep(p_moba, cm_ref, sm_ref, bq_ref, bk_ref, bv_ref, km_ref, si, tm)
    p_gla = _dot(h, w_ref[:, o_gla:o_gla + W_GLA])
    _lru(p_lru, cw_ref, cb_ref, wax_ref, bax_ref, lam_ref, yd_ref, tail_ref, h_ref, tm)
    _gla(p_gla, wa_ref, ba_ref, hn_ref, yb_ref, state_ref, tm)


def _prep(x, w_in, l, consts, tables, B, S, tm):
    T = x.shape[0]
    ns = S // tm
    hw = ATTN_HEADS * LANES
    tok = lambda b, s: (b * ns + s, 0)
    pos = lambda b, s: (s, 0)
    g, qg, wq, kvg, wkv, place, wa, ba, hn, cw, cb, wax, bax, lam = consts
    cq, sq, r32, cm, sm = tables
    w_spec = pl.BlockSpec((None,) + w_in.shape[1:], lambda b, s: (l, 0, 0))
    c = lambda a: _layer_spec(a, l) if a.ndim == 3 else _const_spec(a.shape)
    t = lambda a: pl.BlockSpec((tm, a.shape[1]), pos)
    nb = max(2 * SUBLANES, -(-(S // MOBA_BLOCK) // SUBLANES) * SUBLANES)
    tok_t = lambda b, s: (0, b * ns + s)
    outs = [(hw, True), (hw, False), (GROUP_WIDTH, True), (hw, True), (hw, False), (GROUP_WIDTH, True),
            (GROUP_WIDTH, False), (GROUP_WIDTH, False)]
    return pl.pallas_call(
        functools.partial(_prep_kernel, tm=tm),
        grid=(B, ns),
        in_specs=[pl.BlockSpec((tm, D_MODEL), tok), c(g), w_spec,
                  c(qg), c(wq), c(kvg), c(wkv), c(place), t(cq), t(sq), t(r32),
                  t(cm), t(sm),
                  c(wa), c(ba), c(hn),
                  c(cw), c(cb), c(wax), c(bax), c(lam)],
        out_specs=[pl.BlockSpec((wd, tm), tok_t) if tr else pl.BlockSpec((tm, wd), tok) for wd, tr in outs],
        out_shape=[jax.ShapeDtypeStruct((wd, T) if tr else (T, wd), BF16) for wd, tr in outs],
        scratch_shapes=[pltpu.VMEM((D_MODEL, W_IN), BF16),
                        pltpu.VMEM((nb, GROUP_WIDTH), F32),
                        pltpu.VMEM((GLA_HEADS * GLA_DV, GLA_HEADS * GLA_DK), F32),
                        pltpu.VMEM((SUBLANES, LRU_WIDTH), F32), pltpu.VMEM((1, LRU_WIDTH), F32)],
        compiler_params=_params(("arbitrary", "arbitrary")),
        name="mixer_prep",
    )(x, g, w_in, qg, wq, kvg, wkv, place, cq, sq, r32, cm, sm, wa, ba, hn, cw, cb, wax, bax, lam)


HEAD_V = 64
ACC_ROWS = HEAD_V + 16


def _attn_kernel(qt_ref, k_ref, vt_ref, o_ref, m_ref, acc_ref, *, tq, tk):
    i = pl.program_id(1)
    m_ref[...] = jnp.full(m_ref.shape, NEG_BIG, F32)
    acc_ref[...] = jnp.zeros(acc_ref.shape, F32)
    ones = jnp.ones((ACC_ROWS - HEAD_V, tk), BF16)

    def tile(j, diagonal):
        keys = pl.ds(pl.multiple_of(j * tk, tk), tk)
        if diagonal:
            kpos = j * tk + lax.broadcasted_iota(jnp.int32, (tk, tq), 0)
            qpos = i * tq + lax.broadcasted_iota(jnp.int32, (tk, tq), 1)
            allowed = kpos <= qpos
        scores = [_dot(k_ref[keys, h * LANES:(h + 1) * LANES], qt_ref[h * LANES:(h + 1) * LANES, :])
                  for h in range(ATTN_HEADS)]
        for h in range(ATTN_HEADS):
            vt = jnp.concatenate([vt_ref[h * HEAD_V:(h + 1) * HEAD_V, keys], ones], axis=0)
            s = scores[h]
            if diagonal:
                s = jnp.where(allowed, s, NEG_BIG)
            m_prev = m_ref[h]
            m_next = jnp.maximum(m_prev, jnp.max(s, axis=0, keepdims=True))
            alpha = jnp.exp2(m_prev - m_next)
            p = jnp.exp2(s - jnp.tile(m_next, (tk // SUBLANES, 1)))
            acc_ref[h] = jnp.tile(alpha, (ACC_ROWS // SUBLANES, 1)) * acc_ref[h] + _dot(vt, p.astype(BF16))
            m_ref[h] = m_next

    n_full = (i * tq) // tk

    def body(j, _):
        tile(j, False)
        return 0

    lax.fori_loop(0, n_full, body, 0)
    for d in range(tq // tk):
        tile(n_full + d, True)
    out_t = jnp.concatenate(
        [acc_ref[h, :HEAD_V] * jnp.tile(1.0 / acc_ref[h, HEAD_V:HEAD_V + SUBLANES], (HEAD_V // SUBLANES, 1))
         for h in range(ATTN_HEADS)], axis=0)
    o_ref[...] = out_t.T.astype(o_ref.dtype)


def _attn(qt, k, vt, B, S, tq, tk, name):
    T = k.shape[0]
    nq = S // tq
    hw = ATTN_HEADS * LANES
    stat = pltpu.VMEM((ATTN_HEADS, SUBLANES, tq), F32)
    return pl.pallas_call(
        functools.partial(_attn_kernel, tq=tq, tk=tk),
        grid=(B, nq),
        in_specs=[pl.BlockSpec((hw, tq), lambda b, i: (0, b * nq + i)),
                  pl.BlockSpec((S, hw), lambda b, i: (b, 0)),
                  pl.BlockSpec((GROUP_WIDTH, S), lambda b, i: (0, b))],
        out_specs=pl.BlockSpec((tq, GROUP_WIDTH), lambda b, i: (b * nq + i, 0)),
        out_shape=jax.ShapeDtypeStruct((T, GROUP_WIDTH), BF16),
        scratch_shapes=[stat, pltpu.VMEM((ATTN_HEADS, ACC_ROWS, tq), F32)],
        compiler_params=_params(("parallel", "arbitrary")),
        name=name,
    )(qt, k, vt)


def _ffn_kernel(x_ref, ya_ref, yb_ref, yc_ref, yd_ref, wo_ref, fg_ref, wi_ref, w2_ref, ng_ref, o_ref,
                *, final, chunk):
    W = GROUP_WIDTH
    x1 = x_ref[...]
    for n, y_ref in enumerate((ya_ref, yb_ref, yc_ref, yd_ref)):
        x1 = x1 + _dot(y_ref[...], wo_ref[n * W:(n + 1) * W, :])
    h = _rms(x1, fg_ref[...]).astype(BF16)
    parts = []
    for c in range(FFN_HIDDEN // chunk):
        g = _dot(h, wi_ref[:, c * chunk:(c + 1) * chunk])
        up = _dot(h, wi_ref[:, FFN_HIDDEN + c * chunk:FFN_HIDDEN + (c + 1) * chunk])
        act = (g * _sigmoid(g) * up).astype(BF16)
        parts.append(_dot(act, w2_ref[c * chunk:(c + 1) * chunk, :]))
    acc = x1 + sum(parts[1:], parts[0])
    if final:
        acc = _rms(acc, ng_ref[...])
    o_ref[...] = acc


def _layer_spec(stacked, l):
    nd = stacked.ndim - 1
    return pl.BlockSpec((None,) + stacked.shape[1:], lambda *_: (l,) + (0,) * nd, pipeline_mode=pl.Buffered(1))


def _ffn(x, ya, yb, yc, yd, wo, fg, wi, w2, ng, l, final, tm, chunk):
    T = x.shape[0]
    tok = lambda i: (i, 0)
    ysp = pl.BlockSpec((tm, GROUP_WIDTH), tok)
    return pl.pallas_call(
        functools.partial(_ffn_kernel, final=final, chunk=chunk),
        grid=(T // tm,),
        in_specs=[pl.BlockSpec((tm, D_MODEL), tok), ysp, ysp, ysp, ysp,
                  _layer_spec(wo, l), _layer_spec(fg, l), _layer_spec(wi, l),
                  _layer_spec(w2, l), _const_spec(ng.shape)],
        out_specs=pl.BlockSpec((tm, D_MODEL), tok),
        out_shape=jax.ShapeDtypeStruct((T, D_MODEL), F32),
        compiler_params=_params(("parallel",)),
        name="outproj_ffn",
    )(x, ya, yb, yc, yd, wo, fg, wi, w2, ng)


def _swap_cols(w, half):
    return jnp.concatenate([-w[..., half:2 * half], w[..., :half]], axis=-1)


def _layout_mla(w_uq, w_ukv):
    zq = jnp.zeros(w_uq.shape[:-1] + (LANES - MLA_NOPE - MLA_ROPE,), w_uq.dtype)
    zr = jnp.zeros(w_uq.shape[:-1] + (MLA_NOPE,), w_uq.dtype)
    qd = MLA_NOPE + MLA_ROPE
    plain, swapped = [], []
    for h in range(MLA_HEADS):
        nope = w_uq[..., h * qd:h * qd + MLA_NOPE]
        rope = w_uq[..., h * qd + MLA_NOPE:(h + 1) * qd]
        plain += [nope, rope, zq]
        swapped += [zr, _swap_cols(rope, MLA_ROPE // 2), zq]
    wq = jnp.concatenate(plain + swapped, axis=-1).astype(BF16)
    zk = jnp.zeros(w_ukv.shape[:-1] + (LANES - MLA_NOPE,), w_ukv.dtype)
    kd = MLA_NOPE + MLA_V
    kparts = []
    vparts = []
    for h in range(MLA_HEADS):
        kparts += [w_ukv[..., h * kd:h * kd + MLA_NOPE], zk]
        vparts += [w_ukv[..., h * kd + MLA_NOPE:(h + 1) * kd]]
    wkv = jnp.concatenate(kparts + vparts, axis=-1).astype(BF16)
    place = np.zeros((MLA_ROPE, MLA_HEADS * LANES), np.float32)
    for h in range(MLA_HEADS):
        place[np.arange(MLA_ROPE), h * LANES + MLA_NOPE + np.arange(MLA_ROPE)] = 1.0
    return wq, wkv, jnp.asarray(place, BF16)


def _tables(S):
    def cs(dim):
        inv_freq = ROPE_THETA ** (-jnp.arange(0, dim, 2, dtype=F32) / dim)
        ang = jnp.arange(S, dtype=F32)[:, None] * inv_freq[None, :]
        return jnp.cos(ang), jnp.sin(ang)

    c, s = cs(MLA_ROPE)
    scale = (MLA_NOPE + MLA_ROPE) ** -0.5 * LOG2E
    one = jnp.ones((S, MLA_NOPE), F32)
    zero = jnp.zeros((S, MLA_NOPE), F32)
    pad1 = jnp.ones((S, LANES - MLA_NOPE - MLA_ROPE), F32)
    pad0 = jnp.zeros((S, LANES - MLA_NOPE - MLA_ROPE), F32)
    cq = jnp.concatenate([one, c, c, pad1], axis=1) * scale
    sq = jnp.concatenate([zero, s, s, pad0], axis=1) * scale
    r32 = jnp.concatenate([c, c, s, s], axis=1)
    c, s = cs(MOBA_ROT)
    one = jnp.ones((S, MOBA_HD - MOBA_ROT), F32)
    zero = jnp.zeros((S, MOBA_HD - MOBA_ROT), F32)
    cm = jnp.concatenate([c, c, one] * 2, axis=1)
    sm = jnp.concatenate([s, s, zero] * 2, axis=1)
    return cq, sq, r32, cm, sm


def _block_diag(w):
    depth, n, c, d = w.shape
    rows = []
    for j in range(n):
        rows.append(jnp.concatenate([jnp.zeros((depth, c, j * d), w.dtype), w[:, j],
                                     jnp.zeros((depth, c, (n - 1 - j) * d), w.dtype)], axis=-1))
    return jnp.concatenate(rows, axis=1)


def _stacked_consts(attn_norm, mla_q_norm, mla_w_uq, mla_kv_norm, mla_w_ukv, gla_w_a2, gla_b_a2,
                    gla_head_norm, lru_conv_w, lru_conv_b, lru_w_a, lru_b_a, lru_w_x, lru_b_x, lru_lambda):
    depth = attn_norm.shape[0]
    r3 = lambda v: v.reshape(depth, 1, -1).astype(F32)
    wq, wkv, place = _layout_mla(mla_w_uq, mla_w_ukv)
    wa = jnp.pad(gla_w_a2, ((0, 0), (0, LANES - GLA_GATE_RANK), (0, 0))).astype(BF16)
    hn = jnp.tile(r3(gla_head_norm), (1, 1, GLA_HEADS))
    wax = jnp.concatenate([_block_diag(lru_w_a), _block_diag(lru_w_x)], axis=-1).astype(BF16)
    bax = jnp.concatenate([r3(lru_b_a), r3(lru_b_x)], axis=-1)
    return (r3(attn_norm), r3(mla_q_norm), wq, r3(mla_kv_norm), wkv, place,
            wa, r3(gla_b_a2), hn, lru_conv_w.reshape(depth, LRU_CONV, LRU_WIDTH), r3(lru_conv_b), wax, bax,
            r3(lru_lambda))


def kernel(x, attn_norm, w_in, mla_q_norm, mla_w_uq, mla_kv_norm, mla_w_ukv, gla_w_a2, gla_b_a2, gla_head_norm,
           lru_conv_w, lru_conv_b, lru_w_a, lru_b_a, lru_w_x, lru_b_x, lru_lambda, w_out, ffn_norm, w_ffn_in,
           w_ffn_out, final_norm):
    B, S, D = x.shape
    depth = w_in.shape[0]
    T = B * S
    assert D == D_MODEL and S % max(PREP_TM, ATTN_TQ, FFN_TM) == 0 and S // MOBA_BLOCK <= MOBA_SLOT
    tables = _tables(S)
    row = lambda v: v.reshape(1, -1).astype(F32)
    xt = x.reshape(T, D)
    wo, wi, w2 = w_out.astype(BF16), w_ffn_in.astype(BF16), w_ffn_out.astype(BF16)
    fg = ffn_norm.reshape(depth, 1, D).astype(F32)
    consts = _stacked_consts(attn_norm, mla_q_norm, mla_w_uq, mla_kv_norm, mla_w_ukv, gla_w_a2, gla_b_a2,
                             gla_head_norm, lru_conv_w, lru_conv_b, lru_w_a, lru_b_a, lru_w_x, lru_b_x, lru_lambda)
    for l in range(depth):
        mq, mk, mv, bq, bk, bv, y_b, y_d = _prep(xt, w_in, l, consts, tables, B, S, PREP_TM)
        y_a = _attn(mq, mk, mv, B, S, ATTN_TQ, ATTN_TK, "mla_attn")
        y_c = _attn(bq, bk, bv, B, S, ATTN_TQ, ATTN_TK, "moba_attn")
        xt = _ffn(xt, y_a, y_b, y_c, y_d, wo, fg, wi, w2, row(final_norm), l, l == depth - 1,
                  FFN_TM, FFN_HIDDEN // 2)
    return xt.reshape(B, S, D)
```

```python
import functools

import numpy as np
import jax
import jax.numpy as jnp
from jax import lax
from jax.experimental import pallas as pl
from jax.experimental.pallas import tpu as pltpu

F32 = jnp.float32
BF16 = jnp.bfloat16

D_MODEL = 1024
GROUP_WIDTH = 256
ROPE_THETA = 500000.0
NORM_EPS = 1e-6
MLA_HEADS = 4
MLA_V = 64
MLA_NOPE = 64
MLA_ROPE = 32
MLA_Q_RANK = 256
MLA_KV_RANK = 128
GLA_HEADS = 4
GLA_DV = 64
GLA_DK = 32
GLA_GATE_RANK = 16
GLA_TAU = 16.0
GLA_CHUNK = 64
MOBA_HEADS = 4
MOBA_HD = 64
MOBA_ROT = 16
MOBA_BLOCK = 256
MOBA_TOPK = 3
LRU_WIDTH = 256
LRU_BLOCKS = 4
LRU_BW = 64
LRU_CONV = 4
LRU_C = 8.0
FFN_HIDDEN = 2816

LANES = 128
SUBLANES = 8
VMEM_LIMIT = 56 * 1024 * 1024

W_MLA = 512
W_GLA = 896
W_MOBA = 1280
W_LRU = 512
W_IN = W_MLA + W_GLA + W_MOBA + W_LRU
NEG_BIG = -1e30
LOG2E = 1.4426950408889634
MOBA_SLOT = 32
PREP_TM = 512
FFN_TM = 1024
ATTN_HEADS = 4
ATTN_TQ = 1024
ATTN_TK = 512


def _rms(x, g):
    return x * lax.rsqrt(jnp.mean(x * x, axis=-1, keepdims=True) + NORM_EPS) * g


def _dot(a, b):
    return jnp.dot(a, b, preferred_element_type=F32)


def _dot_nt(a, b):
    return lax.dot_general(a, b, (((1,), (1,)), ((), ())), preferred_element_type=F32)


def _dot_tn(a, b):
    return lax.dot_general(a, b, (((0,), (0,)), ((), ())), preferred_element_type=F32)


def _split3(x):
    hi = x.astype(BF16)
    r = x - hi.astype(F32)
    mid = r.astype(BF16)
    lo = (r - mid.astype(F32)).astype(BF16)
    return hi, mid, lo


def _sigmoid(x):
    return 0.5 * jnp.tanh(0.5 * x) + 0.5


def _const_spec(shape):
    nd = len(shape)
    return pl.BlockSpec(shape, lambda *_: (0,) * nd, pipeline_mode=pl.Buffered(1))


def _params(sem):
    return pltpu.CompilerParams(dimension_semantics=sem, vmem_limit_bytes=VMEM_LIMIT)


def _mla_prep(p, qg_ref, wq_ref, kvg_ref, wkv_ref, place_ref, cq_ref, sq_ref, r32_ref, qt_ref, k_ref, vt_ref):
    hw = MLA_HEADS * LANES
    nq = _rms(p[:, :MLA_Q_RANK], qg_ref[...]).astype(BF16)
    q2 = _dot(nq, wq_ref[...])
    cq = cq_ref[...]
    sq = sq_ref[...]
    q = jnp.concatenate([q2[:, h * LANES:(h + 1) * LANES] * cq + q2[:, hw + h * LANES:hw + (h + 1) * LANES] * sq
                         for h in range(MLA_HEADS)], axis=1)
    qt_ref[...] = q.T.astype(BF16)
    nkv = _rms(p[:, MLA_Q_RANK:MLA_Q_RANK + MLA_KV_RANK], kvg_ref[...]).astype(BF16)
    kv = _dot(nkv, wkv_ref[...])
    c0 = MLA_Q_RANK + MLA_KV_RANK
    r32 = r32_ref[...]
    k_pe = p[:, c0:c0 + MLA_ROPE] * r32[:, :MLA_ROPE] + p[:, c0 + MLA_ROPE:c0 + 2 * MLA_ROPE] * r32[:, MLA_ROPE:]
    k_ref[...] = (kv[:, :hw] + _dot(k_pe.astype(BF16), place_ref[...])).astype(BF16)
    vt_ref[...] = kv[:, hw:].T.astype(BF16)


def _moba_prep(p, cm_ref, sm_ref, qt_ref, k_ref, vt_ref, km_ref, si, tm):
    W = GROUP_WIDTH
    HD = MOBA_HD
    nb = km_ref.shape[0]
    cm = jnp.concatenate([cm_ref[...]] * (W // LANES), axis=1)
    sm = jnp.concatenate([sm_ref[...]] * (W // LANES), axis=1)
    q = p[:, 0:W] * cm + p[:, 3 * W:4 * W] * sm
    k = p[:, W:2 * W] * cm + p[:, 4 * W:5 * W] * sm
    vt_ref[...] = p[:, 2 * W:3 * W].T.astype(BF16)
    nblk = tm // MOBA_BLOCK
    for j in range(nblk):
        mean = jnp.sum(k[j * MOBA_BLOCK:(j + 1) * MOBA_BLOCK, :], axis=0, keepdims=True) * (1.0 / MOBA_BLOCK)
        km_ref[pl.ds(si * nblk + j, 1), :] = mean
    km = km_ref[...]
    n_idx = lax.broadcasted_iota(jnp.int32, (nb, tm), 0)
    blk = (si * tm + lax.broadcasted_iota(jnp.int32, (nb, tm), 1)) // MOBA_BLOCK
    past = n_idx < blk
    own = n_idx == blk
    n_f = n_idx.astype(F32)
    lane_m = lax.broadcasted_iota(jnp.int32, (nb, W), 1)
    km_heads = jnp.concatenate([jnp.where((lane_m // HD) == h, km, 0.0) for h in range(MOBA_HEADS)], axis=0)
    m_hi, m_mid, m_lo = _split3(km_heads)
    q_hi, q_mid, q_lo = _split3(q)
    nm = MOBA_HEADS * nb
    g_hi = _dot_nt(jnp.concatenate([m_hi, m_mid, m_lo], axis=0), q_hi)
    g_mid = _dot_nt(jnp.concatenate([m_hi, m_mid], axis=0), q_mid)
    g_lo = _dot_nt(m_hi, q_lo)
    gates = (((g_lo + g_hi[2 * nm:]) + g_mid[nm:]) + (g_mid[:nm] + g_hi[nm:2 * nm])) + g_hi[:nm]
    masks = []
    for h in range(MOBA_HEADS):
        gate = jnp.where(past, gates[h * nb:(h + 1) * nb], -jnp.inf)
        keep = own
        for _ in range(MOBA_TOPK):
            mx = jnp.max(gate, axis=0, keepdims=True)
            first = jnp.min(jnp.where(gate == mx, n_f, float(LANES)), axis=0, keepdims=True)
            pick = (n_f == first) & (mx > -jnp.inf)
            keep = keep | pick
            gate = jnp.where(pick, -jnp.inf, gate)
        masks.append(jnp.where(keep, 0.0, NEG_BIG))
        if nb < MOBA_SLOT:
            masks.append(jnp.zeros((MOBA_SLOT - nb, tm), F32))
    mask_hi = jnp.concatenate(masks, axis=0).T
    mask_lo = pltpu.roll(mask_hi, HD, 1)
    lane = lax.broadcasted_iota(jnp.int32, (tm, LANES), 1)
    head_lanes = lane < HD
    q_scaled = q * (MOBA_HD ** -0.5 * LOG2E)
    own_blk = (si * tm + lax.broadcasted_iota(jnp.int32, (tm, LANES), 0)) // MOBA_BLOCK
    kone = [(lane == own_blk + HD + slot * MOBA_SLOT).astype(F32) for slot in range(2)]
    q_ext = []
    for h in range(MOBA_HEADS):
        pr = h // 2
        q_pair = q_scaled[:, pr * LANES:(pr + 1) * LANES]
        k_pair = k[:, pr * LANES:(pr + 1) * LANES]
        if h % 2:
            q_pair = pltpu.roll(q_pair, HD, 1)
            k_pair = pltpu.roll(k_pair, HD, 1)
        q_ext.append(jnp.where(head_lanes, q_pair, mask_lo if h < 2 else mask_hi))
        k_ref[:, h * LANES:(h + 1) * LANES] = jnp.where(
            head_lanes, k_pair, kone[h % 2]).astype(BF16)
    qt_ref[...] = jnp.concatenate(q_ext, axis=1).T.astype(BF16)


def _gla(p, wa_ref, ba_ref, hn_ref, o_ref, state_ref, tm):
    C = GLA_CHUNK
    hk = GLA_HEADS * GLA_DK
    hv = GLA_HEADS * GLA_DV
    nc = tm // C
    rc = lax.broadcasted_iota(jnp.int32, (tm, hk), 0) % C
    lane_k = lax.broadcasted_iota(jnp.int32, (C, hk), 1)
    lane_v = lax.broadcasted_iota(jnp.int32, (C, hv), 1)
    row4 = lax.broadcasted_iota(jnp.int32, (GLA_HEADS * C, C), 0)
    col4 = lax.broadcasted_iota(jnp.int32, (GLA_HEADS * C, C), 1)
    causal4 = (row4 % C) >= col4
    sr = lax.broadcasted_iota(jnp.int32, (hv, hk), 0)
    sc = lax.broadcasted_iota(jnp.int32, (hv, hk), 1)
    blockdiag = (sr // GLA_DV) == (sc // GLA_DK)
    scale = GLA_DK ** -0.5

    q = p[:, 0:hk]
    k = p[:, hk:2 * hk]
    vb = p[:, 2 * hk:2 * hk + hv].astype(BF16)
    a_low = p[:, 2 * hk + 2 * hv:2 * hk + 2 * hv + LANES]
    a_lin = _dot(a_low.astype(BF16), wa_ref[...]) + ba_ref[...]
    b = (jnp.minimum(a_lin, 0.0) - jnp.log(1.0 + jnp.exp(-jnp.abs(a_lin)))) * (1.0 / GLA_TAU)
    for s in (1, 2, 4, 8, 16, 32):
        b = b + jnp.where(rc >= s, pltpu.roll(b, s, 0), 0.0)
    last = [b[c * C + C - 1:(c + 1) * C, :] for c in range(nc)]
    bl = jnp.concatenate([jnp.broadcast_to(r, (C, hk)) for r in last], axis=0)
    bref = 0.5 * bl
    qf = q * scale
    qs = qf * jnp.exp(b - bref)
    ks = (k * jnp.exp(bref - b)).astype(BF16)
    kd = (k * jnp.exp(bl - b)).astype(BF16)
    qe = (qf * jnp.exp(b)).astype(BF16)

    intra, incr = [], []
    for c in range(nc):
        sl = slice(c * C, (c + 1) * C)
        qs4 = jnp.concatenate(
            [jnp.where((lane_k // GLA_DK) == h, qs[sl], 0.0) for h in range(GLA_HEADS)], axis=0).astype(BF16)
        att = jnp.where(causal4, _dot_nt(qs4, ks[sl]), 0.0)
        res = _dot(att.astype(BF16), vb[sl])
        o = jnp.where((lane_v // GLA_DV) == 0, res[0:C, :], 0.0)
        for h in range(1, GLA_HEADS):
            o = o + jnp.where((lane_v // GLA_DV) == h, res[h * C:(h + 1) * C, :], 0.0)
        intra.append(o)
        incr.append(jnp.where(blockdiag, _dot_tn(vb[sl], kd[sl]), 0.0))

    st = state_ref[...]
    outs = []
    for c in range(nc):
        outs.append(intra[c] + _dot_nt(qe[c * C:(c + 1) * C], st.astype(BF16)))
        st = st * jnp.exp(last[c]) + incr[c]
    state_ref[...] = st
    o = jnp.concatenate(outs, axis=0)

    lane_t = lax.broadcasted_iota(jnp.int32, (tm, hv), 1)
    osq = o * o
    inv = jnp.zeros_like(o)
    for h in range(GLA_HEADS):
        mh = (lane_t // GLA_DV) == h
        ms = jnp.sum(jnp.where(mh, osq, 0.0), axis=-1, keepdims=True) * (1.0 / GLA_DV)
        inv = jnp.where(mh, lax.rsqrt(ms + NORM_EPS), inv)
    g = p[:, 2 * hk + hv:2 * hk + 2 * hv]
    o_ref[...] = (o * inv * hn_ref[...] * (g * _sigmoid(g))).astype(o_ref.dtype)


def _lru(p, cw_ref, cb_ref, wax_ref, bax_ref, lam_ref, o_ref, xs_ref, h_ref, tm):
    W = LRU_WIDTH
    G = SUBLANES
    x = p[:, 0:W]
    gate = p[:, W:2 * W]
    xs_ref[G:, :] = x
    cw = cw_ref[...]
    xc = cb_ref[...] + cw[LRU_CONV - 1:LRU_CONV, :] * x
    for d in range(1, LRU_CONV):
        xc = xc + cw[LRU_CONV - 1 - d:LRU_CONV - d, :] * xs_ref[G - d:G - d + tm, :]
    xs_ref[:G, :] = x[tm - G:, :]
    ri = _dot(xc.astype(BF16), wax_ref[...]) + bax_ref[...]
    r = _sigmoid(ri[:, :W])
    ig = _sigmoid(ri[:, W:])
    lam = lam_ref[...]
    softplus_neg = jnp.maximum(-lam, 0.0) + jnp.log1p(jnp.exp(-jnp.abs(lam)))
    log_a = (-LRU_C * r) * softplus_neg
    a = jnp.exp(log_a)
    u = jnp.sqrt(-jnp.tanh(log_a) * (a * a + 1.0)) * (ig * xc)
    sub = lax.broadcasted_iota(jnp.int32, (G, W), 0)
    h = h_ref[...]
    outs = []
    for g in range(tm // G):
        ag = a[g * G:(g + 1) * G, :]
        ug = u[g * G:(g + 1) * G, :]
        for s in (1, 2, 4):
            live = sub >= s
            ug = ug + ag * jnp.where(live, pltpu.roll(ug, s, 0), 0.0)
            ag = ag * jnp.where(live, pltpu.roll(ag, s, 0), 1.0)
        hg = ug + ag * h
        outs.append(hg)
        h = hg[G - 1:G, :]
    h_ref[...] = h
    hs = jnp.concatenate(outs, axis=0)
    gelu = 0.5 * gate * (1.0 + jnp.tanh(0.7978845608028654 * (gate + 0.044715 * gate * gate * gate)))
    o_ref[...] = (hs * gelu).astype(o_ref.dtype)


def _w_in_plan():
    sp = np.cumsum([0, 256, 128, 32, 128, 128, 256, 16, 256, 256, 256, 256, 256, 256])
    mq, mkv, mkr, gq, gk, gv, ga, gg, bq, bk, bv, rx, rg = [(int(sp[n]), int(sp[n + 1] - sp[n])) for n in range(13)]
    keep = lambda c: [(c[1], c[0], 1.0)]
    zeros = lambda n: [(n, None, 0.0)]

    def swapped(start, half):
        return [(half, start + half, -1.0), (half, start, 1.0)]

    def moba_swapped(c):
        out = []
        for h in range(MOBA_HEADS):
            out += swapped(c[0] + h * MOBA_HD, MOBA_ROT // 2) + zeros(MOBA_HD - MOBA_ROT)
        return out

    plan = (keep(mq) + keep(mkv) + keep(mkr) + swapped(mkr[0], MLA_ROPE // 2) + zeros(W_MLA - 256 - 128 - 64)
            + keep(gq) + keep(gk) + keep(gv) + keep(gg) + keep(ga) + zeros(LANES - GLA_GATE_RANK)
            + keep(bq) + keep(bk) + keep(bv) + moba_swapped(bq) + moba_swapped(bk)
            + keep(rx) + keep(rg))
    assert sum(p[0] for p in plan) == W_IN
    return plan, int(sp[-1])


def _build_w_in(w_ref, o_ref, rows_per_step=256):
    plan, n_src = _w_in_plan()
    blocks, cur, room = [], [], LANES
    for width, src, sign in plan:
        while width:
            take = min(width, room)
            cur.append((take, src, sign))
            src = None if src is None else src + take
            width -= take
            room -= take
            if room == 0:
                blocks.append(cur)
                cur, room = [], LANES
    for r0 in range(0, w_ref.shape[0], rows_per_step):
        rows = slice(r0, r0 + rows_per_step)
        for j, frags in enumerate(blocks):
            vals = []
            for width, src, sign in frags:
                if src is None:
                    vals.append(jnp.zeros((rows_per_step, width), F32))
                    continue
                a0 = src // LANES * LANES
                a1 = min(-(-(src + width) // LANES) * LANES, n_src)
                win = w_ref[rows, a0:a1]
                v = win[:, src - a0:src - a0 + width]
                vals.append(-v if sign < 0 else v)
            blk = vals[0] if len(vals) == 1 else jnp.concatenate(vals, axis=1)
            o_ref[rows, j * LANES:(j + 1) * LANES] = blk.astype(BF16)


def _prep_kernel(x_ref, g_ref, w_raw_ref,
                 qg_ref, wq_ref, kvg_ref, wkv_ref, place_ref, cq_ref, sq_ref, r32_ref,
                 cm_ref, sm_ref,
                 wa_ref, ba_ref, hn_ref,
                 cw_ref, cb_ref, wax_ref, bax_ref, lam_ref,
                 mq_ref, mk_ref, mv_ref, bq_ref, bk_ref, bv_ref, yb_ref, yd_ref,
                 w_ref, km_ref, state_ref, xs_ref, h_ref, *, tm):
    si = pl.program_id(1)

    @pl.when((pl.program_id(0) == 0) & (si == 0))
    def _():
        _build_w_in(w_raw_ref, w_ref)

    @pl.when(si == 0)
    def _():
        km_ref[...] = jnp.zeros_like(km_ref)
        state_ref[...] = jnp.zeros_like(state_ref)
        xs_ref[:SUBLANES, :] = jnp.zeros((SUBLANES, LRU_WIDTH), F32)
        h_ref[...] = jnp.zeros_like(h_ref)

    h = _rms(x_ref[...], g_ref[...]).astype(BF16)
    o_mla, o_gla, o_moba, o_lru = 0, W_MLA, W_MLA + W_GLA, W_MLA + W_GLA + W_MOBA
    p_mla = _dot(h, w_ref[:, o_mla:o_mla + W_MLA])
    p_moba = _dot(h, w_ref[:, o_moba:o_moba + W_MOBA])
    _mla_prep(p_mla, qg_ref, wq_ref, kvg_ref, wkv_ref, place_ref, cq_ref, sq_ref, r32_ref, mq_ref, mk_ref, mv_ref)
    p_lru = _dot(h, w_ref[:, o_lru:o_lru + W_LRU])
    _moba_prep(p_moba, cm_ref, sm_ref, bq_ref, bk_ref, bv_ref, km_ref, si, tm)
    p_gla = _dot(h, w_ref[:, o_gla:o_gla + W_GLA])
    _lru(p_lru, cw_ref, cb_ref, wax_ref, bax_ref, lam_ref, yd_ref, xs_ref, h_ref, tm)
    _gla(p_gla, wa_ref, ba_ref, hn_ref, yb_ref, state_ref, tm)


def _prep(x, w_in, l, consts, tables, B, S, tm):
    T = x.shape[0]
    ns = S // tm
    hw = ATTN_HEADS * LANES
    tok = lambda b, s: (b * ns + s, 0)
    pos = lambda b, s: (s, 0)
    g, qg, wq, kvg, wkv, place, wa, ba, hn, cw, cb, wax, bax, lam = consts
    cq, sq, r32, cm, sm = tables
    w_spec = pl.BlockSpec((None,) + w_in.shape[1:], lambda b, s: (l, 0, 0))
    c = lambda a: _layer_spec(a, l) if a.ndim == 3 else _const_spec(a.shape)
    t = lambda a: pl.BlockSpec((tm, a.shape[1]), pos)
    nb = max(2 * SUBLANES, -(-(S // MOBA_BLOCK) // SUBLANES) * SUBLANES)
    tok_t = lambda b, s: (0, b * ns + s)
    outs = [(hw, True), (hw, False), (GROUP_WIDTH, True), (hw, True), (hw, False), (GROUP_WIDTH, True),
            (GROUP_WIDTH, False), (GROUP_WIDTH, False)]
    return pl.pallas_call(
        functools.partial(_prep_kernel, tm=tm),
        grid=(B, ns),
        in_specs=[pl.BlockSpec((tm, D_MODEL), tok), c(g), w_spec,
                  c(qg), c(wq), c(kvg), c(wkv), c(place), t(cq), t(sq), t(r32),
                  t(cm), t(sm),
                  c(wa), c(ba), c(hn),
                  c(cw), c(cb), c(wax), c(bax), c(lam)],
        out_specs=[pl.BlockSpec((wd, tm), tok_t) if tr else pl.BlockSpec((tm, wd), tok) for wd, tr in outs],
        out_shape=[jax.ShapeDtypeStruct((wd, T) if tr else (T, wd), BF16) for wd, tr in outs],
        scratch_shapes=[pltpu.VMEM((D_MODEL, W_IN), BF16),
                        pltpu.VMEM((nb, GROUP_WIDTH), F32),
                        pltpu.VMEM((GLA_HEADS * GLA_DV, GLA_HEADS * GLA_DK), F32),
                        pltpu.VMEM((SUBLANES + tm, LRU_WIDTH), F32), pltpu.VMEM((1, LRU_WIDTH), F32)],
        compiler_params=_params(("arbitrary", "arbitrary")),
        name="mixer_prep",
    )(x, g, w_in, qg, wq, kvg, wkv, place, cq, sq, r32, cm, sm, wa, ba, hn, cw, cb, wax, bax, lam)


HEAD_V = 64
ACC_ROWS = HEAD_V + 16


def _attn_kernel(qt_ref, k_ref, vt_ref, o_ref, m_ref, acc_ref, *, tq, tk):
    i = pl.program_id(1)
    m_ref[...] = jnp.full(m_ref.shape, NEG_BIG, F32)
    acc_ref[...] = jnp.zeros(acc_ref.shape, F32)
    ones = jnp.ones((ACC_ROWS - HEAD_V, tk), BF16)

    def tile(j, diagonal):
        keys = pl.ds(pl.multiple_of(j * tk, tk), tk)
        if diagonal:
            kpos = j * tk + lax.broadcasted_iota(jnp.int32, (tk, tq), 0)
            qpos = i * tq + lax.broadcasted_iota(jnp.int32, (tk, tq), 1)
            allowed = kpos <= qpos
        scores = [_dot(k_ref[keys, h * LANES:(h + 1) * LANES], qt_ref[h * LANES:(h + 1) * LANES, :])
                  for h in range(ATTN_HEADS)]
        for h in range(ATTN_HEADS):
            vt = jnp.concatenate([vt_ref[h * HEAD_V:(h + 1) * HEAD_V, keys], ones], axis=0)
            s = scores[h]
            if diagonal:
                s = jnp.where(allowed, s, NEG_BIG)
            m_prev = m_ref[h]
            m_next = jnp.maximum(m_prev, jnp.max(s, axis=0, keepdims=True))
            alpha = jnp.exp2(m_prev - m_next)
            p = jnp.exp2(s - jnp.tile(m_next, (tk // SUBLANES, 1)))
            acc_ref[h] = jnp.tile(alpha, (ACC_ROWS // SUBLANES, 1)) * acc_ref[h] + _dot(vt, p.astype(BF16))
            m_ref[h] = m_next

    n_full = (i * tq) // tk

    def body(j, _):
        tile(j, False)
        return 0

    lax.fori_loop(0, n_full, body, 0)
    for d in range(tq // tk):
        tile(n_full + d, True)
    out_t = jnp.concatenate(
        [acc_ref[h, :HEAD_V] * jnp.tile(1.0 / acc_ref[h, HEAD_V:HEAD_V + SUBLANES], (HEAD_V // SUBLANES, 1))
         for h in range(ATTN_HEADS)], axis=0)
    o_ref[...] = out_t.T.astype(o_ref.dtype)


def _attn(qt, k, vt, B, S, tq, tk, name):
    T = k.shape[0]
    nq = S // tq
    hw = ATTN_HEADS * LANES
    stat = pltpu.VMEM((ATTN_HEADS, SUBLANES, tq), F32)
    return pl.pallas_call(
        functools.partial(_attn_kernel, tq=tq, tk=tk),
        grid=(B, nq),
        in_specs=[pl.BlockSpec((hw, tq), lambda b, i: (0, b * nq + i)),
                  pl.BlockSpec((S, hw), lambda b, i: (b, 0)),
                  pl.BlockSpec((GROUP_WIDTH, S), lambda b, i: (0, b))],
        out_specs=pl.BlockSpec((tq, GROUP_WIDTH), lambda b, i: (b * nq + i, 0)),
        out_shape=jax.ShapeDtypeStruct((T, GROUP_WIDTH), BF16),
        scratch_shapes=[stat, pltpu.VMEM((ATTN_HEADS, ACC_ROWS, tq), F32)],
        compiler_params=_params(("parallel", "arbitrary")),
        name=name,
    )(qt, k, vt)


def _ffn_kernel(x_ref, ya_ref, yb_ref, yc_ref, yd_ref, wo_ref, fg_ref, wi_ref, w2_ref, ng_ref, o_ref,
                *, final, chunk):
    W = GROUP_WIDTH
    x1 = x_ref[...]
    for n, y_ref in enumerate((ya_ref, yb_ref, yc_ref, yd_ref)):
        x1 = x1 + _dot(y_ref[...], wo_ref[n * W:(n + 1) * W, :])
    h = _rms(x1, fg_ref[...]).astype(BF16)
    parts = []
    for c in range(FFN_HIDDEN // chunk):
        g = _dot(h, wi_ref[:, c * chunk:(c + 1) * chunk])
        up = _dot(h, wi_ref[:, FFN_HIDDEN + c * chunk:FFN_HIDDEN + (c + 1) * chunk])
        act = (g * _sigmoid(g) * up).astype(BF16)
        parts.append(_dot(act, w2_ref[c * chunk:(c + 1) * chunk, :]))
    acc = x1 + sum(parts[1:], parts[0])
    if final:
        acc = _rms(acc, ng_ref[...])
    o_ref[...] = acc


def _layer_spec(stacked, l):
    nd = stacked.ndim - 1
    return pl.BlockSpec((None,) + stacked.shape[1:], lambda *_: (l,) + (0,) * nd, pipeline_mode=pl.Buffered(1))


def _ffn(x, ya, yb, yc, yd, wo, fg, wi, w2, ng, l, final, tm, chunk):
    T = x.shape[0]
    tok = lambda i: (i, 0)
    ysp = pl.BlockSpec((tm, GROUP_WIDTH), tok)
    return pl.pallas_call(
        functools.partial(_ffn_kernel, final=final, chunk=chunk),
        grid=(T // tm,),
        in_specs=[pl.BlockSpec((tm, D_MODEL), tok), ysp, ysp, ysp, ysp,
                  _layer_spec(wo, l), _layer_spec(fg, l), _layer_spec(wi, l),
                  _layer_spec(w2, l), _const_spec(ng.shape)],
        out_specs=pl.BlockSpec((tm, D_MODEL), tok),
        out_shape=jax.ShapeDtypeStruct((T, D_MODEL), F32),
        compiler_params=_params(("parallel",)),
        name="outproj_ffn",
    )(x, ya, yb, yc, yd, wo, fg, wi, w2, ng)


def _swap_cols(w, half):
    return jnp.concatenate([-w[..., half:2 * half], w[..., :half]], axis=-1)


def _layout_mla(w_uq, w_ukv):
    zq = jnp.zeros(w_uq.shape[:-1] + (LANES - MLA_NOPE - MLA_ROPE,), w_uq.dtype)
    zr = jnp.zeros(w_uq.shape[:-1] + (MLA_NOPE,), w_uq.dtype)
    qd = MLA_NOPE + MLA_ROPE
    plain, swapped = [], []
    for h in range(MLA_HEADS):
        nope = w_uq[..., h * qd:h * qd + MLA_NOPE]
        rope = w_uq[..., h * qd + MLA_NOPE:(h + 1) * qd]
        plain += [nope, rope, zq]
        swapped += [zr, _swap_cols(rope, MLA_ROPE // 2), zq]
    wq = jnp.concatenate(plain + swapped, axis=-1).astype(BF16)
    zk = jnp.zeros(w_ukv.shape[:-1] + (LANES - MLA_NOPE,), w_ukv.dtype)
    kd = MLA_NOPE + MLA_V
    kparts = []
    vparts = []
    for h in range(MLA_HEADS):
        kparts += [w_ukv[..., h * kd:h * kd + MLA_NOPE], zk]
        vparts += [w_ukv[..., h * kd + MLA_NOPE:(h + 1) * kd]]
    wkv = jnp.concatenate(kparts + vparts, axis=-1).astype(BF16)
    place = np.zeros((MLA_ROPE, MLA_HEADS * LANES), np.float32)
    for h in range(MLA_HEADS):
        place[np.arange(MLA_ROPE), h * LANES + MLA_NOPE + np.arange(MLA_ROPE)] = 1.0
    return wq, wkv, jnp.asarray(place, BF16)


def _tables(S):
    def cs(dim):
        inv_freq = ROPE_THETA ** (-jnp.arange(0, dim, 2, dtype=F32) / dim)
        ang = jnp.arange(S, dtype=F32)[:, None] * inv_freq[None, :]
        return jnp.cos(ang), jnp.sin(ang)

    c, s = cs(MLA_ROPE)
    scale = (MLA_NOPE + MLA_ROPE) ** -0.5 * LOG2E
    one = jnp.ones((S, MLA_NOPE), F32)
    zero = jnp.zeros((S, MLA_NOPE), F32)
    pad1 = jnp.ones((S, LANES - MLA_NOPE - MLA_ROPE), F32)
    pad0 = jnp.zeros((S, LANES - MLA_NOPE - MLA_ROPE), F32)
    cq = jnp.concatenate([one, c, c, pad1], axis=1) * scale
    sq = jnp.concatenate([zero, s, s, pad0], axis=1) * scale
    r32 = jnp.concatenate([c, c, s, s], axis=1)
    c, s = cs(MOBA_ROT)
    one = jnp.ones((S, MOBA_HD - MOBA_ROT), F32)
    zero = jnp.zeros((S, MOBA_HD - MOBA_ROT), F32)
    cm = jnp.concatenate([c, c, one] * 2, axis=1)
    sm = jnp.concatenate([s, s, zero] * 2, axis=1)
    return cq, sq, r32, cm, sm


def _block_diag(w):
    depth, n, c, d = w.shape
    rows = []
    for j in range(n):
        rows.append(jnp.concatenate([jnp.zeros((depth, c, j * d), w.dtype), w[:, j],
                                     jnp.zeros((depth, c, (n - 1 - j) * d), w.dtype)], axis=-1))
    return jnp.concatenate(rows, axis=1)


def _stacked_consts(attn_norm, mla_q_norm, mla_w_uq, mla_kv_norm, mla_w_ukv, gla_w_a2, gla_b_a2,
                    gla_head_norm, lru_conv_w, lru_conv_b, lru_w_a, lru_b_a, lru_w_x, lru_b_x, lru_lambda):
    depth = attn_norm.shape[0]
    r3 = lambda v: v.reshape(depth, 1, -1).astype(F32)
    wq, wkv, place = _layout_mla(mla_w_uq, mla_w_ukv)
    wa = jnp.pad(gla_w_a2, ((0, 0), (0, LANES - GLA_GATE_RANK), (0, 0))).astype(BF16)
    hn = jnp.tile(r3(gla_head_norm), (1, 1, GLA_HEADS))
    wax = jnp.concatenate([_block_diag(lru_w_a), _block_diag(lru_w_x)], axis=-1).astype(BF16)
    bax = jnp.concatenate([r3(lru_b_a), r3(lru_b_x)], axis=-1)
    return (r3(attn_norm), r3(mla_q_norm), wq, r3(mla_kv_norm), wkv, place,
            wa, r3(gla_b_a2), hn, lru_conv_w.reshape(depth, LRU_CONV, LRU_WIDTH), r3(lru_conv_b), wax, bax,
            r3(lru_lambda))


def kernel(x, attn_norm, w_in, mla_q_norm, mla_w_uq, mla_kv_norm, mla_w_ukv, gla_w_a2, gla_b_a2, gla_head_norm,
           lru_conv_w, lru_conv_b, lru_w_a, lru_b_a, lru_w_x, lru_b_x, lru_lambda, w_out, ffn_norm, w_ffn_in,
           w_ffn_out, final_norm):
    B, S, D = x.shape
    depth = w_in.shape[0]
    T = B * S
    assert D == D_MODEL and S % max(PREP_TM, ATTN_TQ, FFN_TM) == 0 and S // MOBA_BLOCK <= MOBA_SLOT
    tables = _tables(S)
    row = lambda v: v.reshape(1, -1).astype(F32)
    xt = x.reshape(T, D)
    wo, wi, w2 = w_out.astype(BF16), w_ffn_in.astype(BF16), w_ffn_out.astype(BF16)
    fg = ffn_norm.reshape(depth, 1, D).astype(F32)
    consts = _stacked_consts(attn_norm, mla_q_norm, mla_w_uq, mla_kv_norm, mla_w_ukv, gla_w_a2, gla_b_a2,
                             gla_head_norm, lru_conv_w, lru_conv_b, lru_w_a, lru_b_a, lru_w_x, lru_b_x, lru_lambda)
    for l in range(depth):
        mq, mk, mv, bq, bk, bv, y_b, y_d = _prep(xt, w_in, l, consts, tables, B, S, PREP_TM)
        y_a = _attn(mq, mk, mv, B, S, ATTN_TQ, ATTN_TK, "mla_attn")
        y_c = _attn(bq, bk, bv, B, S, ATTN_TQ, ATTN_TK, "moba_attn")
        xt = _ffn(xt, y_a, y_b, y_c, y_d, wo, fg, wi, w2, row(final_norm), l, l == depth - 1,
                  FFN_TM, FFN_HIDDEN // 2)
    return xt.reshape(B, S, D)
```

```python
import functools

import numpy as np
import jax
import jax.numpy as jnp
from jax import lax
from jax.experimental import pallas as pl
from jax.experimental.pallas import tpu as pltpu

F32 = jnp.float32
BF16 = jnp.bfloat16

D_MODEL = 1024
GROUP_WIDTH = 256
ROPE_THETA = 500000.0
NORM_EPS = 1e-6
MLA_HEADS = 4
MLA_V = 64
MLA_NOPE = 64
MLA_ROPE = 32
MLA_Q_RANK = 256
MLA_KV_RANK = 128
GLA_HEADS = 4
GLA_DV = 64
GLA_DK = 32
GLA_GATE_RANK = 16
GLA_TAU = 16.0
GLA_CHUNK = 64
MOBA_HEADS = 4
MOBA_HD = 64
MOBA_ROT = 16
MOBA_BLOCK = 256
MOBA_TOPK = 3
LRU_WIDTH = 256
LRU_BLOCKS = 4
LRU_BW = 64
LRU_CONV = 4
LRU_C = 8.0
FFN_HIDDEN = 2816

LANES = 128
SUBLANES = 8
VMEM_LIMIT = 56 * 1024 * 1024

W_MLA = 512
W_GLA = 896
W_MOBA = 1280
W_LRU = 512
W_IN = W_MLA + W_GLA + W_MOBA + W_LRU
NEG_BIG = -1e30
LOG2E = 1.4426950408889634
MOBA_SLOT = 32
PREP_TM = 512
FFN_TM = 1024
ATTN_HEADS = 4
ATTN_TQ = 4096
ATTN_TK = 256


def _rms(x, g):
    return x * lax.rsqrt(jnp.mean(x * x, axis=-1, keepdims=True) + NORM_EPS) * g


def _dot(a, b):
    return jnp.dot(a, b, preferred_element_type=F32)


def _dot_nt(a, b):
    return lax.dot_general(a, b, (((1,), (1,)), ((), ())), preferred_element_type=F32)


def _dot_tn(a, b):
    return lax.dot_general(a, b, (((0,), (0,)), ((), ())), preferred_element_type=F32)


def _split3(x):
    hi = x.astype(BF16)
    r = x - hi.astype(F32)
    mid = r.astype(BF16)
    lo = (r - mid.astype(F32)).astype(BF16)
    return hi, mid, lo


def _sigmoid(x):
    return 0.5 * jnp.tanh(0.5 * x) + 0.5


def _const_spec(shape):
    nd = len(shape)
    return pl.BlockSpec(shape, lambda *_: (0,) * nd, pipeline_mode=pl.Buffered(1))


def _params(sem):
    return pltpu.CompilerParams(dimension_semantics=sem, vmem_limit_bytes=VMEM_LIMIT)


def _mla_prep(p, qg_ref, wq_ref, kvg_ref, wkv_ref, place_ref, cq_ref, sq_ref, r32_ref, qt_ref, k_ref, vt_ref):
    hw = MLA_HEADS * LANES
    nq = _rms(p[:, :MLA_Q_RANK], qg_ref[...]).astype(BF16)
    q2 = _dot(nq, wq_ref[...])
    cq = cq_ref[...]
    sq = sq_ref[...]
    q = jnp.concatenate([q2[:, h * LANES:(h + 1) * LANES] * cq + q2[:, hw + h * LANES:hw + (h + 1) * LANES] * sq
                         for h in range(MLA_HEADS)], axis=1)
    qt_ref[...] = q.T.astype(BF16)
    nkv = _rms(p[:, MLA_Q_RANK:MLA_Q_RANK + MLA_KV_RANK], kvg_ref[...]).astype(BF16)
    kv = _dot(nkv, wkv_ref[...])
    c0 = MLA_Q_RANK + MLA_KV_RANK
    r32 = r32_ref[...]
    k_pe = p[:, c0:c0 + MLA_ROPE] * r32[:, :MLA_ROPE] + p[:, c0 + MLA_ROPE:c0 + 2 * MLA_ROPE] * r32[:, MLA_ROPE:]
    k_ref[...] = (kv[:, :hw] + _dot(k_pe.astype(BF16), place_ref[...])).astype(BF16)
    vt_ref[...] = kv[:, hw:].T.astype(BF16)


def _moba_prep(p, cm_ref, sm_ref, qt_ref, k_ref, vt_ref, km_ref, si, tm):
    W = GROUP_WIDTH
    HD = MOBA_HD
    nb = km_ref.shape[0]
    cm = jnp.concatenate([cm_ref[...]] * (W // LANES), axis=1)
    sm = jnp.concatenate([sm_ref[...]] * (W // LANES), axis=1)
    q = p[:, 0:W] * cm + p[:, 3 * W:4 * W] * sm
    k = p[:, W:2 * W] * cm + p[:, 4 * W:5 * W] * sm
    vt_ref[...] = p[:, 2 * W:3 * W].T.astype(BF16)
    nblk = tm // MOBA_BLOCK
    for j in range(nblk):
        mean = jnp.sum(k[j * MOBA_BLOCK:(j + 1) * MOBA_BLOCK, :], axis=0, keepdims=True) * (1.0 / MOBA_BLOCK)
        km_ref[pl.ds(si * nblk + j, 1), :] = mean
    km = km_ref[...]
    n_idx = lax.broadcasted_iota(jnp.int32, (nb, tm), 0)
    blk = (si * tm + lax.broadcasted_iota(jnp.int32, (nb, tm), 1)) // MOBA_BLOCK
    past = n_idx < blk
    own = n_idx == blk
    n_f = n_idx.astype(F32)
    lane_m = lax.broadcasted_iota(jnp.int32, (nb, W), 1)
    km_heads = jnp.concatenate([jnp.where((lane_m // HD) == h, km, 0.0) for h in range(MOBA_HEADS)], axis=0)
    m_hi, m_mid, m_lo = _split3(km_heads)
    q_hi, q_mid, q_lo = _split3(q)
    nm = MOBA_HEADS * nb
    g_hi = _dot_nt(jnp.concatenate([m_hi, m_mid, m_lo], axis=0), q_hi)
    g_mid = _dot_nt(jnp.concatenate([m_hi, m_mid], axis=0), q_mid)
    g_lo = _dot_nt(m_hi, q_lo)
    gates = (((g_lo + g_hi[2 * nm:]) + g_mid[nm:]) + (g_mid[:nm] + g_hi[nm:2 * nm])) + g_hi[:nm]
    masks = []
    for h in range(MOBA_HEADS):
        gate = jnp.where(past, gates[h * nb:(h + 1) * nb], -jnp.inf)
        keep = own
        for _ in range(MOBA_TOPK):
            mx = jnp.max(gate, axis=0, keepdims=True)
            first = jnp.min(jnp.where(gate == mx, n_f, float(LANES)), axis=0, keepdims=True)
            pick = (n_f == first) & (mx > -jnp.inf)
            keep = keep | pick
            gate = jnp.where(pick, -jnp.inf, gate)
        masks.append(jnp.where(keep, 0.0, NEG_BIG))
        if nb < MOBA_SLOT:
            masks.append(jnp.zeros((MOBA_SLOT - nb, tm), F32))
    mask_hi = jnp.concatenate(masks, axis=0).T
    mask_lo = pltpu.roll(mask_hi, HD, 1)
    lane = lax.broadcasted_iota(jnp.int32, (tm, LANES), 1)
    head_lanes = lane < HD
    q_scaled = q * (MOBA_HD ** -0.5 * LOG2E)
    own_blk = (si * tm + lax.broadcasted_iota(jnp.int32, (tm, LANES), 0)) // MOBA_BLOCK
    kone = [(lane == own_blk + HD + slot * MOBA_SLOT).astype(F32) for slot in range(2)]
    q_ext = []
    for h in range(MOBA_HEADS):
        pr = h // 2
        q_pair = q_scaled[:, pr * LANES:(pr + 1) * LANES]
        k_pair = k[:, pr * LANES:(pr + 1) * LANES]
        if h % 2:
            q_pair = pltpu.roll(q_pair, HD, 1)
            k_pair = pltpu.roll(k_pair, HD, 1)
        q_ext.append(jnp.where(head_lanes, q_pair, mask_lo if h < 2 else mask_hi))
        k_ref[:, h * LANES:(h + 1) * LANES] = jnp.where(
            head_lanes, k_pair, kone[h % 2]).astype(BF16)
    qt_ref[...] = jnp.concatenate(q_ext, axis=1).T.astype(BF16)


def _gla(p, wa_ref, ba_ref, hn_ref, o_ref, state_ref, tm):
    C = GLA_CHUNK
    hk = GLA_HEADS * GLA_DK
    hv = GLA_HEADS * GLA_DV
    nc = tm // C
    rc = lax.broadcasted_iota(jnp.int32, (tm, hk), 0) % C
    lane_k = lax.broadcasted_iota(jnp.int32, (C, hk), 1)
    lane_v = lax.broadcasted_iota(jnp.int32, (C, hv), 1)
    row4 = lax.broadcasted_iota(jnp.int32, (GLA_HEADS * C, C), 0)
    col4 = lax.broadcasted_iota(jnp.int32, (GLA_HEADS * C, C), 1)
    causal4 = (row4 % C) >= col4
    sr = lax.broadcasted_iota(jnp.int32, (hv, hk), 0)
    sc = lax.broadcasted_iota(jnp.int32, (hv, hk), 1)
    blockdiag = (sr // GLA_DV) == (sc // GLA_DK)
    scale = GLA_DK ** -0.5

    q = p[:, 0:hk]
    k = p[:, hk:2 * hk]
    vb = p[:, 2 * hk:2 * hk + hv].astype(BF16)
    a_low = p[:, 2 * hk + 2 * hv:2 * hk + 2 * hv + LANES]
    a_lin = _dot(a_low.astype(BF16), wa_ref[...]) + ba_ref[...]
    b = (jnp.minimum(a_lin, 0.0) - jnp.log(1.0 + jnp.exp(-jnp.abs(a_lin)))) * (1.0 / GLA_TAU)
    for s in (1, 2, 4, 8, 16, 32):
        b = b + jnp.where(rc >= s, pltpu.roll(b, s, 0), 0.0)
    last = [b[c * C + C - 1:(c + 1) * C, :] for c in range(nc)]
    bl = jnp.concatenate([jnp.broadcast_to(r, (C, hk)) for r in last], axis=0)
    bref = 0.5 * bl
    qf = q * scale
    qs = qf * jnp.exp(b - bref)
    ks = (k * jnp.exp(bref - b)).astype(BF16)
    kd = (k * jnp.exp(bl - b)).astype(BF16)
    qe = (qf * jnp.exp(b)).astype(BF16)

    intra, incr = [], []
    for c in range(nc):
        sl = slice(c * C, (c + 1) * C)
        qs4 = jnp.concatenate(
            [jnp.where((lane_k // GLA_DK) == h, qs[sl], 0.0) for h in range(GLA_HEADS)], axis=0).astype(BF16)
        att = jnp.where(causal4, _dot_nt(qs4, ks[sl]), 0.0)
        res = _dot(att.astype(BF16), vb[sl])
        o = jnp.where((lane_v // GLA_DV) == 0, res[0:C, :], 0.0)
        for h in range(1, GLA_HEADS):
            o = o + jnp.where((lane_v // GLA_DV) == h, res[h * C:(h + 1) * C, :], 0.0)
        intra.append(o)
        incr.append(jnp.where(blockdiag, _dot_tn(vb[sl], kd[sl]), 0.0))

    st = state_ref[...]
    outs = []
    for c in range(nc):
        outs.append(intra[c] + _dot_nt(qe[c * C:(c + 1) * C], st.astype(BF16)))
        st = st * jnp.exp(last[c]) + incr[c]
    state_ref[...] = st
    o = jnp.concatenate(outs, axis=0)

    lane_t = lax.broadcasted_iota(jnp.int32, (tm, hv), 1)
    osq = o * o
    inv = jnp.zeros_like(o)
    for h in range(GLA_HEADS):
        mh = (lane_t // GLA_DV) == h
        ms = jnp.sum(jnp.where(mh, osq, 0.0), axis=-1, keepdims=True) * (1.0 / GLA_DV)
        inv = jnp.where(mh, lax.rsqrt(ms + NORM_EPS), inv)
    g = p[:, 2 * hk + hv:2 * hk + 2 * hv]
    o_ref[...] = (o * inv * hn_ref[...] * (g * _sigmoid(g))).astype(o_ref.dtype)


def _lru(p, cw_ref, cb_ref, wax_ref, bax_ref, lam_ref, o_ref, xs_ref, h_ref, tm):
    W = LRU_WIDTH
    G = SUBLANES
    x = p[:, 0:W]
    gate = p[:, W:2 * W]
    xs_ref[G:, :] = x
    cw = cw_ref[...]
    xc = cb_ref[...] + cw[LRU_CONV - 1:LRU_CONV, :] * x
    for d in range(1, LRU_CONV):
        xc = xc + cw[LRU_CONV - 1 - d:LRU_CONV - d, :] * xs_ref[G - d:G - d + tm, :]
    xs_ref[:G, :] = x[tm - G:, :]
    ri = _dot(xc.astype(BF16), wax_ref[...]) + bax_ref[...]
    r = _sigmoid(ri[:, :W])
    ig = _sigmoid(ri[:, W:])
    lam = lam_ref[...]
    softplus_neg = jnp.maximum(-lam, 0.0) + jnp.log1p(jnp.exp(-jnp.abs(lam)))
    log_a = (-LRU_C * r) * softplus_neg
    a = jnp.exp(log_a)
    u = jnp.sqrt(-jnp.tanh(log_a) * (a * a + 1.0)) * (ig * xc)
    sub = lax.broadcasted_iota(jnp.int32, (G, W), 0)
    h = h_ref[...]
    outs = []
    for g in range(tm // G):
        ag = a[g * G:(g + 1) * G, :]
        ug = u[g * G:(g + 1) * G, :]
        for s in (1, 2, 4):
            live = sub >= s
            ug = ug + ag * jnp.where(live, pltpu.roll(ug, s, 0), 0.0)
            ag = ag * jnp.where(live, pltpu.roll(ag, s, 0), 1.0)
        hg = ug + ag * h
        outs.append(hg)
        h = hg[G - 1:G, :]
    h_ref[...] = h
    hs = jnp.concatenate(outs, axis=0)
    gelu = 0.5 * gate * (1.0 + jnp.tanh(0.7978845608028654 * (gate + 0.044715 * gate * gate * gate)))
    o_ref[...] = (hs * gelu).astype(o_ref.dtype)


def _w_in_plan():
    sp = np.cumsum([0, 256, 128, 32, 128, 128, 256, 16, 256, 256, 256, 256, 256, 256])
    mq, mkv, mkr, gq, gk, gv, ga, gg, bq, bk, bv, rx, rg = [(int(sp[n]), int(sp[n + 1] - sp[n])) for n in range(13)]
    keep = lambda c: [(c[1], c[0], 1.0)]
    zeros = lambda n: [(n, None, 0.0)]

    def swapped(start, half):
        return [(half, start + half, -1.0), (half, start, 1.0)]

    def moba_swapped(c):
        out = []
        for h in range(MOBA_HEADS):
            out += swapped(c[0] + h * MOBA_HD, MOBA_ROT // 2) + zeros(MOBA_HD - MOBA_ROT)
        return out

    plan = (keep(mq) + keep(mkv) + keep(mkr) + swapped(mkr[0], MLA_ROPE // 2) + zeros(W_MLA - 256 - 128 - 64)
            + keep(gq) + keep(gk) + keep(gv) + keep(gg) + keep(ga) + zeros(LANES - GLA_GATE_RANK)
            + keep(bq) + keep(bk) + keep(bv) + moba_swapped(bq) + moba_swapped(bk)
            + keep(rx) + keep(rg))
    assert sum(p[0] for p in plan) == W_IN
    return plan, int(sp[-1])


def _build_w_in(w_ref, o_ref, rows_per_step=256):
    plan, n_src = _w_in_plan()
    blocks, cur, room = [], [], LANES
    for width, src, sign in plan:
        while width:
            take = min(width, room)
            cur.append((take, src, sign))
            src = None if src is None else src + take
            width -= take
            room -= take
            if room == 0:
                blocks.append(cur)
                cur, room = [], LANES
    for r0 in range(0, w_ref.shape[0], rows_per_step):
        rows = slice(r0, r0 + rows_per_step)
        for j, frags in enumerate(blocks):
            vals = []
            for width, src, sign in frags:
                if src is None:
                    vals.append(jnp.zeros((rows_per_step, width), F32))
                    continue
                a0 = src // LANES * LANES
                a1 = min(-(-(src + width) // LANES) * LANES, n_src)
                win = w_ref[rows, a0:a1]
                v = win[:, src - a0:src - a0 + width]
                vals.append(-v if sign < 0 else v)
            blk = vals[0] if len(vals) == 1 else jnp.concatenate(vals, axis=1)
            o_ref[rows, j * LANES:(j + 1) * LANES] = blk.astype(BF16)


def _prep_kernel(x_ref, g_ref, w_raw_ref,
                 qg_ref, wq_ref, kvg_ref, wkv_ref, place_ref, cq_ref, sq_ref, r32_ref,
                 cm_ref, sm_ref,
                 wa_ref, ba_ref, hn_ref,
                 cw_ref, cb_ref, wax_ref, bax_ref, lam_ref,
                 mq_ref, mk_ref, mv_ref, bq_ref, bk_ref, bv_ref, yb_ref, yd_ref,
                 w_ref, km_ref, state_ref, xs_ref, h_ref, *, tm):
    si = pl.program_id(1)

    @pl.when((pl.program_id(0) == 0) & (si == 0))
    def _():
        _build_w_in(w_raw_ref, w_ref)

    @pl.when(si == 0)
    def _():
        km_ref[...] = jnp.zeros_like(km_ref)
        state_ref[...] = jnp.zeros_like(state_ref)
        xs_ref[:SUBLANES, :] = jnp.zeros((SUBLANES, LRU_WIDTH), F32)
        h_ref[...] = jnp.zeros_like(h_ref)

    h = _rms(x_ref[...], g_ref[...]).astype(BF16)
    o_mla, o_gla, o_moba, o_lru = 0, W_MLA, W_MLA + W_GLA, W_MLA + W_GLA + W_MOBA
    p_mla = _dot(h, w_ref[:, o_mla:o_mla + W_MLA])
    p_moba = _dot(h, w_ref[:, o_moba:o_moba + W_MOBA])
    _mla_prep(p_mla, qg_ref, wq_ref, kvg_ref, wkv_ref, place_ref, cq_ref, sq_ref, r32_ref, mq_ref, mk_ref, mv_ref)
    p_lru = _dot(h, w_ref[:, o_lru:o_lru + W_LRU])
    _moba_prep(p_moba, cm_ref, sm_ref, bq_ref, bk_ref, bv_ref, km_ref, si, tm)
    p_gla = _dot(h, w_ref[:, o_gla:o_gla + W_GLA])
    _lru(p_lru, cw_ref, cb_ref, wax_ref, bax_ref, lam_ref, yd_ref, xs_ref, h_ref, tm)
    _gla(p_gla, wa_ref, ba_ref, hn_ref, yb_ref, state_ref, tm)


def _prep(x, w_in, l, consts, tables, B, S, tm):
    T = x.shape[0]
    ns = S // tm
    hw = ATTN_HEADS * LANES
    tok = lambda b, s: (b * ns + s, 0)
    pos = lambda b, s: (s, 0)
    g, qg, wq, kvg, wkv, place, wa, ba, hn, cw, cb, wax, bax, lam = consts
    cq, sq, r32, cm, sm = tables
    w_spec = pl.BlockSpec((None,) + w_in.shape[1:], lambda b, s: (l, 0, 0))
    c = lambda a: _layer_spec(a, l) if a.ndim == 3 else _const_spec(a.shape)
    t = lambda a: pl.BlockSpec((tm, a.shape[1]), pos)
    nb = max(2 * SUBLANES, -(-(S // MOBA_BLOCK) // SUBLANES) * SUBLANES)
    tok_t = lambda b, s: (0, b * ns + s)
    outs = [(hw, True), (hw, False), (GROUP_WIDTH, True), (hw, True), (hw, False), (GROUP_WIDTH, True),
            (GROUP_WIDTH, False), (GROUP_WIDTH, False)]
    return pl.pallas_call(
        functools.partial(_prep_kernel, tm=tm),
        grid=(B, ns),
        in_specs=[pl.BlockSpec((tm, D_MODEL), tok), c(g), w_spec,
                  c(qg), c(wq), c(kvg), c(wkv), c(place), t(cq), t(sq), t(r32),
                  t(cm), t(sm),
                  c(wa), c(ba), c(hn),
                  c(cw), c(cb), c(wax), c(bax), c(lam)],
        out_specs=[pl.BlockSpec((wd, tm), tok_t) if tr else pl.BlockSpec((tm, wd), tok) for wd, tr in outs],
        out_shape=[jax.ShapeDtypeStruct((wd, T) if tr else (T, wd), BF16) for wd, tr in outs],
        scratch_shapes=[pltpu.VMEM((D_MODEL, W_IN), BF16),
                        pltpu.VMEM((nb, GROUP_WIDTH), F32),
                        pltpu.VMEM((GLA_HEADS * GLA_DV, GLA_HEADS * GLA_DK), F32),
                        pltpu.VMEM((SUBLANES + tm, LRU_WIDTH), F32), pltpu.VMEM((1, LRU_WIDTH), F32)],
        compiler_params=_params(("arbitrary", "arbitrary")),
        name="mixer_prep",
    )(x, g, w_in, qg, wq, kvg, wkv, place, cq, sq, r32, cm, sm, wa, ba, hn, cw, cb, wax, bax, lam)


HEAD_V = 64
ACC_ROWS = HEAD_V + 16


def _attn_kernel(qt_ref, k_ref, vt_ref, o_ref, m_ref, acc_ref, *, tq, tk):
    i = pl.program_id(1)
    m_ref[...] = jnp.full(m_ref.shape, NEG_BIG, F32)
    acc_ref[...] = jnp.zeros(acc_ref.shape, F32)
    ones = jnp.ones((ACC_ROWS - HEAD_V, tk), BF16)

    def tile(j, diagonal, q0=0):
        keys = pl.ds(pl.multiple_of(j * tk, tk), tk)
        nq = tq - q0
        if diagonal:
            kpos = j * tk + lax.broadcasted_iota(jnp.int32, (tk, nq), 0)
            qpos = i * tq + q0 + lax.broadcasted_iota(jnp.int32, (tk, nq), 1)
            allowed = kpos <= qpos
        scores = [_dot(k_ref[keys, h * LANES:(h + 1) * LANES], qt_ref[h * LANES:(h + 1) * LANES, q0:])
                  for h in range(ATTN_HEADS)]
        for h in range(ATTN_HEADS):
            vt = jnp.concatenate([vt_ref[h * HEAD_V:(h + 1) * HEAD_V, keys], ones], axis=0)
            s = scores[h]
            if diagonal:
                s = jnp.where(allowed, s, NEG_BIG)
            m_prev = m_ref[h, :, q0:]
            m_next = jnp.maximum(m_prev, jnp.max(s, axis=0, keepdims=True))
            alpha = jnp.exp2(m_prev - m_next)
            p = jnp.exp2(s - jnp.tile(m_next, (tk // SUBLANES, 1)))
            acc_ref[h, :, q0:] = (jnp.tile(alpha, (ACC_ROWS // SUBLANES, 1)) * acc_ref[h, :, q0:]
                                  + _dot(vt, p.astype(BF16)))
            m_ref[h, :, q0:] = m_next

    n_full = (i * tq) // tk

    def body(j, _):
        tile(j, False)
        return 0

    lax.fori_loop(0, n_full, body, 0)
    for d in range(tq // tk):
        tile(n_full + d, True, d * tk)
    out_t = jnp.concatenate(
        [acc_ref[h, :HEAD_V] * jnp.tile(1.0 / acc_ref[h, HEAD_V:HEAD_V + SUBLANES], (HEAD_V // SUBLANES, 1))
         for h in range(ATTN_HEADS)], axis=0)
    o_ref[...] = out_t.T.astype(o_ref.dtype)


def _attn(qt, k, vt, B, S, tq, tk, name):
    T = k.shape[0]
    nq = S // tq
    hw = ATTN_HEADS * LANES
    stat = pltpu.VMEM((ATTN_HEADS, SUBLANES, tq), F32)
    return pl.pallas_call(
        functools.partial(_attn_kernel, tq=tq, tk=tk),
        grid=(B, nq),
        in_specs=[pl.BlockSpec((hw, tq), lambda b, i: (0, b * nq + i)),
                  pl.BlockSpec((S, hw), lambda b, i: (b, 0)),
                  pl.BlockSpec((GROUP_WIDTH, S), lambda b, i: (0, b))],
        out_specs=pl.BlockSpec((tq, GROUP_WIDTH), lambda b, i: (b * nq + i, 0)),
        out_shape=jax.ShapeDtypeStruct((T, GROUP_WIDTH), BF16),
        scratch_shapes=[stat, pltpu.VMEM((ATTN_HEADS, ACC_ROWS, tq), F32)],
        compiler_params=_params(("parallel", "arbitrary")),
        name=name,
    )(qt, k, vt)


def _ffn_kernel(x_ref, ya_ref, yb_ref, yc_ref, yd_ref, wo_ref, fg_ref, wi_ref, w2_ref, ng_ref, o_ref,
                *, final, chunk):
    W = GROUP_WIDTH
    x1 = x_ref[...]
    for n, y_ref in enumerate((ya_ref, yb_ref, yc_ref, yd_ref)):
        x1 = x1 + _dot(y_ref[...], wo_ref[n * W:(n + 1) * W, :])
    h = _rms(x1, fg_ref[...]).astype(BF16)
    parts = []
    for c in range(FFN_HIDDEN // chunk):
        g = _dot(h, wi_ref[:, c * chunk:(c + 1) * chunk])
        up = _dot(h, wi_ref[:, FFN_HIDDEN + c * chunk:FFN_HIDDEN + (c + 1) * chunk])
        act = (g * _sigmoid(g) * up).astype(BF16)
        parts.append(_dot(act, w2_ref[c * chunk:(c + 1) * chunk, :]))
    acc = x1 + sum(parts[1:], parts[0])
    if final:
        acc = _rms(acc, ng_ref[...])
    o_ref[...] = acc


def _layer_spec(stacked, l):
    nd = stacked.ndim - 1
    return pl.BlockSpec((None,) + stacked.shape[1:], lambda *_: (l,) + (0,) * nd, pipeline_mode=pl.Buffered(1))


def _ffn(x, ya, yb, yc, yd, wo, fg, wi, w2, ng, l, final, tm, chunk):
    T = x.shape[0]
    tok = lambda i: (i, 0)
    ysp = pl.BlockSpec((tm, GROUP_WIDTH), tok)
    return pl.pallas_call(
        functools.partial(_ffn_kernel, final=final, chunk=chunk),
        grid=(T // tm,),
        in_specs=[pl.BlockSpec((tm, D_MODEL), tok), ysp, ysp, ysp, ysp,
                  _layer_spec(wo, l), _layer_spec(fg, l), _layer_spec(wi, l),
                  _layer_spec(w2, l), _const_spec(ng.shape)],
        out_specs=pl.BlockSpec((tm, D_MODEL), tok),
        out_shape=jax.ShapeDtypeStruct((T, D_MODEL), F32),
        compiler_params=_params(("parallel",)),
        name="outproj_ffn",
    )(x, ya, yb, yc, yd, wo, fg, wi, w2, ng)


def _swap_cols(w, half):
    return jnp.concatenate([-w[..., half:2 * half], w[..., :half]], axis=-1)


def _layout_mla(w_uq, w_ukv):
    zq = jnp.zeros(w_uq.shape[:-1] + (LANES - MLA_NOPE - MLA_ROPE,), w_uq.dtype)
    zr = jnp.zeros(w_uq.shape[:-1] + (MLA_NOPE,), w_uq.dtype)
    qd = MLA_NOPE + MLA_ROPE
    plain, swapped = [], []
    for h in range(MLA_HEADS):
        nope = w_uq[..., h * qd:h * qd + MLA_NOPE]
        rope = w_uq[..., h * qd + MLA_NOPE:(h + 1) * qd]
        plain += [nope, rope, zq]
        swapped += [zr, _swap_cols(rope, MLA_ROPE // 2), zq]
    wq = jnp.concatenate(plain + swapped, axis=-1).astype(BF16)
    zk = jnp.zeros(w_ukv.shape[:-1] + (LANES - MLA_NOPE,), w_ukv.dtype)
    kd = MLA_NOPE + MLA_V
    kparts = []
    vparts = []
    for h in range(MLA_HEADS):
        kparts += [w_ukv[..., h * kd:h * kd + MLA_NOPE], zk]
        vparts += [w_ukv[..., h * kd + MLA_NOPE:(h + 1) * kd]]
    wkv = jnp.concatenate(kparts + vparts, axis=-1).astype(BF16)
    place = np.zeros((MLA_ROPE, MLA_HEADS * LANES), np.float32)
    for h in range(MLA_HEADS):
        place[np.arange(MLA_ROPE), h * LANES + MLA_NOPE + np.arange(MLA_ROPE)] = 1.0
    return wq, wkv, jnp.asarray(place, BF16)


def _tables(S):
    def cs(dim):
        inv_freq = ROPE_THETA ** (-jnp.arange(0, dim, 2, dtype=F32) / dim)
        ang = jnp.arange(S, dtype=F32)[:, None] * inv_freq[None, :]
        return jnp.cos(ang), jnp.sin(ang)

    c, s = cs(MLA_ROPE)
    scale = (MLA_NOPE + MLA_ROPE) ** -0.5 * LOG2E
    one = jnp.ones((S, MLA_NOPE), F32)
    zero = jnp.zeros((S, MLA_NOPE), F32)
    pad1 = jnp.ones((S, LANES - MLA_NOPE - MLA_ROPE), F32)
    pad0 = jnp.zeros((S, LANES - MLA_NOPE - MLA_ROPE), F32)
    cq = jnp.concatenate([one, c, c, pad1], axis=1) * scale
    sq = jnp.concatenate([zero, s, s, pad0], axis=1) * scale
    r32 = jnp.concatenate([c, c, s, s], axis=1)
    c, s = cs(MOBA_ROT)
    one = jnp.ones((S, MOBA_HD - MOBA_ROT), F32)
    zero = jnp.zeros((S, MOBA_HD - MOBA_ROT), F32)
    cm = jnp.concatenate([c, c, one] * 2, axis=1)
    sm = jnp.concatenate([s, s, zero] * 2, axis=1)
    return cq, sq, r32, cm, sm


def _block_diag(w):
    depth, n, c, d = w.shape
    rows = []
    for j in range(n):
        rows.append(jnp.concatenate([jnp.zeros((depth, c, j * d), w.dtype), w[:, j],
                                     jnp.zeros((depth, c, (n - 1 - j) * d), w.dtype)], axis=-1))
    return jnp.concatenate(rows, axis=1)


def _stacked_consts(attn_norm, mla_q_norm, mla_w_uq, mla_kv_norm, mla_w_ukv, gla_w_a2, gla_b_a2,
                    gla_head_norm, lru_conv_w, lru_conv_b, lru_w_a, lru_b_a, lru_w_x, lru_b_x, lru_lambda):
    depth = attn_norm.shape[0]
    r3 = lambda v: v.reshape(depth, 1, -1).astype(F32)
    wq, wkv, place = _layout_mla(mla_w_uq, mla_w_ukv)
    wa = jnp.pad(gla_w_a2, ((0, 0), (0, LANES - GLA_GATE_RANK), (0, 0))).astype(BF16)
    hn = jnp.tile(r3(gla_head_norm), (1, 1, GLA_HEADS))
    wax = jnp.concatenate([_block_diag(lru_w_a), _block_diag(lru_w_x)], axis=-1).astype(BF16)
    bax = jnp.concatenate([r3(lru_b_a), r3(lru_b_x)], axis=-1)
    return (r3(attn_norm), r3(mla_q_norm), wq, r3(mla_kv_norm), wkv, place,
            wa, r3(gla_b_a2), hn, lru_conv_w.reshape(depth, LRU_CONV, LRU_WIDTH), r3(lru_conv_b), wax, bax,
            r3(lru_lambda))


def kernel(x, attn_norm, w_in, mla_q_norm, mla_w_uq, mla_kv_norm, mla_w_ukv, gla_w_a2, gla_b_a2, gla_head_norm,
           lru_conv_w, lru_conv_b, lru_w_a, lru_b_a, lru_w_x, lru_b_x, lru_lambda, w_out, ffn_norm, w_ffn_in,
           w_ffn_out, final_norm):
    B, S, D = x.shape
    depth = w_in.shape[0]
    T = B * S
    tq = min(S, ATTN_TQ)
    assert D == D_MODEL and S % max(PREP_TM, tq, FFN_TM) == 0 and S // MOBA_BLOCK <= MOBA_SLOT
    tables = _tables(S)
    row = lambda v: v.reshape(1, -1).astype(F32)
    xt = x.reshape(T, D)
    wo, wi, w2 = w_out.astype(BF16), w_ffn_in.astype(BF16), w_ffn_out.astype(BF16)
    fg = ffn_norm.reshape(depth, 1, D).astype(F32)
    consts = _stacked_consts(attn_norm, mla_q_norm, mla_w_uq, mla_kv_norm, mla_w_ukv, gla_w_a2, gla_b_a2,
                             gla_head_norm, lru_conv_w, lru_conv_b, lru_w_a, lru_b_a, lru_w_x, lru_b_x, lru_lambda)
    for l in range(depth):
        mq, mk, mv, bq, bk, bv, y_b, y_d = _prep(xt, w_in, l, consts, tables, B, S, PREP_TM)
        y_a = _attn(mq, mk, mv, B, S, tq, ATTN_TK, "mla_attn")
        y_c = _attn(bq, bk, bv, B, S, tq, ATTN_TK, "moba_attn")
        xt = _ffn(xt, y_a, y_b, y_c, y_d, wo, fg, wi, w2, row(final_norm), l, l == depth - 1,
                  FFN_TM, FFN_HIDDEN // 2)
    return xt.reshape(B, S, D)
```

```python
import functools

import numpy as np
import jax
import jax.numpy as jnp
from jax import lax
from jax.experimental import pallas as pl
from jax.experimental.pallas import tpu as pltpu

F32 = jnp.float32
BF16 = jnp.bfloat16

D_MODEL = 1024
GROUP_WIDTH = 256
ROPE_THETA = 500000.0
NORM_EPS = 1e-6
MLA_HEADS = 4
MLA_V = 64
MLA_NOPE = 64
MLA_ROPE = 32
MLA_Q_RANK = 256
MLA_KV_RANK = 128
GLA_HEADS = 4
GLA_DV = 64
GLA_DK = 32
GLA_GATE_RANK = 16
GLA_TAU = 16.0
GLA_CHUNK = 64
MOBA_HEADS = 4
MOBA_HD = 64
MOBA_ROT = 16
MOBA_BLOCK = 256
MOBA_TOPK = 3
LRU_WIDTH = 256
LRU_BLOCKS = 4
LRU_BW = 64
LRU_CONV = 4
LRU_C = 8.0
FFN_HIDDEN = 2816

LANES = 128
SUBLANES = 8
VMEM_LIMIT = 56 * 1024 * 1024

W_MLA = 512
W_GLA = 896
W_MOBA = 768
W_LRU = 512
W_IN = W_MLA + W_GLA + W_MOBA + W_LRU
NEG_BIG = -1e30
LOG2E = 1.4426950408889634
MOBA_SLOT = 32
PREP_TM = 512
FFN_TM = 1024
ATTN_HEADS = 4
ATTN_TQ = 4096
ATTN_TK = 256


def _rms(x, g):
    return x * lax.rsqrt(jnp.mean(x * x, axis=-1, keepdims=True) + NORM_EPS) * g


def _dot(a, b):
    return jnp.dot(a, b, preferred_element_type=F32)


def _dot_nt(a, b):
    return lax.dot_general(a, b, (((1,), (1,)), ((), ())), preferred_element_type=F32)


def _dot_tn(a, b):
    return lax.dot_general(a, b, (((0,), (0,)), ((), ())), preferred_element_type=F32)


def _split3(x):
    hi = x.astype(BF16)
    r = x - hi.astype(F32)
    mid = r.astype(BF16)
    lo = (r - mid.astype(F32)).astype(BF16)
    return hi, mid, lo


def _sigmoid(x):
    return 0.5 * jnp.tanh(0.5 * x) + 0.5


def _const_spec(shape):
    nd = len(shape)
    return pl.BlockSpec(shape, lambda *_: (0,) * nd, pipeline_mode=pl.Buffered(1))


def _params(sem):
    return pltpu.CompilerParams(dimension_semantics=sem, vmem_limit_bytes=VMEM_LIMIT)


def _mla_prep(p, qg_ref, wq_ref, kvg_ref, wkv_ref, place_ref, cq_ref, sq_ref, r32_ref, qt_ref, k_ref, vt_ref):
    hw = MLA_HEADS * LANES
    nq = _rms(p[:, :MLA_Q_RANK], qg_ref[...]).astype(BF16)
    q2 = _dot(nq, wq_ref[...])
    cq = cq_ref[...].T
    sq = sq_ref[...].T
    q = jnp.concatenate([q2[:, h * LANES:(h + 1) * LANES] * cq + q2[:, hw + h * LANES:hw + (h + 1) * LANES] * sq
                         for h in range(MLA_HEADS)], axis=1)
    qt_ref[...] = q.T.astype(BF16)
    nkv = _rms(p[:, MLA_Q_RANK:MLA_Q_RANK + MLA_KV_RANK], kvg_ref[...]).astype(BF16)
    kv = _dot(nkv, wkv_ref[...])
    c0 = MLA_Q_RANK + MLA_KV_RANK
    r32 = r32_ref[...].T
    k_pe = (p[:, c0:c0 + MLA_ROPE] * r32[:, :MLA_ROPE]
            + p[:, c0 + MLA_ROPE:c0 + 2 * MLA_ROPE] * r32[:, MLA_ROPE:2 * MLA_ROPE])
    k_ref[...] = (kv[:, :hw] + _dot(k_pe.astype(BF16), place_ref[...])).astype(BF16)
    vt_ref[...] = kv[:, hw:].T.astype(BF16)


def _moba_prep(p, cm_ref, sm_ref, qt_ref, k_ref, vt_ref, km_ref, si, tm):
    W = GROUP_WIDTH
    HD = MOBA_HD
    nb = km_ref.shape[0]
    cm = cm_ref[...].T
    sm = sm_ref[...].T
    half = MOBA_ROT // 2
    first_half = (lax.broadcasted_iota(jnp.int32, (tm, LANES), 1) % HD) < half

    def rope(x):
        out = []
        for pr in range(W // LANES):
            xp = x[:, pr * LANES:(pr + 1) * LANES]
            partner = jnp.where(first_half, pltpu.roll(xp, LANES - half, 1), pltpu.roll(xp, half, 1))
            out.append(xp * cm + partner * sm)
        return jnp.concatenate(out, axis=1)

    q = rope(p[:, 0:W])
    k = rope(p[:, W:2 * W])
    vt_ref[...] = p[:, 2 * W:3 * W].T.astype(BF16)
    nblk = tm // MOBA_BLOCK
    for j in range(nblk):
        mean = jnp.sum(k[j * MOBA_BLOCK:(j + 1) * MOBA_BLOCK, :], axis=0, keepdims=True) * (1.0 / MOBA_BLOCK)
        km_ref[pl.ds(si * nblk + j, 1), :] = mean
    km = km_ref[...]
    n_idx = lax.broadcasted_iota(jnp.int32, (nb, tm), 0)
    blk = (si * tm + lax.broadcasted_iota(jnp.int32, (nb, tm), 1)) // MOBA_BLOCK
    past = n_idx < blk
    own = n_idx == blk
    n_f = n_idx.astype(F32)
    lane_m = lax.broadcasted_iota(jnp.int32, (nb, W), 1)
    km_heads = jnp.concatenate([jnp.where((lane_m // HD) == h, km, 0.0) for h in range(MOBA_HEADS)], axis=0)
    m_hi, m_mid, m_lo = _split3(km_heads)
    q_hi, q_mid, q_lo = _split3(q)
    nm = MOBA_HEADS * nb
    g_hi = _dot_nt(jnp.concatenate([m_hi, m_mid, m_lo], axis=0), q_hi)
    g_mid = _dot_nt(jnp.concatenate([m_hi, m_mid], axis=0), q_mid)
    g_lo = _dot_nt(m_hi, q_lo)
    gates = (((g_lo + g_hi[2 * nm:]) + g_mid[nm:]) + (g_mid[:nm] + g_hi[nm:2 * nm])) + g_hi[:nm]
    masks = []
    for h in range(MOBA_HEADS):
        gate = jnp.where(past, gates[h * nb:(h + 1) * nb], -jnp.inf)
        keep = own
        for _ in range(MOBA_TOPK):
            mx = jnp.max(gate, axis=0, keepdims=True)
            first = jnp.min(jnp.where(gate == mx, n_f, float(LANES)), axis=0, keepdims=True)
            pick = (n_f == first) & (mx > -jnp.inf)
            keep = keep | pick
            gate = jnp.where(pick, -jnp.inf, gate)
        masks.append(jnp.where(keep, 0.0, NEG_BIG))
        if nb < MOBA_SLOT:
            masks.append(jnp.zeros((MOBA_SLOT - nb, tm), F32))
    mask_hi = jnp.concatenate(masks, axis=0).T
    mask_lo = pltpu.roll(mask_hi, HD, 1)
    lane = lax.broadcasted_iota(jnp.int32, (tm, LANES), 1)
    head_lanes = lane < HD
    q_scaled = q * (MOBA_HD ** -0.5 * LOG2E)
    own_blk = (si * tm + lax.broadcasted_iota(jnp.int32, (tm, LANES), 0)) // MOBA_BLOCK
    kone = [(lane == own_blk + HD + slot * MOBA_SLOT).astype(F32) for slot in range(2)]
    q_ext = []
    for h in range(MOBA_HEADS):
        pr = h // 2
        q_pair = q_scaled[:, pr * LANES:(pr + 1) * LANES]
        k_pair = k[:, pr * LANES:(pr + 1) * LANES]
        if h % 2:
            q_pair = pltpu.roll(q_pair, HD, 1)
            k_pair = pltpu.roll(k_pair, HD, 1)
        q_ext.append(jnp.where(head_lanes, q_pair, mask_lo if h < 2 else mask_hi))
        k_ref[:, h * LANES:(h + 1) * LANES] = jnp.where(
            head_lanes, k_pair, kone[h % 2]).astype(BF16)
    qt_ref[...] = jnp.concatenate(q_ext, axis=1).T.astype(BF16)


def _gla(p, wa_ref, ba_ref, hn_ref, o_ref, state_ref, tm):
    C = GLA_CHUNK
    hk = GLA_HEADS * GLA_DK
    hv = GLA_HEADS * GLA_DV
    nc = tm // C
    rc = lax.broadcasted_iota(jnp.int32, (tm, hk), 0) % C
    lane_k = lax.broadcasted_iota(jnp.int32, (C, hk), 1)
    lane_v = lax.broadcasted_iota(jnp.int32, (C, hv), 1)
    row4 = lax.broadcasted_iota(jnp.int32, (GLA_HEADS * C, C), 0)
    col4 = lax.broadcasted_iota(jnp.int32, (GLA_HEADS * C, C), 1)
    causal4 = (row4 % C) >= col4
    sr = lax.broadcasted_iota(jnp.int32, (hv, hk), 0)
    sc = lax.broadcasted_iota(jnp.int32, (hv, hk), 1)
    blockdiag = (sr // GLA_DV) == (sc // GLA_DK)
    scale = GLA_DK ** -0.5

    q = p[:, 0:hk]
    k = p[:, hk:2 * hk]
    vb = p[:, 2 * hk:2 * hk + hv].astype(BF16)
    a_low = p[:, 2 * hk + 2 * hv:2 * hk + 2 * hv + LANES]
    a_lin = _dot(a_low.astype(BF16), wa_ref[...]) + ba_ref[...]
    b = (jnp.minimum(a_lin, 0.0) - jnp.log(1.0 + jnp.exp(-jnp.abs(a_lin)))) * (1.0 / GLA_TAU)
    for s in (1, 2, 4, 8, 16, 32):
        b = b + jnp.where(rc >= s, pltpu.roll(b, s, 0), 0.0)
    last = [b[c * C + C - 1:(c + 1) * C, :] for c in range(nc)]
    bl = jnp.concatenate([jnp.broadcast_to(r, (C, hk)) for r in last], axis=0)
    bref = 0.5 * bl
    qf = q * scale
    qs = qf * jnp.exp(b - bref)
    ks = (k * jnp.exp(bref - b)).astype(BF16)
    kd = (k * jnp.exp(bl - b)).astype(BF16)
    qe = (qf * jnp.exp(b)).astype(BF16)

    intra, incr = [], []
    for c in range(nc):
        sl = slice(c * C, (c + 1) * C)
        qs4 = jnp.concatenate(
            [jnp.where((lane_k // GLA_DK) == h, qs[sl], 0.0) for h in range(GLA_HEADS)], axis=0).astype(BF16)
        att = jnp.where(causal4, _dot_nt(qs4, ks[sl]), 0.0)
        res = _dot(att.astype(BF16), vb[sl])
        o = jnp.where((lane_v // GLA_DV) == 0, res[0:C, :], 0.0)
        for h in range(1, GLA_HEADS):
            o = o + jnp.where((lane_v // GLA_DV) == h, res[h * C:(h + 1) * C, :], 0.0)
        intra.append(o)
        incr.append(jnp.where(blockdiag, _dot_tn(vb[sl], kd[sl]), 0.0))

    st = state_ref[...]
    outs = []
    for c in range(nc):
        outs.append(intra[c] + _dot_nt(qe[c * C:(c + 1) * C], st.astype(BF16)))
        st = st * jnp.exp(last[c]) + incr[c]
    state_ref[...] = st
    o = jnp.concatenate(outs, axis=0)

    lane_t = lax.broadcasted_iota(jnp.int32, (tm, hv), 1)
    osq = o * o
    inv = jnp.zeros_like(o)
    for h in range(GLA_HEADS):
        mh = (lane_t // GLA_DV) == h
        ms = jnp.sum(jnp.where(mh, osq, 0.0), axis=-1, keepdims=True) * (1.0 / GLA_DV)
        inv = jnp.where(mh, lax.rsqrt(ms + NORM_EPS), inv)
    g = p[:, 2 * hk + hv:2 * hk + 2 * hv]
    o_ref[...] = (o * inv * hn_ref[...] * (g * _sigmoid(g))).astype(o_ref.dtype)


def _lru(p, cw_ref, cb_ref, wax_ref, bax_ref, lam_ref, o_ref, xs_ref, h_ref, tm):
    W = LRU_WIDTH
    G = SUBLANES
    x = p[:, 0:W]
    gate = p[:, W:2 * W]
    xs_ref[G:, :] = x
    cw = cw_ref[...]
    xc = cb_ref[...] + cw[LRU_CONV - 1:LRU_CONV, :] * x
    for d in range(1, LRU_CONV):
        xc = xc + cw[LRU_CONV - 1 - d:LRU_CONV - d, :] * xs_ref[G - d:G - d + tm, :]
    xs_ref[:G, :] = x[tm - G:, :]
    ri = _dot(xc.astype(BF16), wax_ref[...]) + bax_ref[...]
    r = _sigmoid(ri[:, :W])
    ig = _sigmoid(ri[:, W:])
    lam = lam_ref[...]
    softplus_neg = jnp.maximum(-lam, 0.0) + jnp.log1p(jnp.exp(-jnp.abs(lam)))
    log_a = (-LRU_C * r) * softplus_neg
    a = jnp.exp(log_a)
    u = jnp.sqrt(-jnp.tanh(log_a) * (a * a + 1.0)) * (ig * xc)
    sub = lax.broadcasted_iota(jnp.int32, (G, W), 0)
    h = h_ref[...]
    outs = []
    for g in range(tm // G):
        ag = a[g * G:(g + 1) * G, :]
        ug = u[g * G:(g + 1) * G, :]
        for s in (1, 2, 4):
            live = sub >= s
            ug = ug + ag * jnp.where(live, pltpu.roll(ug, s, 0), 0.0)
            ag = ag * jnp.where(live, pltpu.roll(ag, s, 0), 1.0)
        hg = ug + ag * h
        outs.append(hg)
        h = hg[G - 1:G, :]
    h_ref[...] = h
    hs = jnp.concatenate(outs, axis=0)
    gelu = 0.5 * gate * (1.0 + jnp.tanh(0.7978845608028654 * (gate + 0.044715 * gate * gate * gate)))
    o_ref[...] = (hs * gelu).astype(o_ref.dtype)


def _w_in_plan():
    sp = np.cumsum([0, 256, 128, 32, 128, 128, 256, 16, 256, 256, 256, 256, 256, 256])
    mq, mkv, mkr, gq, gk, gv, ga, gg, bq, bk, bv, rx, rg = [(int(sp[n]), int(sp[n + 1] - sp[n])) for n in range(13)]
    keep = lambda c: [(c[1], c[0], 1.0)]
    zeros = lambda n: [(n, None, 0.0)]

    def swapped(start, half):
        return [(half, start + half, -1.0), (half, start, 1.0)]

    plan = (keep(mq) + keep(mkv) + keep(mkr) + swapped(mkr[0], MLA_ROPE // 2) + zeros(W_MLA - 256 - 128 - 64)
            + keep(gq) + keep(gk) + keep(gv) + keep(gg) + keep(ga) + zeros(LANES - GLA_GATE_RANK)
            + keep(bq) + keep(bk) + keep(bv)
            + keep(rx) + keep(rg))
    assert sum(p[0] for p in plan) == W_IN
    return plan, int(sp[-1])


def _build_w_in(w_ref, o_ref, rows_per_step=256):
    plan, n_src = _w_in_plan()
    blocks, cur, room = [], [], LANES
    for width, src, sign in plan:
        while width:
            take = min(width, room)
            cur.append((take, src, sign))
            src = None if src is None else src + take
            width -= take
            room -= take
            if room == 0:
                blocks.append(cur)
                cur, room = [], LANES
    for r0 in range(0, w_ref.shape[0], rows_per_step):
        rows = slice(r0, r0 + rows_per_step)
        for j, frags in enumerate(blocks):
            vals = []
            for width, src, sign in frags:
                if src is None:
                    vals.append(jnp.zeros((rows_per_step, width), F32))
                    continue
                a0 = src // LANES * LANES
                a1 = min(-(-(src + width) // LANES) * LANES, n_src)
                win = w_ref[rows, a0:a1]
                v = win[:, src - a0:src - a0 + width]
                vals.append(-v if sign < 0 else v)
            blk = vals[0] if len(vals) == 1 else jnp.concatenate(vals, axis=1)
            o_ref[rows, j * LANES:(j + 1) * LANES] = blk.astype(BF16)


def _prep_kernel(x_ref, g_ref, w_raw_ref,
                 qg_ref, wq_ref, kvg_ref, wkv_ref, place_ref, cq_ref, sq_ref, r32_ref,
                 cm_ref, sm_ref,
                 wa_ref, ba_ref, hn_ref,
                 cw_ref, cb_ref, wax_ref, bax_ref, lam_ref,
                 mq_ref, mk_ref, mv_ref, bq_ref, bk_ref, bv_ref, yb_ref, yd_ref,
                 w_ref, km_ref, state_ref, xs_ref, h_ref, *, tm):
    si = pl.program_id(1)

    @pl.when((pl.program_id(0) == 0) & (si == 0))
    def _():
        _build_w_in(w_raw_ref, w_ref)

    @pl.when(si == 0)
    def _():
        km_ref[...] = jnp.zeros_like(km_ref)
        state_ref[...] = jnp.zeros_like(state_ref)
        xs_ref[:SUBLANES, :] = jnp.zeros((SUBLANES, LRU_WIDTH), F32)
        h_ref[...] = jnp.zeros_like(h_ref)

    h = _rms(x_ref[...], g_ref[...]).astype(BF16)
    o_mla, o_gla, o_moba, o_lru = 0, W_MLA, W_MLA + W_GLA, W_MLA + W_GLA + W_MOBA
    p_mla = _dot(h, w_ref[:, o_mla:o_mla + W_MLA])
    p_moba = _dot(h, w_ref[:, o_moba:o_moba + W_MOBA])
    _mla_prep(p_mla, qg_ref, wq_ref, kvg_ref, wkv_ref, place_ref, cq_ref, sq_ref, r32_ref, mq_ref, mk_ref, mv_ref)
    p_lru = _dot(h, w_ref[:, o_lru:o_lru + W_LRU])
    _moba_prep(p_moba, cm_ref, sm_ref, bq_ref, bk_ref, bv_ref, km_ref, si, tm)
    p_gla = _dot(h, w_ref[:, o_gla:o_gla + W_GLA])
    _lru(p_lru, cw_ref, cb_ref, wax_ref, bax_ref, lam_ref, yd_ref, xs_ref, h_ref, tm)
    _gla(p_gla, wa_ref, ba_ref, hn_ref, yb_ref, state_ref, tm)


def _prep(x, w_in, l, consts, tables, B, S, tm):
    T = x.shape[0]
    ns = S // tm
    hw = ATTN_HEADS * LANES
    tok = lambda b, s: (b * ns + s, 0)
    pos = lambda b, s: (s, 0)
    g, qg, wq, kvg, wkv, place, wa, ba, hn, cw, cb, wax, bax, lam = consts
    cq, sq, r32, cm, sm = tables
    w_spec = pl.BlockSpec((None,) + w_in.shape[1:], lambda b, s: (l, 0, 0))
    c = lambda a: _layer_spec(a, l) if a.ndim == 3 else _const_spec(a.shape)
    t = lambda a: pl.BlockSpec((a.shape[0], tm), lambda b, s: (0, s))
    nb = max(2 * SUBLANES, -(-(S // MOBA_BLOCK) // SUBLANES) * SUBLANES)
    tok_t = lambda b, s: (0, b * ns + s)
    outs = [(hw, True), (hw, False), (GROUP_WIDTH, True), (hw, True), (hw, False), (GROUP_WIDTH, True),
            (GROUP_WIDTH, False), (GROUP_WIDTH, False)]
    return pl.pallas_call(
        functools.partial(_prep_kernel, tm=tm),
        grid=(B, ns),
        in_specs=[pl.BlockSpec((tm, D_MODEL), tok), c(g), w_spec,
                  c(qg), c(wq), c(kvg), c(wkv), c(place), t(cq), t(sq), t(r32),
                  t(cm), t(sm),
                  c(wa), c(ba), c(hn),
                  c(cw), c(cb), c(wax), c(bax), c(lam)],
        out_specs=[pl.BlockSpec((wd, tm), tok_t) if tr else pl.BlockSpec((tm, wd), tok) for wd, tr in outs],
        out_shape=[jax.ShapeDtypeStruct((wd, T) if tr else (T, wd), BF16) for wd, tr in outs],
        scratch_shapes=[pltpu.VMEM((D_MODEL, W_IN), BF16),
                        pltpu.VMEM((nb, GROUP_WIDTH), F32),
                        pltpu.VMEM((GLA_HEADS * GLA_DV, GLA_HEADS * GLA_DK), F32),
                        pltpu.VMEM((SUBLANES + tm, LRU_WIDTH), F32), pltpu.VMEM((1, LRU_WIDTH), F32)],
        compiler_params=_params(("arbitrary", "arbitrary")),
        name="mixer_prep",
    )(x, g, w_in, qg, wq, kvg, wkv, place, cq, sq, r32, cm, sm, wa, ba, hn, cw, cb, wax, bax, lam)


HEAD_V = 64
ACC_ROWS = HEAD_V + 16


def _attn_kernel(qt_ref, k_ref, vt_ref, o_ref, m_ref, acc_ref, *, tq, tk):
    i = pl.program_id(1)
    m_ref[...] = jnp.full(m_ref.shape, NEG_BIG, F32)
    acc_ref[...] = jnp.zeros(acc_ref.shape, F32)
    ones = jnp.ones((ACC_ROWS - HEAD_V, tk), BF16)

    def tile(j, diagonal, q0=0):
        keys = pl.ds(pl.multiple_of(j * tk, tk), tk)
        nq = tq - q0
        if diagonal:
            kpos = j * tk + lax.broadcasted_iota(jnp.int32, (tk, nq), 0)
            qpos = i * tq + q0 + lax.broadcasted_iota(jnp.int32, (tk, nq), 1)
            allowed = kpos <= qpos
        scores = [_dot(k_ref[keys, h * LANES:(h + 1) * LANES], qt_ref[h * LANES:(h + 1) * LANES, q0:])
                  for h in range(ATTN_HEADS)]
        for h in range(ATTN_HEADS):
            vt = jnp.concatenate([vt_ref[h * HEAD_V:(h + 1) * HEAD_V, keys], ones], axis=0)
            s = scores[h]
            if diagonal:
                s = jnp.where(allowed, s, NEG_BIG)
            m_prev = m_ref[h, :, q0:]
            m_next = jnp.maximum(m_prev, jnp.max(s, axis=0, keepdims=True))
            alpha = jnp.exp2(m_prev - m_next)
            p = jnp.exp2(s - jnp.tile(m_next, (tk // SUBLANES, 1)))
            acc_ref[h, :, q0:] = (jnp.tile(alpha, (ACC_ROWS // SUBLANES, 1)) * acc_ref[h, :, q0:]
                                  + _dot(vt, p.astype(BF16)))
            m_ref[h, :, q0:] = m_next

    n_full = (i * tq) // tk

    def body(j, _):
        tile(j, False)
        return 0

    lax.fori_loop(0, n_full, body, 0)
    for d in range(tq // tk):
        tile(n_full + d, True, d * tk)
    out_t = jnp.concatenate(
        [acc_ref[h, :HEAD_V] * jnp.tile(1.0 / acc_ref[h, HEAD_V:HEAD_V + SUBLANES], (HEAD_V // SUBLANES, 1))
         for h in range(ATTN_HEADS)], axis=0)
    o_ref[...] = out_t.T.astype(o_ref.dtype)


def _attn(qt, k, vt, B, S, tq, tk, name):
    T = k.shape[0]
    nq = S // tq
    hw = ATTN_HEADS * LANES
    stat = pltpu.VMEM((ATTN_HEADS, SUBLANES, tq), F32)
    return pl.pallas_call(
        functools.partial(_attn_kernel, tq=tq, tk=tk),
        grid=(B, nq),
        in_specs=[pl.BlockSpec((hw, tq), lambda b, i: (0, b * nq + i)),
                  pl.BlockSpec((S, hw), lambda b, i: (b, 0)),
                  pl.BlockSpec((GROUP_WIDTH, S), lambda b, i: (0, b))],
        out_specs=pl.BlockSpec((tq, GROUP_WIDTH), lambda b, i: (b * nq + i, 0)),
        out_shape=jax.ShapeDtypeStruct((T, GROUP_WIDTH), BF16),
        scratch_shapes=[stat, pltpu.VMEM((ATTN_HEADS, ACC_ROWS, tq), F32)],
        compiler_params=_params(("parallel", "arbitrary")),
        name=name,
    )(qt, k, vt)


def _ffn_kernel(x_ref, ya_ref, yb_ref, yc_ref, yd_ref, wo_ref, fg_ref, wi_ref, w2_ref, ng_ref, o_ref,
                *, final, chunk):
    W = GROUP_WIDTH
    x1 = x_ref[...]
    for n, y_ref in enumerate((ya_ref, yb_ref, yc_ref, yd_ref)):
        x1 = x1 + _dot(y_ref[...], wo_ref[n * W:(n + 1) * W, :])
    h = _rms(x1, fg_ref[...]).astype(BF16)
    parts = []
    for c in range(FFN_HIDDEN // chunk):
        g = _dot(h, wi_ref[:, c * chunk:(c + 1) * chunk])
        up = _dot(h, wi_ref[:, FFN_HIDDEN + c * chunk:FFN_HIDDEN + (c + 1) * chunk])
        act = (g * _sigmoid(g) * up).astype(BF16)
        parts.append(_dot(act, w2_ref[c * chunk:(c + 1) * chunk, :]))
    acc = x1 + sum(parts[1:], parts[0])
    if final:
        acc = _rms(acc, ng_ref[...])
    o_ref[...] = acc


def _layer_spec(stacked, l):
    nd = stacked.ndim - 1
    return pl.BlockSpec((None,) + stacked.shape[1:], lambda *_: (l,) + (0,) * nd, pipeline_mode=pl.Buffered(1))


def _ffn(x, ya, yb, yc, yd, wo, fg, wi, w2, ng, l, final, tm, chunk):
    T = x.shape[0]
    tok = lambda i: (i, 0)
    ysp = pl.BlockSpec((tm, GROUP_WIDTH), tok)
    return pl.pallas_call(
        functools.partial(_ffn_kernel, final=final, chunk=chunk),
        grid=(T // tm,),
        in_specs=[pl.BlockSpec((tm, D_MODEL), tok), ysp, ysp, ysp, ysp,
                  _layer_spec(wo, l), _layer_spec(fg, l), _layer_spec(wi, l),
                  _layer_spec(w2, l), _const_spec(ng.shape)],
        out_specs=pl.BlockSpec((tm, D_MODEL), tok),
        out_shape=jax.ShapeDtypeStruct((T, D_MODEL), F32),
        compiler_params=_params(("parallel",)),
        name="outproj_ffn",
    )(x, ya, yb, yc, yd, wo, fg, wi, w2, ng)


def _swap_cols(w, half):
    return jnp.concatenate([-w[..., half:2 * half], w[..., :half]], axis=-1)


def _layout_mla(w_uq, w_ukv):
    zq = jnp.zeros(w_uq.shape[:-1] + (LANES - MLA_NOPE - MLA_ROPE,), w_uq.dtype)
    zr = jnp.zeros(w_uq.shape[:-1] + (MLA_NOPE,), w_uq.dtype)
    qd = MLA_NOPE + MLA_ROPE
    plain, swapped = [], []
    for h in range(MLA_HEADS):
        nope = w_uq[..., h * qd:h * qd + MLA_NOPE]
        rope = w_uq[..., h * qd + MLA_NOPE:(h + 1) * qd]
        plain += [nope, rope, zq]
        swapped += [zr, _swap_cols(rope, MLA_ROPE // 2), zq]
    wq = jnp.concatenate(plain + swapped, axis=-1).astype(BF16)
    zk = jnp.zeros(w_ukv.shape[:-1] + (LANES - MLA_NOPE,), w_ukv.dtype)
    kd = MLA_NOPE + MLA_V
    kparts = []
    vparts = []
    for h in range(MLA_HEADS):
        kparts += [w_ukv[..., h * kd:h * kd + MLA_NOPE], zk]
        vparts += [w_ukv[..., h * kd + MLA_NOPE:(h + 1) * kd]]
    wkv = jnp.concatenate(kparts + vparts, axis=-1).astype(BF16)
    place = np.zeros((MLA_ROPE, MLA_HEADS * LANES), np.float32)
    for h in range(MLA_HEADS):
        place[np.arange(MLA_ROPE), h * LANES + MLA_NOPE + np.arange(MLA_ROPE)] = 1.0
    return wq, wkv, jnp.asarray(place, BF16)


def _tables(S):
    pos = jnp.arange(S, dtype=F32)[None, :]

    def cs(dim):
        inv_freq = ROPE_THETA ** (-jnp.arange(0, dim, 2, dtype=F32) / dim)
        ang = inv_freq[:, None] * pos
        return jnp.cos(ang), jnp.sin(ang)

    c, s = cs(MLA_ROPE)
    scale = (MLA_NOPE + MLA_ROPE) ** -0.5 * LOG2E
    one = jnp.ones((MLA_NOPE, S), F32)
    zero = jnp.zeros((MLA_NOPE, S), F32)
    pad1 = jnp.ones((LANES - MLA_NOPE - MLA_ROPE, S), F32)
    pad0 = jnp.zeros((LANES - MLA_NOPE - MLA_ROPE, S), F32)
    cq = jnp.concatenate([one, c, c, pad1], axis=0) * scale
    sq = jnp.concatenate([zero, s, s, pad0], axis=0) * scale
    r32 = jnp.concatenate([c, c, s, s, zero], axis=0)
    c, s = cs(MOBA_ROT)
    one = jnp.ones((MOBA_HD - MOBA_ROT, S), F32)
    zero = jnp.zeros((MOBA_HD - MOBA_ROT, S), F32)
    cm = jnp.concatenate([c, c, one] * 2, axis=0)
    sm = jnp.concatenate([-s, s, zero] * 2, axis=0)
    return cq, sq, r32, cm, sm


def _block_diag(w):
    depth, n, c, d = w.shape
    rows = []
    for j in range(n):
        rows.append(jnp.concatenate([jnp.zeros((depth, c, j * d), w.dtype), w[:, j],
                                     jnp.zeros((depth, c, (n - 1 - j) * d), w.dtype)], axis=-1))
    return jnp.concatenate(rows, axis=1)


def _stacked_consts(attn_norm, mla_q_norm, mla_w_uq, mla_kv_norm, mla_w_ukv, gla_w_a2, gla_b_a2,
                    gla_head_norm, lru_conv_w, lru_conv_b, lru_w_a, lru_b_a, lru_w_x, lru_b_x, lru_lambda):
    depth = attn_norm.shape[0]
    r3 = lambda v: v.reshape(depth, 1, -1).astype(F32)
    wq, wkv, place = _layout_mla(mla_w_uq, mla_w_ukv)
    wa = jnp.pad(gla_w_a2, ((0, 0), (0, LANES - GLA_GATE_RANK), (0, 0))).astype(BF16)
    hn = jnp.tile(r3(gla_head_norm), (1, 1, GLA_HEADS))
    wax = jnp.concatenate([_block_diag(lru_w_a), _block_diag(lru_w_x)], axis=-1).astype(BF16)
    bax = jnp.concatenate([r3(lru_b_a), r3(lru_b_x)], axis=-1)
    return (r3(attn_norm), r3(mla_q_norm), wq, r3(mla_kv_norm), wkv, place,
            wa, r3(gla_b_a2), hn, lru_conv_w.reshape(depth, LRU_CONV, LRU_WIDTH), r3(lru_conv_b), wax, bax,
            r3(lru_lambda))


def kernel(x, attn_norm, w_in, mla_q_norm, mla_w_uq, mla_kv_norm, mla_w_ukv, gla_w_a2, gla_b_a2, gla_head_norm,
           lru_conv_w, lru_conv_b, lru_w_a, lru_b_a, lru_w_x, lru_b_x, lru_lambda, w_out, ffn_norm, w_ffn_in,
           w_ffn_out, final_norm):
    B, S, D = x.shape
    depth = w_in.shape[0]
    T = B * S
    tq = min(S, ATTN_TQ)
    assert D == D_MODEL and S % max(PREP_TM, tq, FFN_TM) == 0 and S // MOBA_BLOCK <= MOBA_SLOT
    tables = _tables(S)
    row = lambda v: v.reshape(1, -1).astype(F32)
    xt = x.reshape(T, D)
    wo, wi, w2 = w_out.astype(BF16), w_ffn_in.astype(BF16), w_ffn_out.astype(BF16)
    fg = ffn_norm.reshape(depth, 1, D).astype(F32)
    consts = _stacked_consts(attn_norm, mla_q_norm, mla_w_uq, mla_kv_norm, mla_w_ukv, gla_w_a2, gla_b_a2,
                             gla_head_norm, lru_conv_w, lru_conv_b, lru_w_a, lru_b_a, lru_w_x, lru_b_x, lru_lambda)
    for l in range(depth):
        mq, mk, mv, bq, bk, bv, y_b, y_d = _prep(xt, w_in, l, consts, tables, B, S, PREP_TM)
        y_a = _attn(mq, mk, mv, B, S, tq, ATTN_TK, "mla_attn")
        y_c = _attn(bq, bk, bv, B, S, tq, ATTN_TK, "moba_attn")
        xt = _ffn(xt, y_a, y_b, y_c, y_d, wo, fg, wi, w2, row(final_norm), l, l == depth - 1,
                  FFN_TM, FFN_HIDDEN // 2)
    return xt.reshape(B, S, D)
```

```python
import functools

import numpy as np
import jax
import jax.numpy as jnp
from jax import lax
from jax.experimental import pallas as pl
from jax.experimental.pallas import tpu as pltpu

F32 = jnp.float32
BF16 = jnp.bfloat16

D_MODEL = 1024
GROUP_WIDTH = 256
ROPE_THETA = 500000.0
NORM_EPS = 1e-6
MLA_HEADS = 4
MLA_V = 64
MLA_NOPE = 64
MLA_ROPE = 32
MLA_Q_RANK = 256
MLA_KV_RANK = 128
GLA_HEADS = 4
GLA_DV = 64
GLA_DK = 32
GLA_GATE_RANK = 16
GLA_TAU = 16.0
GLA_CHUNK = 64
MOBA_HEADS = 4
MOBA_HD = 64
MOBA_ROT = 16
MOBA_BLOCK = 256
MOBA_TOPK = 3
LRU_WIDTH = 256
LRU_CONV = 4
LRU_C = 8.0
FFN_HIDDEN = 2816

LANES = 128
SUBLANES = 8
VMEM_LIMIT = 56 * 1024 * 1024

W_MLA = 512
W_GLA = 896
W_MOBA = 768
W_LRU = 512
W_IN = W_MLA + W_GLA + W_MOBA + W_LRU
NEG_BIG = -1e30
LOG2E = 1.4426950408889634
MOBA_SLOT = 32
PREP_TM = 512
FFN_TM = 1024
ATTN_HEADS = 4
ATTN_TQ = 4096
ATTN_TK = 256


def _rms(x, g):
    return x * lax.rsqrt(jnp.mean(x * x, axis=-1, keepdims=True) + NORM_EPS) * g


def _dot(a, b):
    return jnp.dot(a, b, preferred_element_type=F32)


def _dot_nt(a, b):
    return lax.dot_general(a, b, (((1,), (1,)), ((), ())), preferred_element_type=F32)


def _dot_tn(a, b):
    return lax.dot_general(a, b, (((0,), (0,)), ((), ())), preferred_element_type=F32)


def _split3(x):
    hi = x.astype(BF16)
    r = x - hi.astype(F32)
    mid = r.astype(BF16)
    lo = (r - mid.astype(F32)).astype(BF16)
    return hi, mid, lo


def _sigmoid(x):
    return 0.5 * jnp.tanh(0.5 * x) + 0.5


def _const_spec(shape):
    nd = len(shape)
    return pl.BlockSpec(shape, lambda *_: (0,) * nd, pipeline_mode=pl.Buffered(1))


def _params(sem):
    return pltpu.CompilerParams(dimension_semantics=sem, vmem_limit_bytes=VMEM_LIMIT)


def _mla_prep(p, qg_ref, wq_ref, kvg_ref, wkv_ref, place_ref, cq_ref, sq_ref, r32_ref, qt_ref, k_ref, vt_ref):
    hw = MLA_HEADS * LANES
    nq = _rms(p[:, :MLA_Q_RANK], qg_ref[...]).astype(BF16)
    q2 = _dot(nq, wq_ref[...])
    cq = cq_ref[...].T
    sq = sq_ref[...].T
    q = jnp.concatenate([q2[:, h * LANES:(h + 1) * LANES] * cq + q2[:, hw + h * LANES:hw + (h + 1) * LANES] * sq
                         for h in range(MLA_HEADS)], axis=1)
    qt_ref[...] = q.T.astype(BF16)
    nkv = _rms(p[:, MLA_Q_RANK:MLA_Q_RANK + MLA_KV_RANK], kvg_ref[...]).astype(BF16)
    kv = _dot(nkv, wkv_ref[...])
    c0 = MLA_Q_RANK + MLA_KV_RANK
    r32 = r32_ref[...].T
    k_pe = (p[:, c0:c0 + MLA_ROPE] * r32[:, :MLA_ROPE]
            + p[:, c0 + MLA_ROPE:c0 + 2 * MLA_ROPE] * r32[:, MLA_ROPE:2 * MLA_ROPE])
    k_ref[...] = (kv[:, :hw] + _dot(k_pe.astype(BF16), place_ref[...])).astype(BF16)
    vt_ref[...] = kv[:, hw:].T.astype(BF16)


def _moba_prep(p, cm_ref, sm_ref, qt_ref, k_ref, vt_ref, km_ref, si, tm):
    W = GROUP_WIDTH
    HD = MOBA_HD
    nb = km_ref.shape[0]
    cm = cm_ref[...].T
    sm = sm_ref[...].T
    half = MOBA_ROT // 2
    first_half = (lax.broadcasted_iota(jnp.int32, (tm, LANES), 1) % HD) < half

    def rope(x):
        out = []
        for pr in range(W // LANES):
            xp = x[:, pr * LANES:(pr + 1) * LANES]
            partner = jnp.where(first_half, pltpu.roll(xp, LANES - half, 1), pltpu.roll(xp, half, 1))
            out.append(xp * cm + partner * sm)
        return jnp.concatenate(out, axis=1)

    q = rope(p[:, 0:W])
    k = rope(p[:, W:2 * W])
    vt_ref[...] = p[:, 2 * W:3 * W].T.astype(BF16)
    nblk = tm // MOBA_BLOCK
    for j in range(nblk):
        mean = jnp.sum(k[j * MOBA_BLOCK:(j + 1) * MOBA_BLOCK, :], axis=0, keepdims=True) * (1.0 / MOBA_BLOCK)
        km_ref[pl.ds(si * nblk + j, 1), :] = mean
    km = km_ref[...]
    n_idx = lax.broadcasted_iota(jnp.int32, (nb, tm), 0)
    blk = (si * tm + lax.broadcasted_iota(jnp.int32, (nb, tm), 1)) // MOBA_BLOCK
    past = n_idx < blk
    own = n_idx == blk
    n_f = n_idx.astype(F32)
    lane_m = lax.broadcasted_iota(jnp.int32, (nb, W), 1)
    km_heads = jnp.concatenate([jnp.where((lane_m // HD) == h, km, 0.0) for h in range(MOBA_HEADS)], axis=0)
    m_hi, m_mid, m_lo = _split3(km_heads)
    q_hi, q_mid, q_lo = _split3(q)
    nm = MOBA_HEADS * nb
    g_hi = _dot_nt(jnp.concatenate([m_hi, m_mid, m_lo], axis=0), q_hi)
    g_mid = _dot_nt(jnp.concatenate([m_hi, m_mid], axis=0), q_mid)
    g_lo = _dot_nt(m_hi, q_lo)
    gates = (((g_lo + g_hi[2 * nm:]) + g_mid[nm:]) + (g_mid[:nm] + g_hi[nm:2 * nm])) + g_hi[:nm]
    masks = []
    for h in range(MOBA_HEADS):
        gate = jnp.where(past, gates[h * nb:(h + 1) * nb], -jnp.inf)
        keep = own
        for _ in range(MOBA_TOPK):
            mx = jnp.max(gate, axis=0, keepdims=True)
            first = jnp.min(jnp.where(gate == mx, n_f, float(LANES)), axis=0, keepdims=True)
            pick = (n_f == first) & (mx > -jnp.inf)
            keep = keep | pick
            gate = jnp.where(pick, -jnp.inf, gate)
        masks.append(jnp.where(keep, 0.0, NEG_BIG))
        if nb < MOBA_SLOT:
            masks.append(jnp.zeros((MOBA_SLOT - nb, tm), F32))
    mask_hi = jnp.concatenate(masks, axis=0).T
    mask_lo = pltpu.roll(mask_hi, HD, 1)
    lane = lax.broadcasted_iota(jnp.int32, (tm, LANES), 1)
    head_lanes = lane < HD
    q_scaled = q * (MOBA_HD ** -0.5 * LOG2E)
    own_blk = (si * tm + lax.broadcasted_iota(jnp.int32, (tm, LANES), 0)) // MOBA_BLOCK
    kone = [(lane == own_blk + HD + slot * MOBA_SLOT).astype(F32) for slot in range(2)]
    q_ext = []
    for h in range(MOBA_HEADS):
        pr = h // 2
        q_pair = q_scaled[:, pr * LANES:(pr + 1) * LANES]
        k_pair = k[:, pr * LANES:(pr + 1) * LANES]
        if h % 2:
            q_pair = pltpu.roll(q_pair, HD, 1)
            k_pair = pltpu.roll(k_pair, HD, 1)
        q_ext.append(jnp.where(head_lanes, q_pair, mask_lo if h < 2 else mask_hi))
        k_ref[:, h * LANES:(h + 1) * LANES] = jnp.where(
            head_lanes, k_pair, kone[h % 2]).astype(BF16)
    qt_ref[...] = jnp.concatenate(q_ext, axis=1).T.astype(BF16)


def _gla(p, wa_ref, ba_ref, hn_ref, o_ref, state_ref, tm):
    C = GLA_CHUNK
    hk = GLA_HEADS * GLA_DK
    hv = GLA_HEADS * GLA_DV
    nc = tm // C
    rc = lax.broadcasted_iota(jnp.int32, (tm, hk), 0) % C
    lane_k = lax.broadcasted_iota(jnp.int32, (C, hk), 1)
    lane_v = lax.broadcasted_iota(jnp.int32, (C, hv), 1)
    row4 = lax.broadcasted_iota(jnp.int32, (GLA_HEADS * C, C), 0)
    col4 = lax.broadcasted_iota(jnp.int32, (GLA_HEADS * C, C), 1)
    causal4 = (row4 % C) >= col4
    sr = lax.broadcasted_iota(jnp.int32, (hv, hk), 0)
    sc = lax.broadcasted_iota(jnp.int32, (hv, hk), 1)
    blockdiag = (sr // GLA_DV) == (sc // GLA_DK)
    scale = GLA_DK ** -0.5

    q = p[:, 0:hk]
    k = p[:, hk:2 * hk]
    vb = p[:, 2 * hk:2 * hk + hv].astype(BF16)
    a_low = p[:, 2 * hk + 2 * hv:2 * hk + 2 * hv + LANES]
    a_lin = _dot(a_low.astype(BF16), wa_ref[...]) + ba_ref[...]
    b = (jnp.minimum(a_lin, 0.0) - jnp.log(1.0 + jnp.exp(-jnp.abs(a_lin)))) * (1.0 / GLA_TAU)
    for s in (1, 2, 4, 8, 16, 32):
        b = b + jnp.where(rc >= s, pltpu.roll(b, s, 0), 0.0)
    last = [b[c * C + C - 1:(c + 1) * C, :] for c in range(nc)]
    bl = jnp.concatenate([jnp.broadcast_to(r, (C, hk)) for r in last], axis=0)
    bref = 0.5 * bl
    qf = q * scale
    qs = qf * jnp.exp(b - bref)
    ks = (k * jnp.exp(bref - b)).astype(BF16)
    kd = (k * jnp.exp(bl - b)).astype(BF16)
    qe = (qf * jnp.exp(b)).astype(BF16)

    intra, incr = [], []
    for c in range(nc):
        sl = slice(c * C, (c + 1) * C)
        qs4 = jnp.concatenate(
            [jnp.where((lane_k // GLA_DK) == h, qs[sl], 0.0) for h in range(GLA_HEADS)], axis=0).astype(BF16)
        att = jnp.where(causal4, _dot_nt(qs4, ks[sl]), 0.0)
        res = _dot(att.astype(BF16), vb[sl])
        o = jnp.where((lane_v // GLA_DV) == 0, res[0:C, :], 0.0)
        for h in range(1, GLA_HEADS):
            o = o + jnp.where((lane_v // GLA_DV) == h, res[h * C:(h + 1) * C, :], 0.0)
        intra.append(o)
        incr.append(jnp.where(blockdiag, _dot_tn(vb[sl], kd[sl]), 0.0))

    st = state_ref[...]
    outs = []
    for c in range(nc):
        outs.append(intra[c] + _dot_nt(qe[c * C:(c + 1) * C], st.astype(BF16)))
        st = st * jnp.exp(last[c]) + incr[c]
    state_ref[...] = st
    o = jnp.concatenate(outs, axis=0)

    lane_t = lax.broadcasted_iota(jnp.int32, (tm, hv), 1)
    osq = o * o
    inv = jnp.zeros_like(o)
    for h in range(GLA_HEADS):
        mh = (lane_t // GLA_DV) == h
        ms = jnp.sum(jnp.where(mh, osq, 0.0), axis=-1, keepdims=True) * (1.0 / GLA_DV)
        inv = jnp.where(mh, lax.rsqrt(ms + NORM_EPS), inv)
    g = p[:, 2 * hk + hv:2 * hk + 2 * hv]
    o_ref[...] = (o * inv * hn_ref[...] * (g * _sigmoid(g))).astype(o_ref.dtype)


def _lru(p, cw_ref, cb_ref, wax_ref, bax_ref, lam_ref, o_ref, xs_ref, h_ref, tm):
    W = LRU_WIDTH
    G = SUBLANES
    x = p[:, 0:W]
    gate = p[:, W:2 * W]
    xs_ref[G:, :] = x
    cw = cw_ref[...]
    xc = cb_ref[...] + cw[LRU_CONV - 1:LRU_CONV, :] * x
    for d in range(1, LRU_CONV):
        xc = xc + cw[LRU_CONV - 1 - d:LRU_CONV - d, :] * xs_ref[G - d:G - d + tm, :]
    xs_ref[:G, :] = x[tm - G:, :]
    ri = _dot(xc.astype(BF16), wax_ref[...]) + bax_ref[...]
    r = _sigmoid(ri[:, :W])
    ig = _sigmoid(ri[:, W:])
    lam = lam_ref[...]
    softplus_neg = jnp.maximum(-lam, 0.0) + jnp.log1p(jnp.exp(-jnp.abs(lam)))
    log_a = (-LRU_C * r) * softplus_neg
    a = jnp.exp(log_a)
    u = jnp.sqrt(-jnp.tanh(log_a) * (a * a + 1.0)) * (ig * xc)
    sub = lax.broadcasted_iota(jnp.int32, (G, W), 0)
    h = h_ref[...]
    outs = []
    for g in range(tm // G):
        ag = a[g * G:(g + 1) * G, :]
        ug = u[g * G:(g + 1) * G, :]
        for s in (1, 2, 4):
            live = sub >= s
            ug = ug + ag * jnp.where(live, pltpu.roll(ug, s, 0), 0.0)
            ag = ag * jnp.where(live, pltpu.roll(ag, s, 0), 1.0)
        hg = ug + ag * h
        outs.append(hg)
        h = hg[G - 1:G, :]
    h_ref[...] = h
    hs = jnp.concatenate(outs, axis=0)
    gelu = 0.5 * gate * (1.0 + jnp.tanh(0.7978845608028654 * (gate + 0.044715 * gate * gate * gate)))
    o_ref[...] = (hs * gelu).astype(o_ref.dtype)


def _w_in_plan():
    sp = np.cumsum([0, 256, 128, 32, 128, 128, 256, 16, 256, 256, 256, 256, 256, 256])
    mq, mkv, mkr, gq, gk, gv, ga, gg, bq, bk, bv, rx, rg = [(int(sp[n]), int(sp[n + 1] - sp[n])) for n in range(13)]
    keep = lambda c: [(c[1], c[0], 1.0)]
    zeros = lambda n: [(n, None, 0.0)]

    def swapped(start, half):
        return [(half, start + half, -1.0), (half, start, 1.0)]

    plan = (keep(mq) + keep(mkv) + keep(mkr) + swapped(mkr[0], MLA_ROPE // 2) + zeros(W_MLA - 256 - 128 - 64)
            + keep(gq) + keep(gk) + keep(gv) + keep(gg) + keep(ga) + zeros(LANES - GLA_GATE_RANK)
            + keep(bq) + keep(bk) + keep(bv)
            + keep(rx) + keep(rg))
    assert sum(p[0] for p in plan) == W_IN
    return plan, int(sp[-1])


def _build_w_in(w_ref, o_ref, rows_per_step=256):
    plan, n_src = _w_in_plan()
    blocks, cur, room = [], [], LANES
    for width, src, sign in plan:
        while width:
            take = min(width, room)
            cur.append((take, src, sign))
            src = None if src is None else src + take
            width -= take
            room -= take
            if room == 0:
                blocks.append(cur)
                cur, room = [], LANES
    for r0 in range(0, w_ref.shape[0], rows_per_step):
        rows = slice(r0, r0 + rows_per_step)
        for j, frags in enumerate(blocks):
            vals = []
            for width, src, sign in frags:
                if src is None:
                    vals.append(jnp.zeros((rows_per_step, width), F32))
                    continue
                a0 = src // LANES * LANES
                a1 = min(-(-(src + width) // LANES) * LANES, n_src)
                win = w_ref[rows, a0:a1]
                v = win[:, src - a0:src - a0 + width]
                vals.append(-v if sign < 0 else v)
            blk = vals[0] if len(vals) == 1 else jnp.concatenate(vals, axis=1)
            o_ref[rows, j * LANES:(j + 1) * LANES] = blk.astype(BF16)


def _prep_kernel(x_ref, g_ref, w_raw_ref,
                 qg_ref, wq_ref, kvg_ref, wkv_ref, place_ref, cq_ref, sq_ref, r32_ref,
                 cm_ref, sm_ref,
                 wa_ref, ba_ref, hn_ref,
                 cw_ref, cb_ref, wax_ref, bax_ref, lam_ref,
                 mq_ref, mk_ref, mv_ref, bq_ref, bk_ref, bv_ref, yb_ref, yd_ref,
                 w_ref, km_ref, state_ref, xs_ref, h_ref, *, tm):
    si = pl.program_id(1)

    @pl.when((pl.program_id(0) == 0) & (si == 0))
    def _():
        _build_w_in(w_raw_ref, w_ref)

    @pl.when(si == 0)
    def _():
        km_ref[...] = jnp.zeros_like(km_ref)
        state_ref[...] = jnp.zeros_like(state_ref)
        xs_ref[:SUBLANES, :] = jnp.zeros((SUBLANES, LRU_WIDTH), F32)
        h_ref[...] = jnp.zeros_like(h_ref)

    h = _rms(x_ref[...], g_ref[...]).astype(BF16)
    o_mla, o_gla, o_moba, o_lru = 0, W_MLA, W_MLA + W_GLA, W_MLA + W_GLA + W_MOBA
    p_mla = _dot(h, w_ref[:, o_mla:o_mla + W_MLA])
    p_moba = _dot(h, w_ref[:, o_moba:o_moba + W_MOBA])
    _mla_prep(p_mla, qg_ref, wq_ref, kvg_ref, wkv_ref, place_ref, cq_ref, sq_ref, r32_ref, mq_ref, mk_ref, mv_ref)
    p_lru = _dot(h, w_ref[:, o_lru:o_lru + W_LRU])
    _moba_prep(p_moba, cm_ref, sm_ref, bq_ref, bk_ref, bv_ref, km_ref, si, tm)
    p_gla = _dot(h, w_ref[:, o_gla:o_gla + W_GLA])
    _lru(p_lru, cw_ref, cb_ref, wax_ref, bax_ref, lam_ref, yd_ref, xs_ref, h_ref, tm)
    _gla(p_gla, wa_ref, ba_ref, hn_ref, yb_ref, state_ref, tm)


def _prep(x, w_in, l, consts, tables, B, S, tm):
    T = x.shape[0]
    ns = S // tm
    hw = ATTN_HEADS * LANES
    tok = lambda b, s: (b * ns + s, 0)
    pos = lambda b, s: (s, 0)
    g, qg, wq, kvg, wkv, place, wa, ba, hn, cw, cb, wax, bax, lam = consts
    cq, sq, r32, cm, sm = tables
    w_spec = pl.BlockSpec((None,) + w_in.shape[1:], lambda b, s: (l, 0, 0))
    c = lambda a: _layer_spec(a, l) if a.ndim == 3 else _const_spec(a.shape)
    t = lambda a: pl.BlockSpec((a.shape[0], tm), lambda b, s: (0, s))
    nb = max(2 * SUBLANES, -(-(S // MOBA_BLOCK) // SUBLANES) * SUBLANES)
    tok_t = lambda b, s: (0, b * ns + s)
    outs = [(hw, True), (hw, False), (GROUP_WIDTH, True), (hw, True), (hw, False), (GROUP_WIDTH, True),
            (GROUP_WIDTH, False), (GROUP_WIDTH, False)]
    return pl.pallas_call(
        functools.partial(_prep_kernel, tm=tm),
        grid=(B, ns),
        in_specs=[pl.BlockSpec((tm, D_MODEL), tok), c(g), w_spec,
                  c(qg), c(wq), c(kvg), c(wkv), c(place), t(cq), t(sq), t(r32),
                  t(cm), t(sm),
                  c(wa), c(ba), c(hn),
                  c(cw), c(cb), c(wax), c(bax), c(lam)],
        out_specs=[pl.BlockSpec((wd, tm), tok_t) if tr else pl.BlockSpec((tm, wd), tok) for wd, tr in outs],
        out_shape=[jax.ShapeDtypeStruct((wd, T) if tr else (T, wd), BF16) for wd, tr in outs],
        scratch_shapes=[pltpu.VMEM((D_MODEL, W_IN), BF16),
                        pltpu.VMEM((nb, GROUP_WIDTH), F32),
                        pltpu.VMEM((GLA_HEADS * GLA_DV, GLA_HEADS * GLA_DK), F32),
                        pltpu.VMEM((SUBLANES + tm, LRU_WIDTH), F32), pltpu.VMEM((1, LRU_WIDTH), F32)],
        compiler_params=_params(("arbitrary", "arbitrary")),
        name="mixer_prep",
    )(x, g, w_in, qg, wq, kvg, wkv, place, cq, sq, r32, cm, sm, wa, ba, hn, cw, cb, wax, bax, lam)


HEAD_V = 64
ACC_ROWS = HEAD_V + 16


def _attn_kernel(qt_ref, k_ref, vt_ref, o_ref, m_ref, acc_ref, *, tq, tk):
    i = pl.program_id(1)
    m_ref[...] = jnp.full(m_ref.shape, NEG_BIG, F32)
    acc_ref[...] = jnp.zeros(acc_ref.shape, F32)
    ones = jnp.ones((ACC_ROWS - HEAD_V, tk), BF16)

    def tile(j, diagonal, q0=0):
        keys = pl.ds(pl.multiple_of(j * tk, tk), tk)
        nq = tq - q0
        if diagonal:
            kpos = j * tk + lax.broadcasted_iota(jnp.int32, (tk, nq), 0)
            qpos = i * tq + q0 + lax.broadcasted_iota(jnp.int32, (tk, nq), 1)
            allowed = kpos <= qpos
        scores = [_dot(k_ref[keys, h * LANES:(h + 1) * LANES], qt_ref[h * LANES:(h + 1) * LANES, q0:])
                  for h in range(ATTN_HEADS)]
        for h in range(ATTN_HEADS):
            vt = jnp.concatenate([vt_ref[h * HEAD_V:(h + 1) * HEAD_V, keys], ones], axis=0)
            s = scores[h]
            if diagonal:
                s = jnp.where(allowed, s, NEG_BIG)
            m_prev = m_ref[h, :, q0:]
            m_next = jnp.maximum(m_prev, jnp.max(s, axis=0, keepdims=True))
            alpha = jnp.exp2(m_prev - m_next)
            p = jnp.exp2(s - jnp.tile(m_next, (tk // SUBLANES, 1)))
            acc_ref[h, :, q0:] = (jnp.tile(alpha, (ACC_ROWS // SUBLANES, 1)) * acc_ref[h, :, q0:]
                                  + _dot(vt, p.astype(BF16)))
            m_ref[h, :, q0:] = m_next

    n_full = (i * tq) // tk

    def body(j, _):
        tile(j, False)
        return 0

    lax.fori_loop(0, n_full, body, 0)
    for d in range(tq // tk):
        tile(n_full + d, True, d * tk)
    out_t = jnp.concatenate(
        [acc_ref[h, :HEAD_V] * jnp.tile(1.0 / acc_ref[h, HEAD_V:HEAD_V + SUBLANES], (HEAD_V // SUBLANES, 1))
         for h in range(ATTN_HEADS)], axis=0)
    o_ref[...] = out_t.T.astype(o_ref.dtype)


def _attn(qt, k, vt, B, S, tq, tk, name):
    T = k.shape[0]
    nq = S // tq
    hw = ATTN_HEADS * LANES
    stat = pltpu.VMEM((ATTN_HEADS, SUBLANES, tq), F32)
    return pl.pallas_call(
        functools.partial(_attn_kernel, tq=tq, tk=tk),
        grid=(B, nq),
        in_specs=[pl.BlockSpec((hw, tq), lambda b, i: (0, b * nq + i)),
                  pl.BlockSpec((S, hw), lambda b, i: (b, 0)),
                  pl.BlockSpec((GROUP_WIDTH, S), lambda b, i: (0, b))],
        out_specs=pl.BlockSpec((tq, GROUP_WIDTH), lambda b, i: (b * nq + i, 0)),
        out_shape=jax.ShapeDtypeStruct((T, GROUP_WIDTH), BF16),
        scratch_shapes=[stat, pltpu.VMEM((ATTN_HEADS, ACC_ROWS, tq), F32)],
        compiler_params=_params(("parallel", "arbitrary")),
        name=name,
    )(qt, k, vt)


def _ffn_kernel(x_ref, ya_ref, yb_ref, yc_ref, yd_ref, wo_ref, fg_ref, wi_ref, w2_ref, ng_ref, o_ref,
                *, final, chunk):
    W = GROUP_WIDTH
    x1 = x_ref[...]
    for n, y_ref in enumerate((ya_ref, yb_ref, yc_ref, yd_ref)):
        x1 = x1 + _dot(y_ref[...], wo_ref[n * W:(n + 1) * W, :])
    h = _rms(x1, fg_ref[...]).astype(BF16)
    parts = []
    for c in range(FFN_HIDDEN // chunk):
        g = _dot(h, wi_ref[:, c * chunk:(c + 1) * chunk])
        up = _dot(h, wi_ref[:, FFN_HIDDEN + c * chunk:FFN_HIDDEN + (c + 1) * chunk])
        act = (g * _sigmoid(g) * up).astype(BF16)
        parts.append(_dot(act, w2_ref[c * chunk:(c + 1) * chunk, :]))
    acc = x1 + sum(parts[1:], parts[0])
    if final:
        acc = _rms(acc, ng_ref[...])
    o_ref[...] = acc


def _layer_spec(stacked, l):
    nd = stacked.ndim - 1
    return pl.BlockSpec((None,) + stacked.shape[1:], lambda *_: (l,) + (0,) * nd, pipeline_mode=pl.Buffered(1))


def _ffn(x, ya, yb, yc, yd, wo, fg, wi, w2, ng, l, final, tm, chunk):
    T = x.shape[0]
    tok = lambda i: (i, 0)
    ysp = pl.BlockSpec((tm, GROUP_WIDTH), tok)
    return pl.pallas_call(
        functools.partial(_ffn_kernel, final=final, chunk=chunk),
        grid=(T // tm,),
        in_specs=[pl.BlockSpec((tm, D_MODEL), tok), ysp, ysp, ysp, ysp,
                  _layer_spec(wo, l), _layer_spec(fg, l), _layer_spec(wi, l),
                  _layer_spec(w2, l), _const_spec(ng.shape)],
        out_specs=pl.BlockSpec((tm, D_MODEL), tok),
        out_shape=jax.ShapeDtypeStruct((T, D_MODEL), F32),
        compiler_params=_params(("parallel",)),
        name="outproj_ffn",
    )(x, ya, yb, yc, yd, wo, fg, wi, w2, ng)


def _swap_cols(w, half):
    return jnp.concatenate([-w[..., half:2 * half], w[..., :half]], axis=-1)


def _layout_mla(w_uq, w_ukv):
    zq = jnp.zeros(w_uq.shape[:-1] + (LANES - MLA_NOPE - MLA_ROPE,), w_uq.dtype)
    zr = jnp.zeros(w_uq.shape[:-1] + (MLA_NOPE,), w_uq.dtype)
    qd = MLA_NOPE + MLA_ROPE
    plain, swapped = [], []
    for h in range(MLA_HEADS):
        nope = w_uq[..., h * qd:h * qd + MLA_NOPE]
        rope = w_uq[..., h * qd + MLA_NOPE:(h + 1) * qd]
        plain += [nope, rope, zq]
        swapped += [zr, _swap_cols(rope, MLA_ROPE // 2), zq]
    wq = jnp.concatenate(plain + swapped, axis=-1).astype(BF16)
    zk = jnp.zeros(w_ukv.shape[:-1] + (LANES - MLA_NOPE,), w_ukv.dtype)
    kd = MLA_NOPE + MLA_V
    kparts = []
    vparts = []
    for h in range(MLA_HEADS):
        kparts += [w_ukv[..., h * kd:h * kd + MLA_NOPE], zk]
        vparts += [w_ukv[..., h * kd + MLA_NOPE:(h + 1) * kd]]
    wkv = jnp.concatenate(kparts + vparts, axis=-1).astype(BF16)
    place = np.zeros((MLA_ROPE, MLA_HEADS * LANES), np.float32)
    for h in range(MLA_HEADS):
        place[np.arange(MLA_ROPE), h * LANES + MLA_NOPE + np.arange(MLA_ROPE)] = 1.0
    return wq, wkv, jnp.asarray(place, BF16)


def _tables(S):
    pos = jnp.arange(S, dtype=F32)[None, :]

    def cs(dim):
        inv_freq = ROPE_THETA ** (-jnp.arange(0, dim, 2, dtype=F32) / dim)
        ang = inv_freq[:, None] * pos
        return jnp.cos(ang), jnp.sin(ang)

    c, s = cs(MLA_ROPE)
    scale = (MLA_NOPE + MLA_ROPE) ** -0.5 * LOG2E
    one = jnp.ones((MLA_NOPE, S), F32)
    zero = jnp.zeros((MLA_NOPE, S), F32)
    pad1 = jnp.ones((LANES - MLA_NOPE - MLA_ROPE, S), F32)
    pad0 = jnp.zeros((LANES - MLA_NOPE - MLA_ROPE, S), F32)
    cq = jnp.concatenate([one, c, c, pad1], axis=0) * scale
    sq = jnp.concatenate([zero, s, s, pad0], axis=0) * scale
    r32 = jnp.concatenate([c, c, s, s, zero], axis=0)
    c, s = cs(MOBA_ROT)
    one = jnp.ones((MOBA_HD - MOBA_ROT, S), F32)
    zero = jnp.zeros((MOBA_HD - MOBA_ROT, S), F32)
    cm = jnp.concatenate([c, c, one] * 2, axis=0)
    sm = jnp.concatenate([-s, s, zero] * 2, axis=0)
    return cq, sq, r32, cm, sm


def _block_diag(w):
    depth, n, c, d = w.shape
    rows = []
    for j in range(n):
        rows.append(jnp.concatenate([jnp.zeros((depth, c, j * d), w.dtype), w[:, j],
                                     jnp.zeros((depth, c, (n - 1 - j) * d), w.dtype)], axis=-1))
    return jnp.concatenate(rows, axis=1)


def _stacked_consts(attn_norm, mla_q_norm, mla_w_uq, mla_kv_norm, mla_w_ukv, gla_w_a2, gla_b_a2,
                    gla_head_norm, lru_conv_w, lru_conv_b, lru_w_a, lru_b_a, lru_w_x, lru_b_x, lru_lambda):
    depth = attn_norm.shape[0]
    r3 = lambda v: v.reshape(depth, 1, -1).astype(F32)
    wq, wkv, place = _layout_mla(mla_w_uq, mla_w_ukv)
    wa = jnp.pad(gla_w_a2, ((0, 0), (0, LANES - GLA_GATE_RANK), (0, 0))).astype(BF16)
    hn = jnp.tile(r3(gla_head_norm), (1, 1, GLA_HEADS))
    wax = jnp.concatenate([_block_diag(lru_w_a), _block_diag(lru_w_x)], axis=-1).astype(BF16)
    bax = jnp.concatenate([r3(lru_b_a), r3(lru_b_x)], axis=-1)
    return (r3(attn_norm), r3(mla_q_norm), wq, r3(mla_kv_norm), wkv, place,
            wa, r3(gla_b_a2), hn, lru_conv_w.reshape(depth, LRU_CONV, LRU_WIDTH), r3(lru_conv_b), wax, bax,
            r3(lru_lambda))


def kernel(x, attn_norm, w_in, mla_q_norm, mla_w_uq, mla_kv_norm, mla_w_ukv, gla_w_a2, gla_b_a2, gla_head_norm,
           lru_conv_w, lru_conv_b, lru_w_a, lru_b_a, lru_w_x, lru_b_x, lru_lambda, w_out, ffn_norm, w_ffn_in,
           w_ffn_out, final_norm):
    B, S, D = x.shape
    depth = w_in.shape[0]
    T = B * S
    tq = min(S, ATTN_TQ)
    assert D == D_MODEL and S % max(PREP_TM, tq, FFN_TM) == 0 and S // MOBA_BLOCK <= MOBA_SLOT
    tables = _tables(S)
    row = lambda v: v.reshape(1, -1).astype(F32)
    xt = x.reshape(T, D)
    wo, wi, w2 = w_out.astype(BF16), w_ffn_in.astype(BF16), w_ffn_out.astype(BF16)
    fg = ffn_norm.reshape(depth, 1, D).astype(F32)
    consts = _stacked_consts(attn_norm, mla_q_norm, mla_w_uq, mla_kv_norm, mla_w_ukv, gla_w_a2, gla_b_a2,
                             gla_head_norm, lru_conv_w, lru_conv_b, lru_w_a, lru_b_a, lru_w_x, lru_b_x, lru_lambda)
    for l in range(depth):
        mq, mk, mv, bq, bk, bv, y_b, y_d = _prep(xt, w_in, l, consts, tables, B, S, PREP_TM)
        y_a = _attn(mq, mk, mv, B, S, tq, ATTN_TK, "mla_attn")
        y_c = _attn(bq, bk, bv, B, S, tq, ATTN_TK, "moba_attn")
        xt = _ffn(xt, y_a, y_b, y_c, y_d, wo, fg, wi, w2, row(final_norm), l, l == depth - 1,
                  FFN_TM, FFN_HIDDEN // 2)
    return xt.reshape(B, S, D)
```

```python
import functools

import numpy as np
import jax
import jax.numpy as jnp
from jax import lax
from jax.experimental import pallas as pl
from jax.experimental.pallas import tpu as pltpu

F32 = jnp.float32
BF16 = jnp.bfloat16

D_MODEL = 1024
GROUP_WIDTH = 256
ROPE_THETA = 500000.0
NORM_EPS = 1e-6
MLA_HEADS = 4
MLA_V = 64
MLA_NOPE = 64
MLA_ROPE = 32
MLA_Q_RANK = 256
MLA_KV_RANK = 128
GLA_HEADS = 4
GLA_DV = 64
GLA_DK = 32
GLA_GATE_RANK = 16
GLA_TAU = 16.0
GLA_CHUNK = 64
MOBA_HEADS = 4
MOBA_HD = 64
MOBA_ROT = 16
MOBA_BLOCK = 256
MOBA_TOPK = 3
LRU_WIDTH = 256
LRU_CONV = 4
LRU_C = 8.0
FFN_HIDDEN = 2816

LANES = 128
SUBLANES = 8
VMEM_LIMIT = 56 * 1024 * 1024

W_MLA = 512
W_GLA = 896
W_MOBA = 768
W_LRU = 512
W_IN = W_MLA + W_GLA + W_MOBA + W_LRU
NEG_BIG = -1e30
LOG2E = 1.4426950408889634
MOBA_SLOT = 32
PREP_TM = 512
FFN_TM = 1024
ATTN_HEADS = 4
ATTN_TQ = 4096
ATTN_TK = 256


def _rms(x, g):
    return x * lax.rsqrt(jnp.mean(x * x, axis=-1, keepdims=True) + NORM_EPS) * g


def _dot(a, b):
    return jnp.dot(a, b, preferred_element_type=F32)


def _dot_nt(a, b):
    return lax.dot_general(a, b, (((1,), (1,)), ((), ())), preferred_element_type=F32)


def _dot_tn(a, b):
    return lax.dot_general(a, b, (((0,), (0,)), ((), ())), preferred_element_type=F32)


def _split3(x):
    hi = x.astype(BF16)
    r = x - hi.astype(F32)
    mid = r.astype(BF16)
    lo = (r - mid.astype(F32)).astype(BF16)
    return hi, mid, lo


def _sigmoid(x):
    return 0.5 * jnp.tanh(0.5 * x) + 0.5


def _const_spec(shape):
    nd = len(shape)
    return pl.BlockSpec(shape, lambda *_: (0,) * nd, pipeline_mode=pl.Buffered(1))


def _params(sem):
    return pltpu.CompilerParams(dimension_semantics=sem, vmem_limit_bytes=VMEM_LIMIT)


def _mla_prep(p, qg_ref, wq_ref, kvg_ref, wkv_ref, place_ref, cq_ref, sq_ref, r32_ref, qt_ref, k_ref, vt_ref):
    hw = MLA_HEADS * LANES
    nq = _rms(p[:, :MLA_Q_RANK], qg_ref[...]).astype(BF16)
    q2 = _dot(nq, wq_ref[...])
    cq = cq_ref[...].T
    sq = sq_ref[...].T
    q = jnp.concatenate([q2[:, h * LANES:(h + 1) * LANES] * cq + q2[:, hw + h * LANES:hw + (h + 1) * LANES] * sq
                         for h in range(MLA_HEADS)], axis=1)
    qt_ref[...] = q.T.astype(BF16)
    nkv = _rms(p[:, MLA_Q_RANK:MLA_Q_RANK + MLA_KV_RANK], kvg_ref[...]).astype(BF16)
    kv = _dot(nkv, wkv_ref[...])
    c0 = MLA_Q_RANK + MLA_KV_RANK
    r32 = r32_ref[...].T
    k_pe = (p[:, c0:c0 + MLA_ROPE] * r32[:, :MLA_ROPE]
            + p[:, c0 + MLA_ROPE:c0 + 2 * MLA_ROPE] * r32[:, MLA_ROPE:2 * MLA_ROPE])
    k_ref[...] = (kv[:, :hw] + _dot(k_pe.astype(BF16), place_ref[...])).astype(BF16)
    vt_ref[...] = kv[:, hw:].T.astype(BF16)


def _moba_prep(p, cm_ref, sm_ref, qt_ref, k_ref, vt_ref, km_ref, si, tm):
    W = GROUP_WIDTH
    HD = MOBA_HD
    nb = km_ref.shape[0]
    cm = cm_ref[...].T
    sm = sm_ref[...].T
    half = MOBA_ROT // 2
    first_half = (lax.broadcasted_iota(jnp.int32, (tm, LANES), 1) % HD) < half

    def rope(x):
        out = []
        for pr in range(W // LANES):
            xp = x[:, pr * LANES:(pr + 1) * LANES]
            partner = jnp.where(first_half, pltpu.roll(xp, LANES - half, 1), pltpu.roll(xp, half, 1))
            out.append(xp * cm + partner * sm)
        return jnp.concatenate(out, axis=1)

    q = rope(p[:, 0:W])
    k = rope(p[:, W:2 * W])
    vt_ref[...] = p[:, 2 * W:3 * W].T.astype(BF16)
    nblk = tm // MOBA_BLOCK
    for j in range(nblk):
        mean = jnp.sum(k[j * MOBA_BLOCK:(j + 1) * MOBA_BLOCK, :], axis=0, keepdims=True) * (1.0 / MOBA_BLOCK)
        km_ref[pl.ds(si * nblk + j, 1), :] = mean
    km = km_ref[...]
    n_idx = lax.broadcasted_iota(jnp.int32, (nb, tm), 0)
    blk = (si * tm + lax.broadcasted_iota(jnp.int32, (nb, tm), 1)) // MOBA_BLOCK
    past = n_idx < blk
    own = n_idx == blk
    n_f = n_idx.astype(F32)
    lane_m = lax.broadcasted_iota(jnp.int32, (nb, W), 1)
    km_heads = jnp.concatenate([jnp.where((lane_m // HD) == h, km, 0.0) for h in range(MOBA_HEADS)], axis=0)
    m_hi, m_mid, m_lo = _split3(km_heads)
    q_hi, q_mid, q_lo = _split3(q)
    nm = MOBA_HEADS * nb
    g_hi = _dot_nt(jnp.concatenate([m_hi, m_mid, m_lo], axis=0), q_hi)
    g_mid = _dot_nt(jnp.concatenate([m_hi, m_mid], axis=0), q_mid)
    g_lo = _dot_nt(m_hi, q_lo)
    gates = (((g_lo + g_hi[2 * nm:]) + g_mid[nm:]) + (g_mid[:nm] + g_hi[nm:2 * nm])) + g_hi[:nm]
    masks = []
    for h in range(MOBA_HEADS):
        gate = jnp.where(past, gates[h * nb:(h + 1) * nb], -jnp.inf)
        keep = own
        for _ in range(MOBA_TOPK):
            mx = jnp.max(gate, axis=0, keepdims=True)
            first = jnp.min(jnp.where(gate == mx, n_f, float(LANES)), axis=0, keepdims=True)
            pick = (n_f == first) & (mx > -jnp.inf)
            keep = keep | pick
            gate = jnp.where(pick, -jnp.inf, gate)
        masks.append(jnp.where(keep, 0.0, NEG_BIG))
        if nb < MOBA_SLOT:
            masks.append(jnp.zeros((MOBA_SLOT - nb, tm), F32))
    mask_hi = jnp.concatenate(masks, axis=0).T
    mask_lo = pltpu.roll(mask_hi, HD, 1)
    lane = lax.broadcasted_iota(jnp.int32, (tm, LANES), 1)
    head_lanes = lane < HD
    q_scaled = q * (MOBA_HD ** -0.5 * LOG2E)
    own_blk = (si * tm + lax.broadcasted_iota(jnp.int32, (tm, LANES), 0)) // MOBA_BLOCK
    kone = [(lane == own_blk + HD + slot * MOBA_SLOT).astype(F32) for slot in range(2)]
    q_ext = []
    for h in range(MOBA_HEADS):
        pr = h // 2
        q_pair = q_scaled[:, pr * LANES:(pr + 1) * LANES]
        k_pair = k[:, pr * LANES:(pr + 1) * LANES]
        if h % 2:
            q_pair = pltpu.roll(q_pair, HD, 1)
            k_pair = pltpu.roll(k_pair, HD, 1)
        q_ext.append(jnp.where(head_lanes, q_pair, mask_lo if h < 2 else mask_hi))
        k_ref[:, h * LANES:(h + 1) * LANES] = jnp.where(
            head_lanes, k_pair, kone[h % 2]).astype(BF16)
    qt_ref[...] = jnp.concatenate(q_ext, axis=1).T.astype(BF16)


def _gla(p, wa_ref, ba_ref, hn_ref, o_ref, state_ref, tm):
    C = GLA_CHUNK
    hk = GLA_HEADS * GLA_DK
    hv = GLA_HEADS * GLA_DV
    nc = tm // C
    rc = lax.broadcasted_iota(jnp.int32, (tm, hk), 0) % C
    lane_k = lax.broadcasted_iota(jnp.int32, (C, hk), 1)
    lane_v = lax.broadcasted_iota(jnp.int32, (C, hv), 1)
    row4 = lax.broadcasted_iota(jnp.int32, (GLA_HEADS * C, C), 0)
    col4 = lax.broadcasted_iota(jnp.int32, (GLA_HEADS * C, C), 1)
    causal4 = (row4 % C) >= col4
    sr = lax.broadcasted_iota(jnp.int32, (hv, hk), 0)
    sc = lax.broadcasted_iota(jnp.int32, (hv, hk), 1)
    blockdiag = (sr // GLA_DV) == (sc // GLA_DK)
    scale = GLA_DK ** -0.5

    q = p[:, 0:hk]
    k = p[:, hk:2 * hk]
    vb = p[:, 2 * hk:2 * hk + hv].astype(BF16)
    a_low = p[:, 2 * hk + 2 * hv:2 * hk + 2 * hv + LANES]
    a_lin = _dot(a_low.astype(BF16), wa_ref[...]) + ba_ref[...]
    b = (jnp.minimum(a_lin, 0.0) - jnp.log(1.0 + jnp.exp(-jnp.abs(a_lin)))) * (1.0 / GLA_TAU)
    for s in (1, 2, 4, 8, 16, 32):
        b = b + jnp.where(rc >= s, pltpu.roll(b, s, 0), 0.0)
    last = [b[c * C + C - 1:(c + 1) * C, :] for c in range(nc)]
    bl = jnp.concatenate([jnp.broadcast_to(r, (C, hk)) for r in last], axis=0)
    bref = 0.5 * bl
    qf = q * scale
    qs = qf * jnp.exp(b - bref)
    ks = (k * jnp.exp(bref - b)).astype(BF16)
    kd = (k * jnp.exp(bl - b)).astype(BF16)
    qe = (qf * jnp.exp(b)).astype(BF16)

    intra, incr = [], []
    for c in range(nc):
        sl = slice(c * C, (c + 1) * C)
        qs4 = jnp.concatenate(
            [jnp.where((lane_k // GLA_DK) == h, qs[sl], 0.0) for h in range(GLA_HEADS)], axis=0).astype(BF16)
        att = jnp.where(causal4, _dot_nt(qs4, ks[sl]), 0.0)
        res = _dot(att.astype(BF16), vb[sl])
        o = jnp.where((lane_v // GLA_DV) == 0, res[0:C, :], 0.0)
        for h in range(1, GLA_HEADS):
            o = o + jnp.where((lane_v // GLA_DV) == h, res[h * C:(h + 1) * C, :], 0.0)
        intra.append(o)
        incr.append(jnp.where(blockdiag, _dot_tn(vb[sl], kd[sl]), 0.0))

    st = state_ref[...]
    outs = []
    for c in range(nc):
        outs.append(intra[c] + _dot_nt(qe[c * C:(c + 1) * C], st.astype(BF16)))
        st = st * jnp.exp(last[c]) + incr[c]
    state_ref[...] = st
    o = jnp.concatenate(outs, axis=0)

    lane_t = lax.broadcasted_iota(jnp.int32, (tm, hv), 1)
    osq = o * o
    inv = jnp.zeros_like(o)
    for h in range(GLA_HEADS):
        mh = (lane_t // GLA_DV) == h
        ms = jnp.sum(jnp.where(mh, osq, 0.0), axis=-1, keepdims=True) * (1.0 / GLA_DV)
        inv = jnp.where(mh, lax.rsqrt(ms + NORM_EPS), inv)
    g = p[:, 2 * hk + hv:2 * hk + 2 * hv]
    o_ref[...] = (o * inv * hn_ref[...] * (g * _sigmoid(g))).astype(o_ref.dtype)


def _lru(p, cw_ref, cb_ref, wax_ref, bax_ref, lam_ref, o_ref, xs_ref, h_ref, tm):
    W = LRU_WIDTH
    G = SUBLANES
    x = p[:, 0:W]
    gate = p[:, W:2 * W]
    xs_ref[G:, :] = x
    cw = cw_ref[...]
    xc = cb_ref[...] + cw[LRU_CONV - 1:LRU_CONV, :] * x
    for d in range(1, LRU_CONV):
        xc = xc + cw[LRU_CONV - 1 - d:LRU_CONV - d, :] * xs_ref[G - d:G - d + tm, :]
    xs_ref[:G, :] = x[tm - G:, :]
    ri = _dot(xc.astype(BF16), wax_ref[...]) + bax_ref[...]
    r = _sigmoid(ri[:, :W])
    ig = _sigmoid(ri[:, W:])
    lam = lam_ref[...]
    softplus_neg = jnp.maximum(-lam, 0.0) + jnp.log1p(jnp.exp(-jnp.abs(lam)))
    log_a = (-LRU_C * r) * softplus_neg
    a = jnp.exp(log_a)
    u = jnp.sqrt(-jnp.tanh(log_a) * (a * a + 1.0)) * (ig * xc)
    sub = lax.broadcasted_iota(jnp.int32, (G, W), 0)
    h = h_ref[...]
    outs = []
    for g in range(tm // G):
        ag = a[g * G:(g + 1) * G, :]
        ug = u[g * G:(g + 1) * G, :]
        for s in (1, 2, 4):
            live = sub >= s
            ug = ug + ag * jnp.where(live, pltpu.roll(ug, s, 0), 0.0)
            ag = ag * jnp.where(live, pltpu.roll(ag, s, 0), 1.0)
        hg = ug + ag * h
        outs.append(hg)
        h = hg[G - 1:G, :]
    h_ref[...] = h
    hs = jnp.concatenate(outs, axis=0)
    gelu = 0.5 * gate * (1.0 + jnp.tanh(0.7978845608028654 * (gate + 0.044715 * gate * gate * gate)))
    o_ref[...] = (hs * gelu).astype(o_ref.dtype)


def _w_in_plan():
    sp = np.cumsum([0, 256, 128, 32, 128, 128, 256, 16, 256, 256, 256, 256, 256, 256])
    mq, mkv, mkr, gq, gk, gv, ga, gg, bq, bk, bv, rx, rg = [(int(sp[n]), int(sp[n + 1] - sp[n])) for n in range(13)]
    keep = lambda c: [(c[1], c[0], 1.0)]
    zeros = lambda n: [(n, None, 0.0)]

    def swapped(start, half):
        return [(half, start + half, -1.0), (half, start, 1.0)]

    plan = (keep(mq) + keep(mkv) + keep(mkr) + swapped(mkr[0], MLA_ROPE // 2) + zeros(W_MLA - 256 - 128 - 64)
            + keep(gq) + keep(gk) + keep(gv) + keep(gg) + keep(ga) + zeros(LANES - GLA_GATE_RANK)
            + keep(bq) + keep(bk) + keep(bv)
            + keep(rx) + keep(rg))
    assert sum(p[0] for p in plan) == W_IN
    return plan, int(sp[-1])


def _build_w_in(w_ref, o_ref, rows_per_step=256):
    plan, n_src = _w_in_plan()
    blocks, cur, room = [], [], LANES
    for width, src, sign in plan:
        while width:
            take = min(width, room)
            cur.append((take, src, sign))
            src = None if src is None else src + take
            width -= take
            room -= take
            if room == 0:
                blocks.append(cur)
                cur, room = [], LANES
    for r0 in range(0, w_ref.shape[0], rows_per_step):
        rows = slice(r0, r0 + rows_per_step)
        for j, frags in enumerate(blocks):
            vals = []
            for width, src, sign in frags:
                if src is None:
                    vals.append(jnp.zeros((rows_per_step, width), F32))
                    continue
                a0 = src // LANES * LANES
                a1 = min(-(-(src + width) // LANES) * LANES, n_src)
                win = w_ref[rows, a0:a1]
                v = win[:, src - a0:src - a0 + width]
                vals.append(-v if sign < 0 else v)
            blk = vals[0] if len(vals) == 1 else jnp.concatenate(vals, axis=1)
            o_ref[rows, j * LANES:(j + 1) * LANES] = blk.astype(BF16)


def _prep_kernel(x_ref, g_ref, w_raw_ref,
                 qg_ref, wq_ref, kvg_ref, wkv_ref, place_ref, cq_ref, sq_ref, r32_ref,
                 cm_ref, sm_ref,
                 wa_ref, ba_ref, hn_ref,
                 cw_ref, cb_ref, wax_ref, bax_ref, lam_ref,
                 mq_ref, mk_ref, mv_ref, bq_ref, bk_ref, bv_ref, yb_ref, yd_ref,
                 w_ref, km_ref, state_ref, xs_ref, h_ref, *, tm):
    si = pl.program_id(1)

    @pl.when((pl.program_id(0) == 0) & (si == 0))
    def _():
        _build_w_in(w_raw_ref, w_ref)

    @pl.when(si == 0)
    def _():
        km_ref[...] = jnp.zeros_like(km_ref)
        state_ref[...] = jnp.zeros_like(state_ref)
        xs_ref[:SUBLANES, :] = jnp.zeros((SUBLANES, LRU_WIDTH), F32)
        h_ref[...] = jnp.zeros_like(h_ref)

    h = _rms(x_ref[...], g_ref[...]).astype(BF16)
    o_mla, o_gla, o_moba, o_lru = 0, W_MLA, W_MLA + W_GLA, W_MLA + W_GLA + W_MOBA
    p_mla = _dot(h, w_ref[:, o_mla:o_mla + W_MLA])
    p_moba = _dot(h, w_ref[:, o_moba:o_moba + W_MOBA])
    _mla_prep(p_mla, qg_ref, wq_ref, kvg_ref, wkv_ref, place_ref, cq_ref, sq_ref, r32_ref, mq_ref, mk_ref, mv_ref)
    p_lru = _dot(h, w_ref[:, o_lru:o_lru + W_LRU])
    _moba_prep(p_moba, cm_ref, sm_ref, bq_ref, bk_ref, bv_ref, km_ref, si, tm)
    p_gla = _dot(h, w_ref[:, o_gla:o_gla + W_GLA])
    _lru(p_lru, cw_ref, cb_ref, wax_ref, bax_ref, lam_ref, yd_ref, xs_ref, h_ref, tm)
    _gla(p_gla, wa_ref, ba_ref, hn_ref, yb_ref, state_ref, tm)


def _prep(x, w_in, l, consts, tables, B, S, tm):
    T = x.shape[0]
    ns = S // tm
    hw = ATTN_HEADS * LANES
    tok = lambda b, s: (b * ns + s, 0)
    pos = lambda b, s: (s, 0)
    g, qg, wq, kvg, wkv, place, wa, ba, hn, cw, cb, wax, bax, lam = consts
    cq, sq, r32, cm, sm = tables
    w_spec = pl.BlockSpec((None,) + w_in.shape[1:], lambda b, s: (l, 0, 0))
    c = lambda a: _layer_spec(a, l) if a.ndim == 3 else _const_spec(a.shape)
    t = lambda a: pl.BlockSpec((a.shape[0], tm), lambda b, s: (0, s))
    nb = max(2 * SUBLANES, -(-(S // MOBA_BLOCK) // SUBLANES) * SUBLANES)
    tok_t = lambda b, s: (0, b * ns + s)
    outs = [(hw, True), (hw, False), (GROUP_WIDTH, True), (hw, True), (hw, False), (GROUP_WIDTH, True),
            (GROUP_WIDTH, False), (GROUP_WIDTH, False)]
    return pl.pallas_call(
        functools.partial(_prep_kernel, tm=tm),
        grid=(B, ns),
        in_specs=[pl.BlockSpec((tm, D_MODEL), tok), c(g), w_spec,
                  c(qg), c(wq), c(kvg), c(wkv), c(place), t(cq), t(sq), t(r32),
                  t(cm), t(sm),
                  c(wa), c(ba), c(hn),
                  c(cw), c(cb), c(wax), c(bax), c(lam)],
        out_specs=[pl.BlockSpec((wd, tm), tok_t) if tr else pl.BlockSpec((tm, wd), tok) for wd, tr in outs],
        out_shape=[jax.ShapeDtypeStruct((wd, T) if tr else (T, wd), BF16) for wd, tr in outs],
        scratch_shapes=[pltpu.VMEM((D_MODEL, W_IN), BF16),
                        pltpu.VMEM((nb, GROUP_WIDTH), F32),
                        pltpu.VMEM((GLA_HEADS * GLA_DV, GLA_HEADS * GLA_DK), F32),
                        pltpu.VMEM((SUBLANES + tm, LRU_WIDTH), F32), pltpu.VMEM((1, LRU_WIDTH), F32)],
        compiler_params=_params(("arbitrary", "arbitrary")),
        name="mixer_prep",
    )(x, g, w_in, qg, wq, kvg, wkv, place, cq, sq, r32, cm, sm, wa, ba, hn, cw, cb, wax, bax, lam)


HEAD_V = 64
ACC_ROWS = HEAD_V + 16


def _attn_kernel(qt_ref, k_ref, vt_ref, o_ref, m_ref, acc_ref, *, tq, tk):
    i = pl.program_id(1)
    m_ref[...] = jnp.full(m_ref.shape, NEG_BIG, F32)
    acc_ref[...] = jnp.zeros(acc_ref.shape, F32)
    ones = jnp.ones((ACC_ROWS - HEAD_V, tk), BF16)

    def score(j, h, q0=0):
        keys = pl.ds(pl.multiple_of(j * tk, tk), tk)
        return _dot(k_ref[keys, h * LANES:(h + 1) * LANES], qt_ref[h * LANES:(h + 1) * LANES, q0:])

    def consume(j, scores, diagonal, q0=0, ahead=None):
        keys = pl.ds(pl.multiple_of(j * tk, tk), tk)
        nq = tq - q0
        if diagonal:
            kpos = j * tk + lax.broadcasted_iota(jnp.int32, (tk, nq), 0)
            qpos = i * tq + q0 + lax.broadcasted_iota(jnp.int32, (tk, nq), 1)
            allowed = kpos <= qpos
        nxt = []
        for h in range(ATTN_HEADS):
            if ahead is not None:
                nxt.append(ahead(h))
            vt = jnp.concatenate([vt_ref[h * HEAD_V:(h + 1) * HEAD_V, keys], ones], axis=0)
            s = scores[h]
            if diagonal:
                s = jnp.where(allowed, s, NEG_BIG)
            m_prev = m_ref[h, :, q0:]
            m_next = jnp.maximum(m_prev, jnp.max(s, axis=0, keepdims=True))
            alpha = jnp.exp2(m_prev - m_next)
            p = jnp.exp2(s - jnp.tile(m_next, (tk // SUBLANES, 1)))
            acc_ref[h, :, q0:] = (jnp.tile(alpha, (ACC_ROWS // SUBLANES, 1)) * acc_ref[h, :, q0:]
                                  + _dot(vt, p.astype(BF16)))
            m_ref[h, :, q0:] = m_next
        return nxt

    n_full = (i * tq) // tk

    def body(j, _):
        consume(j, [score(j, h) for h in range(ATTN_HEADS)], False)
        return 0

    lax.fori_loop(0, n_full, body, 0)
    n_diag = tq // tk
    pending = [score(n_full, h) for h in range(ATTN_HEADS)]
    for d in range(n_diag):
        ahead = None
        if d + 1 < n_diag:
            ahead = functools.partial(score, n_full + d + 1, q0=(d + 1) * tk)
        pending = consume(n_full + d, pending, True, d * tk, ahead)
    out_t = jnp.concatenate(
        [acc_ref[h, :HEAD_V] * jnp.tile(1.0 / acc_ref[h, HEAD_V:HEAD_V + SUBLANES], (HEAD_V // SUBLANES, 1))
         for h in range(ATTN_HEADS)], axis=0)
    o_ref[...] = out_t.T.astype(o_ref.dtype)


def _attn(qt, k, vt, B, S, tq, tk, name):
    T = k.shape[0]
    nq = S // tq
    hw = ATTN_HEADS * LANES
    stat = pltpu.VMEM((ATTN_HEADS, SUBLANES, tq), F32)
    return pl.pallas_call(
        functools.partial(_attn_kernel, tq=tq, tk=tk),
        grid=(B, nq),
        in_specs=[pl.BlockSpec((hw, tq), lambda b, i: (0, b * nq + i)),
                  pl.BlockSpec((S, hw), lambda b, i: (b, 0)),
                  pl.BlockSpec((GROUP_WIDTH, S), lambda b, i: (0, b))],
        out_specs=pl.BlockSpec((tq, GROUP_WIDTH), lambda b, i: (b * nq + i, 0)),
        out_shape=jax.ShapeDtypeStruct((T, GROUP_WIDTH), BF16),
        scratch_shapes=[stat, pltpu.VMEM((ATTN_HEADS, ACC_ROWS, tq), F32)],
        compiler_params=_params(("parallel", "arbitrary")),
        name=name,
    )(qt, k, vt)


def _ffn_kernel(x_ref, ya_ref, yb_ref, yc_ref, yd_ref, wo_ref, fg_ref, wi_ref, w2_ref, ng_ref, o_ref,
                *, final, chunk):
    W = GROUP_WIDTH
    x1 = x_ref[...]
    for n, y_ref in enumerate((ya_ref, yb_ref, yc_ref, yd_ref)):
        x1 = x1 + _dot(y_ref[...], wo_ref[n * W:(n + 1) * W, :])
    h = _rms(x1, fg_ref[...]).astype(BF16)
    parts = []
    for c in range(FFN_HIDDEN // chunk):
        g = _dot(h, wi_ref[:, c * chunk:(c + 1) * chunk])
        up = _dot(h, wi_ref[:, FFN_HIDDEN + c * chunk:FFN_HIDDEN + (c + 1) * chunk])
        act = (g * _sigmoid(g) * up).astype(BF16)
        parts.append(_dot(act, w2_ref[c * chunk:(c + 1) * chunk, :]))
    acc = x1 + sum(parts[1:], parts[0])
    if final:
        acc = _rms(acc, ng_ref[...])
    o_ref[...] = acc


def _layer_spec(stacked, l):
    nd = stacked.ndim - 1
    return pl.BlockSpec((None,) + stacked.shape[1:], lambda *_: (l,) + (0,) * nd, pipeline_mode=pl.Buffered(1))


def _ffn(x, ya, yb, yc, yd, wo, fg, wi, w2, ng, l, final, tm, chunk):
    T = x.shape[0]
    tok = lambda i: (i, 0)
    ysp = pl.BlockSpec((tm, GROUP_WIDTH), tok)
    return pl.pallas_call(
        functools.partial(_ffn_kernel, final=final, chunk=chunk),
        grid=(T // tm,),
        in_specs=[pl.BlockSpec((tm, D_MODEL), tok), ysp, ysp, ysp, ysp,
                  _layer_spec(wo, l), _layer_spec(fg, l), _layer_spec(wi, l),
                  _layer_spec(w2, l), _const_spec(ng.shape)],
        out_specs=pl.BlockSpec((tm, D_MODEL), tok),
        out_shape=jax.ShapeDtypeStruct((T, D_MODEL), F32),
        compiler_params=_params(("parallel",)),
        name="outproj_ffn",
    )(x, ya, yb, yc, yd, wo, fg, wi, w2, ng)


def _swap_cols(w, half):
    return jnp.concatenate([-w[..., half:2 * half], w[..., :half]], axis=-1)


def _layout_mla(w_uq, w_ukv):
    zq = jnp.zeros(w_uq.shape[:-1] + (LANES - MLA_NOPE - MLA_ROPE,), w_uq.dtype)
    zr = jnp.zeros(w_uq.shape[:-1] + (MLA_NOPE,), w_uq.dtype)
    qd = MLA_NOPE + MLA_ROPE
    plain, swapped = [], []
    for h in range(MLA_HEADS):
        nope = w_uq[..., h * qd:h * qd + MLA_NOPE]
        rope = w_uq[..., h * qd + MLA_NOPE:(h + 1) * qd]
        plain += [nope, rope, zq]
        swapped += [zr, _swap_cols(rope, MLA_ROPE // 2), zq]
    wq = jnp.concatenate(plain + swapped, axis=-1).astype(BF16)
    zk = jnp.zeros(w_ukv.shape[:-1] + (LANES - MLA_NOPE,), w_ukv.dtype)
    kd = MLA_NOPE + MLA_V
    kparts = []
    vparts = []
    for h in range(MLA_HEADS):
        kparts += [w_ukv[..., h * kd:h * kd + MLA_NOPE], zk]
        vparts += [w_ukv[..., h * kd + MLA_NOPE:(h + 1) * kd]]
    wkv = jnp.concatenate(kparts + vparts, axis=-1).astype(BF16)
    place = np.zeros((MLA_ROPE, MLA_HEADS * LANES), np.float32)
    for h in range(MLA_HEADS):
        place[np.arange(MLA_ROPE), h * LANES + MLA_NOPE + np.arange(MLA_ROPE)] = 1.0
    return wq, wkv, jnp.asarray(place, BF16)


def _tables(S):
    pos = jnp.arange(S, dtype=F32)[None, :]

    def cs(dim):
        inv_freq = ROPE_THETA ** (-jnp.arange(0, dim, 2, dtype=F32) / dim)
        ang = inv_freq[:, None] * pos
        return jnp.cos(ang), jnp.sin(ang)

    c, s = cs(MLA_ROPE)
    scale = (MLA_NOPE + MLA_ROPE) ** -0.5 * LOG2E
    one = jnp.ones((MLA_NOPE, S), F32)
    zero = jnp.zeros((MLA_NOPE, S), F32)
    pad1 = jnp.ones((LANES - MLA_NOPE - MLA_ROPE, S), F32)
    pad0 = jnp.zeros((LANES - MLA_NOPE - MLA_ROPE, S), F32)
    cq = jnp.concatenate([one, c, c, pad1], axis=0) * scale
    sq = jnp.concatenate([zero, s, s, pad0], axis=0) * scale
    r32 = jnp.concatenate([c, c, s, s, zero], axis=0)
    c, s = cs(MOBA_ROT)
    one = jnp.ones((MOBA_HD - MOBA_ROT, S), F32)
    zero = jnp.zeros((MOBA_HD - MOBA_ROT, S), F32)
    cm = jnp.concatenate([c, c, one] * 2, axis=0)
    sm = jnp.concatenate([-s, s, zero] * 2, axis=0)
    return cq, sq, r32, cm, sm


def _block_diag(w):
    depth, n, c, d = w.shape
    rows = []
    for j in range(n):
        rows.append(jnp.concatenate([jnp.zeros((depth, c, j * d), w.dtype), w[:, j],
                                     jnp.zeros((depth, c, (n - 1 - j) * d), w.dtype)], axis=-1))
    return jnp.concatenate(rows, axis=1)


def _stacked_consts(attn_norm, mla_q_norm, mla_w_uq, mla_kv_norm, mla_w_ukv, gla_w_a2, gla_b_a2,
                    gla_head_norm, lru_conv_w, lru_conv_b, lru_w_a, lru_b_a, lru_w_x, lru_b_x, lru_lambda):
    depth = attn_norm.shape[0]
    r3 = lambda v: v.reshape(depth, 1, -1).astype(F32)
    wq, wkv, place = _layout_mla(mla_w_uq, mla_w_ukv)
    wa = jnp.pad(gla_w_a2, ((0, 0), (0, LANES - GLA_GATE_RANK), (0, 0))).astype(BF16)
    hn = jnp.tile(r3(gla_head_norm), (1, 1, GLA_HEADS))
    wax = jnp.concatenate([_block_diag(lru_w_a), _block_diag(lru_w_x)], axis=-1).astype(BF16)
    bax = jnp.concatenate([r3(lru_b_a), r3(lru_b_x)], axis=-1)
    return (r3(attn_norm), r3(mla_q_norm), wq, r3(mla_kv_norm), wkv, place,
            wa, r3(gla_b_a2), hn, lru_conv_w.reshape(depth, LRU_CONV, LRU_WIDTH), r3(lru_conv_b), wax, bax,
            r3(lru_lambda))


def kernel(x, attn_norm, w_in, mla_q_norm, mla_w_uq, mla_kv_norm, mla_w_ukv, gla_w_a2, gla_b_a2, gla_head_norm,
           lru_conv_w, lru_conv_b, lru_w_a, lru_b_a, lru_w_x, lru_b_x, lru_lambda, w_out, ffn_norm, w_ffn_in,
           w_ffn_out, final_norm):
    B, S, D = x.shape
    depth = w_in.shape[0]
    T = B * S
    tq = min(S, ATTN_TQ)
    assert D == D_MODEL and S % max(PREP_TM, tq, FFN_TM) == 0 and S // MOBA_BLOCK <= MOBA_SLOT
    tables = _tables(S)
    row = lambda v: v.reshape(1, -1).astype(F32)
    xt = x.reshape(T, D)
    wo, wi, w2 = w_out.astype(BF16), w_ffn_in.astype(BF16), w_ffn_out.astype(BF16)
    fg = ffn_norm.reshape(depth, 1, D).astype(F32)
    consts = _stacked_consts(attn_norm, mla_q_norm, mla_w_uq, mla_kv_norm, mla_w_ukv, gla_w_a2, gla_b_a2,
                             gla_head_norm, lru_conv_w, lru_conv_b, lru_w_a, lru_b_a, lru_w_x, lru_b_x, lru_lambda)
    for l in range(depth):
        mq, mk, mv, bq, bk, bv, y_b, y_d = _prep(xt, w_in, l, consts, tables, B, S, PREP_TM)
        y_a = _attn(mq, mk, mv, B, S, tq, ATTN_TK, "mla_attn")
        y_c = _attn(bq, bk, bv, B, S, tq, ATTN_TK, "moba_attn")
        xt = _ffn(xt, y_a, y_b, y_c, y_d, wo, fg, wi, w2, row(final_norm), l, l == depth - 1,
                  FFN_TM, FFN_HIDDEN // 2)
    return xt.reshape(B, S, D)
```

```python
import functools

import numpy as np
import jax
import jax.numpy as jnp
from jax import lax
from jax.experimental import pallas as pl
from jax.experimental.pallas import tpu as pltpu

F32 = jnp.float32
BF16 = jnp.bfloat16

D_MODEL = 1024
GROUP_WIDTH = 256
ROPE_THETA = 500000.0
NORM_EPS = 1e-6
MLA_HEADS = 4
MLA_V = 64
MLA_NOPE = 64
MLA_ROPE = 32
MLA_Q_RANK = 256
MLA_KV_RANK = 128
GLA_HEADS = 4
GLA_DV = 64
GLA_DK = 32
GLA_GATE_RANK = 16
GLA_TAU = 16.0
GLA_CHUNK = 64
MOBA_HEADS = 4
MOBA_HD = 64
MOBA_ROT = 16
MOBA_BLOCK = 256
MOBA_TOPK = 3
LRU_WIDTH = 256
LRU_CONV = 4
LRU_C = 8.0
FFN_HIDDEN = 2816

LANES = 128
SUBLANES = 8
VMEM_LIMIT = 56 * 1024 * 1024

W_MLA = 512
W_GLA = 896
W_MOBA = 768
W_LRU = 512
W_IN = W_MLA + W_GLA + W_MOBA + W_LRU
NEG_BIG = -1e30
LOG2E = 1.4426950408889634
MOBA_SLOT = 32
PREP_TM = 512
FFN_TM = 1024
ATTN_HEADS = 4
ATTN_TQ = 4096
ATTN_TK = 256


def _rms(x, g):
    return x * lax.rsqrt(jnp.mean(x * x, axis=-1, keepdims=True) + NORM_EPS) * g


def _dot(a, b):
    return jnp.dot(a, b, preferred_element_type=F32)


def _dot_nt(a, b):
    return lax.dot_general(a, b, (((1,), (1,)), ((), ())), preferred_element_type=F32)


def _dot_tn(a, b):
    return lax.dot_general(a, b, (((0,), (0,)), ((), ())), preferred_element_type=F32)


def _split3(x):
    hi = x.astype(BF16)
    r = x - hi.astype(F32)
    mid = r.astype(BF16)
    lo = (r - mid.astype(F32)).astype(BF16)
    return hi, mid, lo


def _sigmoid(x):
    return 0.5 * jnp.tanh(0.5 * x) + 0.5


def _const_spec(shape):
    nd = len(shape)
    return pl.BlockSpec(shape, lambda *_: (0,) * nd, pipeline_mode=pl.Buffered(1))


def _params(sem):
    return pltpu.CompilerParams(dimension_semantics=sem, vmem_limit_bytes=VMEM_LIMIT)


def _mla_prep(p, qg_ref, wq_ref, kvg_ref, wkv_ref, place_ref, cq_ref, sq_ref, r32_ref, qt_ref, k_ref, vt_ref):
    hw = MLA_HEADS * LANES
    nq = _rms(p[:, :MLA_Q_RANK], qg_ref[...]).astype(BF16)
    q2 = _dot(nq, wq_ref[...])
    cq = cq_ref[...].T
    sq = sq_ref[...].T
    q = jnp.concatenate([q2[:, h * LANES:(h + 1) * LANES] * cq + q2[:, hw + h * LANES:hw + (h + 1) * LANES] * sq
                         for h in range(MLA_HEADS)], axis=1)
    qt_ref[...] = q.T.astype(BF16)
    nkv = _rms(p[:, MLA_Q_RANK:MLA_Q_RANK + MLA_KV_RANK], kvg_ref[...]).astype(BF16)
    kv = _dot(nkv, wkv_ref[...])
    c0 = MLA_Q_RANK + MLA_KV_RANK
    r32 = r32_ref[...].T
    k_pe = (p[:, c0:c0 + MLA_ROPE] * r32[:, :MLA_ROPE]
            + p[:, c0 + MLA_ROPE:c0 + 2 * MLA_ROPE] * r32[:, MLA_ROPE:2 * MLA_ROPE])
    k_ref[...] = (kv[:, :hw] + _dot(k_pe.astype(BF16), place_ref[...])).astype(BF16)
    vt_ref[...] = kv[:, hw:].T.astype(BF16)


def _moba_prep(p, cm_ref, sm_ref, qt_ref, k_ref, vt_ref, km_ref, si, tm):
    W = GROUP_WIDTH
    HD = MOBA_HD
    nb = km_ref.shape[0]
    cm = cm_ref[...].T
    sm = sm_ref[...].T
    half = MOBA_ROT // 2
    first_half = (lax.broadcasted_iota(jnp.int32, (tm, LANES), 1) % HD) < half

    def rope(x):
        out = []
        for pr in range(W // LANES):
            xp = x[:, pr * LANES:(pr + 1) * LANES]
            partner = jnp.where(first_half, pltpu.roll(xp, LANES - half, 1), pltpu.roll(xp, half, 1))
            out.append(xp * cm + partner * sm)
        return jnp.concatenate(out, axis=1)

    q = rope(p[:, 0:W])
    k = rope(p[:, W:2 * W])
    vt_ref[...] = p[:, 2 * W:3 * W].T.astype(BF16)
    nblk = tm // MOBA_BLOCK
    for j in range(nblk):
        mean = jnp.sum(k[j * MOBA_BLOCK:(j + 1) * MOBA_BLOCK, :], axis=0, keepdims=True) * (1.0 / MOBA_BLOCK)
        km_ref[pl.ds(si * nblk + j, 1), :] = mean
    km = km_ref[...]
    n_idx = lax.broadcasted_iota(jnp.int32, (nb, tm), 0)
    blk = (si * tm + lax.broadcasted_iota(jnp.int32, (nb, tm), 1)) // MOBA_BLOCK
    past = n_idx < blk
    own = n_idx == blk
    n_f = n_idx.astype(F32)
    lane_m = lax.broadcasted_iota(jnp.int32, (nb, W), 1)
    km_heads = jnp.concatenate([jnp.where((lane_m // HD) == h, km, 0.0) for h in range(MOBA_HEADS)], axis=0)
    m_hi, m_mid, m_lo = _split3(km_heads)
    q_hi, q_mid, q_lo = _split3(q)
    nm = MOBA_HEADS * nb
    g_hi = _dot_nt(jnp.concatenate([m_hi, m_mid, m_lo], axis=0), q_hi)
    g_mid = _dot_nt(jnp.concatenate([m_hi, m_mid], axis=0), q_mid)
    g_lo = _dot_nt(m_hi, q_lo)
    gates = (((g_lo + g_hi[2 * nm:]) + g_mid[nm:]) + (g_mid[:nm] + g_hi[nm:2 * nm])) + g_hi[:nm]
    masks = []
    for h in range(MOBA_HEADS):
        gate = jnp.where(past, gates[h * nb:(h + 1) * nb], -jnp.inf)
        keep = own
        for _ in range(MOBA_TOPK):
            mx = jnp.max(gate, axis=0, keepdims=True)
            first = jnp.min(jnp.where(gate == mx, n_f, float(LANES)), axis=0, keepdims=True)
            pick = (n_f == first) & (mx > -jnp.inf)
            keep = keep | pick
            gate = jnp.where(pick, -jnp.inf, gate)
        masks.append(jnp.where(keep, 0.0, NEG_BIG))
        if nb < MOBA_SLOT:
            masks.append(jnp.zeros((MOBA_SLOT - nb, tm), F32))
    mask_hi = jnp.concatenate(masks, axis=0).T
    mask_lo = pltpu.roll(mask_hi, HD, 1)
    lane = lax.broadcasted_iota(jnp.int32, (tm, LANES), 1)
    head_lanes = lane < HD
    q_scaled = q * (MOBA_HD ** -0.5 * LOG2E)
    own_blk = (si * tm + lax.broadcasted_iota(jnp.int32, (tm, LANES), 0)) // MOBA_BLOCK
    kone = [(lane == own_blk + HD + slot * MOBA_SLOT).astype(F32) for slot in range(2)]
    q_ext = []
    for h in range(MOBA_HEADS):
        pr = h // 2
        q_pair = q_scaled[:, pr * LANES:(pr + 1) * LANES]
        k_pair = k[:, pr * LANES:(pr + 1) * LANES]
        if h % 2:
            q_pair = pltpu.roll(q_pair, HD, 1)
            k_pair = pltpu.roll(k_pair, HD, 1)
        q_ext.append(jnp.where(head_lanes, q_pair, mask_lo if h < 2 else mask_hi))
        k_ref[:, h * LANES:(h + 1) * LANES] = jnp.where(
            head_lanes, k_pair, kone[h % 2]).astype(BF16)
    qt_ref[...] = jnp.concatenate(q_ext, axis=1).T.astype(BF16)


def _gla(p, wa_ref, ba_ref, hn_ref, o_ref, state_ref, tm):
    C = GLA_CHUNK
    hk = GLA_HEADS * GLA_DK
    hv = GLA_HEADS * GLA_DV
    nc = tm // C
    rc = lax.broadcasted_iota(jnp.int32, (tm, hk), 0) % C
    lane_k = lax.broadcasted_iota(jnp.int32, (C, hk), 1)
    lane_v = lax.broadcasted_iota(jnp.int32, (C, hv), 1)
    row4 = lax.broadcasted_iota(jnp.int32, (GLA_HEADS * C, C), 0)
    col4 = lax.broadcasted_iota(jnp.int32, (GLA_HEADS * C, C), 1)
    causal4 = (row4 % C) >= col4
    sr = lax.broadcasted_iota(jnp.int32, (hv, hk), 0)
    sc = lax.broadcasted_iota(jnp.int32, (hv, hk), 1)
    blockdiag = (sr // GLA_DV) == (sc // GLA_DK)
    scale = GLA_DK ** -0.5

    q = p[:, 0:hk]
    k = p[:, hk:2 * hk]
    vb = p[:, 2 * hk:2 * hk + hv].astype(BF16)
    a_low = p[:, 2 * hk + 2 * hv:2 * hk + 2 * hv + LANES]
    a_lin = _dot(a_low.astype(BF16), wa_ref[...]) + ba_ref[...]
    b = (jnp.minimum(a_lin, 0.0) - jnp.log(1.0 + jnp.exp(-jnp.abs(a_lin)))) * (1.0 / GLA_TAU)
    for s in (1, 2, 4, 8, 16, 32):
        b = b + jnp.where(rc >= s, pltpu.roll(b, s, 0), 0.0)
    last = [b[c * C + C - 1:(c + 1) * C, :] for c in range(nc)]
    bl = jnp.concatenate([jnp.broadcast_to(r, (C, hk)) for r in last], axis=0)
    bref = 0.5 * bl
    qf = q * scale
    qs = qf * jnp.exp(b - bref)
    ks = (k * jnp.exp(bref - b)).astype(BF16)
    kd = (k * jnp.exp(bl - b)).astype(BF16)
    qe = (qf * jnp.exp(b)).astype(BF16)

    intra, incr = [], []
    for c in range(nc):
        sl = slice(c * C, (c + 1) * C)
        qs4 = jnp.concatenate(
            [jnp.where((lane_k // GLA_DK) == h, qs[sl], 0.0) for h in range(GLA_HEADS)], axis=0).astype(BF16)
        att = jnp.where(causal4, _dot_nt(qs4, ks[sl]), 0.0)
        res = _dot(att.astype(BF16), vb[sl])
        o = jnp.where((lane_v // GLA_DV) == 0, res[0:C, :], 0.0)
        for h in range(1, GLA_HEADS):
            o = o + jnp.where((lane_v // GLA_DV) == h, res[h * C:(h + 1) * C, :], 0.0)
        intra.append(o)
        incr.append(jnp.where(blockdiag, _dot_tn(vb[sl], kd[sl]), 0.0))

    st = state_ref[...]
    outs = []
    for c in range(nc):
        outs.append(intra[c] + _dot_nt(qe[c * C:(c + 1) * C], st.astype(BF16)))
        st = st * jnp.exp(last[c]) + incr[c]
    state_ref[...] = st
    o = jnp.concatenate(outs, axis=0)

    lane_t = lax.broadcasted_iota(jnp.int32, (tm, hv), 1)
    osq = o * o
    inv = jnp.zeros_like(o)
    for h in range(GLA_HEADS):
        mh = (lane_t // GLA_DV) == h
        ms = jnp.sum(jnp.where(mh, osq, 0.0), axis=-1, keepdims=True) * (1.0 / GLA_DV)
        inv = jnp.where(mh, lax.rsqrt(ms + NORM_EPS), inv)
    g = p[:, 2 * hk + hv:2 * hk + 2 * hv]
    o_ref[...] = (o * inv * hn_ref[...] * (g * _sigmoid(g))).astype(o_ref.dtype)


def _lru(p, cw_ref, cb_ref, wax_ref, bax_ref, lam_ref, o_ref, xs_ref, h_ref, tm):
    W = LRU_WIDTH
    G = SUBLANES
    x = p[:, 0:W]
    gate = p[:, W:2 * W]
    xs_ref[G:, :] = x
    cw = cw_ref[...]
    xc = cb_ref[...] + cw[LRU_CONV - 1:LRU_CONV, :] * x
    for d in range(1, LRU_CONV):
        xc = xc + cw[LRU_CONV - 1 - d:LRU_CONV - d, :] * xs_ref[G - d:G - d + tm, :]
    xs_ref[:G, :] = x[tm - G:, :]
    ri = _dot(xc.astype(BF16), wax_ref[...]) + bax_ref[...]
    r = _sigmoid(ri[:, :W])
    ig = _sigmoid(ri[:, W:])
    lam = lam_ref[...]
    softplus_neg = jnp.maximum(-lam, 0.0) + jnp.log1p(jnp.exp(-jnp.abs(lam)))
    log_a = (-LRU_C * r) * softplus_neg
    a = jnp.exp(log_a)
    u = jnp.sqrt(-jnp.tanh(log_a) * (a * a + 1.0)) * (ig * xc)
    sub = lax.broadcasted_iota(jnp.int32, (G, W), 0)
    h = h_ref[...]
    outs = []
    for g in range(tm // G):
        ag = a[g * G:(g + 1) * G, :]
        ug = u[g * G:(g + 1) * G, :]
        for s in (1, 2, 4):
            live = sub >= s
            ug = ug + ag * jnp.where(live, pltpu.roll(ug, s, 0), 0.0)
            ag = ag * jnp.where(live, pltpu.roll(ag, s, 0), 1.0)
        hg = ug + ag * h
        outs.append(hg)
        h = hg[G - 1:G, :]
    h_ref[...] = h
    hs = jnp.concatenate(outs, axis=0)
    gelu = 0.5 * gate * (1.0 + jnp.tanh(0.7978845608028654 * (gate + 0.044715 * gate * gate * gate)))
    o_ref[...] = (hs * gelu).astype(o_ref.dtype)


def _w_in_plan():
    sp = np.cumsum([0, 256, 128, 32, 128, 128, 256, 16, 256, 256, 256, 256, 256, 256])
    mq, mkv, mkr, gq, gk, gv, ga, gg, bq, bk, bv, rx, rg = [(int(sp[n]), int(sp[n + 1] - sp[n])) for n in range(13)]
    keep = lambda c: [(c[1], c[0], 1.0)]
    zeros = lambda n: [(n, None, 0.0)]

    def swapped(start, half):
        return [(half, start + half, -1.0), (half, start, 1.0)]

    plan = (keep(mq) + keep(mkv) + keep(mkr) + swapped(mkr[0], MLA_ROPE // 2) + zeros(W_MLA - 256 - 128 - 64)
            + keep(gq) + keep(gk) + keep(gv) + keep(gg) + keep(ga) + zeros(LANES - GLA_GATE_RANK)
            + keep(bq) + keep(bk) + keep(bv)
            + keep(rx) + keep(rg))
    assert sum(p[0] for p in plan) == W_IN
    return plan, int(sp[-1])


def _build_w_in(w_ref, o_ref, rows_per_step=256):
    plan, n_src = _w_in_plan()
    blocks, cur, room = [], [], LANES
    for width, src, sign in plan:
        while width:
            take = min(width, room)
            cur.append((take, src, sign))
            src = None if src is None else src + take
            width -= take
            room -= take
            if room == 0:
                blocks.append(cur)
                cur, room = [], LANES
    for r0 in range(0, w_ref.shape[0], rows_per_step):
        rows = slice(r0, r0 + rows_per_step)
        for j, frags in enumerate(blocks):
            vals = []
            for width, src, sign in frags:
                if src is None:
                    vals.append(jnp.zeros((rows_per_step, width), F32))
                    continue
                a0 = src // LANES * LANES
                a1 = min(-(-(src + width) // LANES) * LANES, n_src)
                win = w_ref[rows, a0:a1]
                v = win[:, src - a0:src - a0 + width]
                vals.append(-v if sign < 0 else v)
            blk = vals[0] if len(vals) == 1 else jnp.concatenate(vals, axis=1)
            o_ref[rows, j * LANES:(j + 1) * LANES] = blk.astype(BF16)


def _prep_kernel(x_ref, g_ref, w_raw_ref,
                 qg_ref, wq_ref, kvg_ref, wkv_ref, place_ref, cq_ref, sq_ref, r32_ref,
                 cm_ref, sm_ref,
                 wa_ref, ba_ref, hn_ref,
                 cw_ref, cb_ref, wax_ref, bax_ref, lam_ref,
                 mq_ref, mk_ref, mv_ref, bq_ref, bk_ref, bv_ref, yb_ref, yd_ref,
                 w_ref, km_ref, state_ref, xs_ref, h_ref, *, tm):
    si = pl.program_id(1)

    @pl.when((pl.program_id(0) == 0) & (si == 0))
    def _():
        _build_w_in(w_raw_ref, w_ref)

    @pl.when(si == 0)
    def _():
        km_ref[...] = jnp.zeros_like(km_ref)
        state_ref[...] = jnp.zeros_like(state_ref)
        xs_ref[:SUBLANES, :] = jnp.zeros((SUBLANES, LRU_WIDTH), F32)
        h_ref[...] = jnp.zeros_like(h_ref)

    h = _rms(x_ref[...], g_ref[...]).astype(BF16)
    o_mla, o_gla, o_moba, o_lru = 0, W_MLA, W_MLA + W_GLA, W_MLA + W_GLA + W_MOBA
    p_mla = _dot(h, w_ref[:, o_mla:o_mla + W_MLA])
    p_moba = _dot(h, w_ref[:, o_moba:o_moba + W_MOBA])
    _mla_prep(p_mla, qg_ref, wq_ref, kvg_ref, wkv_ref, place_ref, cq_ref, sq_ref, r32_ref, mq_ref, mk_ref, mv_ref)
    p_lru = _dot(h, w_ref[:, o_lru:o_lru + W_LRU])
    _moba_prep(p_moba, cm_ref, sm_ref, bq_ref, bk_ref, bv_ref, km_ref, si, tm)
    p_gla = _dot(h, w_ref[:, o_gla:o_gla + W_GLA])
    _lru(p_lru, cw_ref, cb_ref, wax_ref, bax_ref, lam_ref, yd_ref, xs_ref, h_ref, tm)
    _gla(p_gla, wa_ref, ba_ref, hn_ref, yb_ref, state_ref, tm)


def _prep(x, w_in, l, consts, tables, B, S, tm):
    T = x.shape[0]
    ns = S // tm
    hw = ATTN_HEADS * LANES
    tok = lambda b, s: (b * ns + s, 0)
    pos = lambda b, s: (s, 0)
    g, qg, wq, kvg, wkv, place, wa, ba, hn, cw, cb, wax, bax, lam = consts
    cq, sq, r32, cm, sm = tables
    w_spec = pl.BlockSpec((None,) + w_in.shape[1:], lambda b, s: (l, 0, 0))
    c = lambda a: _layer_spec(a, l) if a.ndim == 3 else _const_spec(a.shape)
    t = lambda a: pl.BlockSpec((a.shape[0], tm), lambda b, s: (0, s))
    nb = max(2 * SUBLANES, -(-(S // MOBA_BLOCK) // SUBLANES) * SUBLANES)
    tok_t = lambda b, s: (0, b * ns + s)
    outs = [(hw, True), (hw, False), (GROUP_WIDTH, True), (hw, True), (hw, False), (GROUP_WIDTH, True),
            (GROUP_WIDTH, False), (GROUP_WIDTH, False)]
    return pl.pallas_call(
        functools.partial(_prep_kernel, tm=tm),
        grid=(B, ns),
        in_specs=[pl.BlockSpec((tm, D_MODEL), tok), c(g), w_spec,
                  c(qg), c(wq), c(kvg), c(wkv), c(place), t(cq), t(sq), t(r32),
                  t(cm), t(sm),
                  c(wa), c(ba), c(hn),
                  c(cw), c(cb), c(wax), c(bax), c(lam)],
        out_specs=[pl.BlockSpec((wd, tm), tok_t) if tr else pl.BlockSpec((tm, wd), tok) for wd, tr in outs],
        out_shape=[jax.ShapeDtypeStruct((wd, T) if tr else (T, wd), BF16) for wd, tr in outs],
        scratch_shapes=[pltpu.VMEM((D_MODEL, W_IN), BF16),
                        pltpu.VMEM((nb, GROUP_WIDTH), F32),
                        pltpu.VMEM((GLA_HEADS * GLA_DV, GLA_HEADS * GLA_DK), F32),
                        pltpu.VMEM((SUBLANES + tm, LRU_WIDTH), F32), pltpu.VMEM((1, LRU_WIDTH), F32)],
        compiler_params=_params(("arbitrary", "arbitrary")),
        name="mixer_prep",
    )(x, g, w_in, qg, wq, kvg, wkv, place, cq, sq, r32, cm, sm, wa, ba, hn, cw, cb, wax, bax, lam)


HEAD_V = 64
ACC_ROWS = HEAD_V + 16


def _attn_kernel(qt_ref, k_ref, vt_ref, o_ref, m_ref, acc_ref, *, tq, tk):
    i = pl.program_id(1)
    m_ref[...] = jnp.full(m_ref.shape, NEG_BIG, F32)
    acc_ref[...] = jnp.zeros(acc_ref.shape, F32)
    ones = jnp.ones((ACC_ROWS - HEAD_V, tk), BF16)

    def score(j, h, q0=0):
        keys = pl.ds(pl.multiple_of(j * tk, tk), tk)
        return _dot(k_ref[keys, h * LANES:(h + 1) * LANES], qt_ref[h * LANES:(h + 1) * LANES, q0:])

    def consume(j, scores, diagonal, q0=0, ahead=None):
        keys = pl.ds(pl.multiple_of(j * tk, tk), tk)
        nq = tq - q0
        if diagonal:
            kpos = j * tk + lax.broadcasted_iota(jnp.int32, (tk, nq), 0)
            qpos = i * tq + q0 + lax.broadcasted_iota(jnp.int32, (tk, nq), 1)
            allowed = kpos <= qpos
        nxt = []
        for h in range(ATTN_HEADS):
            if ahead is not None:
                nxt.append(ahead(h))
            vt = jnp.concatenate([vt_ref[h * HEAD_V:(h + 1) * HEAD_V, keys], ones], axis=0)
            s = scores[h]
            if diagonal:
                s = jnp.where(allowed, s, NEG_BIG)
            m_prev = m_ref[h, :, q0:]
            m_next = jnp.maximum(m_prev, jnp.max(s, axis=0, keepdims=True))
            alpha = jnp.exp2(m_prev - m_next)
            p = jnp.exp2(s - jnp.tile(m_next, (tk // SUBLANES, 1)))
            acc_ref[h, :, q0:] = (jnp.tile(alpha, (ACC_ROWS // SUBLANES, 1)) * acc_ref[h, :, q0:]
                                  + _dot(vt, p.astype(BF16)))
            m_ref[h, :, q0:] = m_next
        return nxt

    n_full = (i * tq) // tk

    def body(j, _):
        consume(j, [score(j, h) for h in range(ATTN_HEADS)], False)
        return 0

    lax.fori_loop(0, n_full, body, 0)
    n_diag = tq // tk
    pending = [score(n_full, h) for h in range(ATTN_HEADS)]
    for d in range(n_diag):
        ahead = None
        if d + 1 < n_diag:
            ahead = functools.partial(score, n_full + d + 1, q0=(d + 1) * tk)
        pending = consume(n_full + d, pending, True, d * tk, ahead)
    out_t = jnp.concatenate(
        [acc_ref[h, :HEAD_V] * jnp.tile(1.0 / acc_ref[h, HEAD_V:HEAD_V + SUBLANES], (HEAD_V // SUBLANES, 1))
         for h in range(ATTN_HEADS)], axis=0)
    o_ref[...] = out_t.T.astype(o_ref.dtype)


def _attn(qt, k, vt, B, S, tq, tk, name):
    T = k.shape[0]
    nq = S // tq
    hw = ATTN_HEADS * LANES
    stat = pltpu.VMEM((ATTN_HEADS, SUBLANES, tq), F32)
    return pl.pallas_call(
        functools.partial(_attn_kernel, tq=tq, tk=tk),
        grid=(B, nq),
        in_specs=[pl.BlockSpec((hw, tq), lambda b, i: (0, b * nq + i)),
                  pl.BlockSpec((S, hw), lambda b, i: (b, 0)),
                  pl.BlockSpec((GROUP_WIDTH, S), lambda b, i: (0, b))],
        out_specs=pl.BlockSpec((tq, GROUP_WIDTH), lambda b, i: (b * nq + i, 0)),
        out_shape=jax.ShapeDtypeStruct((T, GROUP_WIDTH), BF16),
        scratch_shapes=[stat, pltpu.VMEM((ATTN_HEADS, ACC_ROWS, tq), F32)],
        compiler_params=_params(("parallel", "arbitrary")),
        name=name,
    )(qt, k, vt)


def _ffn_kernel(x_ref, ya_ref, yb_ref, yc_ref, yd_ref, wo_ref, fg_ref, wi_ref, w2_ref, ng_ref, o_ref,
                *, final, chunk):
    W = GROUP_WIDTH
    x1 = x_ref[...]
    for n, y_ref in enumerate((ya_ref, yb_ref, yc_ref, yd_ref)):
        x1 = x1 + _dot(y_ref[...], wo_ref[n * W:(n + 1) * W, :])
    h = _rms(x1, fg_ref[...]).astype(BF16)
    n_chunks = FFN_HIDDEN // chunk
    gates = [(_dot(h, wi_ref[:, c * chunk:(c + 1) * chunk]),
              _dot(h, wi_ref[:, FFN_HIDDEN + c * chunk:FFN_HIDDEN + (c + 1) * chunk])) for c in range(n_chunks)]
    parts = []
    for c, (g, up) in enumerate(gates):
        act = (g * _sigmoid(g) * up).astype(BF16)
        parts.append(_dot(act, w2_ref[c * chunk:(c + 1) * chunk, :]))
    acc = x1 + sum(parts[1:], parts[0])
    if final:
        acc = _rms(acc, ng_ref[...])
    o_ref[...] = acc


def _layer_spec(stacked, l):
    nd = stacked.ndim - 1
    return pl.BlockSpec((None,) + stacked.shape[1:], lambda *_: (l,) + (0,) * nd, pipeline_mode=pl.Buffered(1))


def _ffn(x, ya, yb, yc, yd, wo, fg, wi, w2, ng, l, final, tm, chunk):
    T = x.shape[0]
    tok = lambda i: (i, 0)
    ysp = pl.BlockSpec((tm, GROUP_WIDTH), tok)
    return pl.pallas_call(
        functools.partial(_ffn_kernel, final=final, chunk=chunk),
        grid=(T // tm,),
        in_specs=[pl.BlockSpec((tm, D_MODEL), tok), ysp, ysp, ysp, ysp,
                  _layer_spec(wo, l), _layer_spec(fg, l), _layer_spec(wi, l),
                  _layer_spec(w2, l), _const_spec(ng.shape)],
        out_specs=pl.BlockSpec((tm, D_MODEL), tok),
        out_shape=jax.ShapeDtypeStruct((T, D_MODEL), F32),
        compiler_params=_params(("parallel",)),
        name="outproj_ffn",
    )(x, ya, yb, yc, yd, wo, fg, wi, w2, ng)


def _swap_cols(w, half):
    return jnp.concatenate([-w[..., half:2 * half], w[..., :half]], axis=-1)


def _layout_mla(w_uq, w_ukv):
    zq = jnp.zeros(w_uq.shape[:-1] + (LANES - MLA_NOPE - MLA_ROPE,), w_uq.dtype)
    zr = jnp.zeros(w_uq.shape[:-1] + (MLA_NOPE,), w_uq.dtype)
    qd = MLA_NOPE + MLA_ROPE
    plain, swapped = [], []
    for h in range(MLA_HEADS):
        nope = w_uq[..., h * qd:h * qd + MLA_NOPE]
        rope = w_uq[..., h * qd + MLA_NOPE:(h + 1) * qd]
        plain += [nope, rope, zq]
        swapped += [zr, _swap_cols(rope, MLA_ROPE // 2), zq]
    wq = jnp.concatenate(plain + swapped, axis=-1).astype(BF16)
    zk = jnp.zeros(w_ukv.shape[:-1] + (LANES - MLA_NOPE,), w_ukv.dtype)
    kd = MLA_NOPE + MLA_V
    kparts = []
    vparts = []
    for h in range(MLA_HEADS):
        kparts += [w_ukv[..., h * kd:h * kd + MLA_NOPE], zk]
        vparts += [w_ukv[..., h * kd + MLA_NOPE:(h + 1) * kd]]
    wkv = jnp.concatenate(kparts + vparts, axis=-1).astype(BF16)
    place = np.zeros((MLA_ROPE, MLA_HEADS * LANES), np.float32)
    for h in range(MLA_HEADS):
        place[np.arange(MLA_ROPE), h * LANES + MLA_NOPE + np.arange(MLA_ROPE)] = 1.0
    return wq, wkv, jnp.asarray(place, BF16)


def _tables(S):
    pos = jnp.arange(S, dtype=F32)[None, :]

    def cs(dim):
        inv_freq = ROPE_THETA ** (-jnp.arange(0, dim, 2, dtype=F32) / dim)
        ang = inv_freq[:, None] * pos
        return jnp.cos(ang), jnp.sin(ang)

    c, s = cs(MLA_ROPE)
    scale = (MLA_NOPE + MLA_ROPE) ** -0.5 * LOG2E
    one = jnp.ones((MLA_NOPE, S), F32)
    zero = jnp.zeros((MLA_NOPE, S), F32)
    pad1 = jnp.ones((LANES - MLA_NOPE - MLA_ROPE, S), F32)
    pad0 = jnp.zeros((LANES - MLA_NOPE - MLA_ROPE, S), F32)
    cq = jnp.concatenate([one, c, c, pad1], axis=0) * scale
    sq = jnp.concatenate([zero, s, s, pad0], axis=0) * scale
    r32 = jnp.concatenate([c, c, s, s, zero], axis=0)
    c, s = cs(MOBA_ROT)
    one = jnp.ones((MOBA_HD - MOBA_ROT, S), F32)
    zero = jnp.zeros((MOBA_HD - MOBA_ROT, S), F32)
    cm = jnp.concatenate([c, c, one] * 2, axis=0)
    sm = jnp.concatenate([-s, s, zero] * 2, axis=0)
    return cq, sq, r32, cm, sm


def _block_diag(w):
    depth, n, c, d = w.shape
    rows = []
    for j in range(n):
        rows.append(jnp.concatenate([jnp.zeros((depth, c, j * d), w.dtype), w[:, j],
                                     jnp.zeros((depth, c, (n - 1 - j) * d), w.dtype)], axis=-1))
    return jnp.concatenate(rows, axis=1)


def _stacked_consts(attn_norm, mla_q_norm, mla_w_uq, mla_kv_norm, mla_w_ukv, gla_w_a2, gla_b_a2,
                    gla_head_norm, lru_conv_w, lru_conv_b, lru_w_a, lru_b_a, lru_w_x, lru_b_x, lru_lambda):
    depth = attn_norm.shape[0]
    r3 = lambda v: v.reshape(depth, 1, -1).astype(F32)
    wq, wkv, place = _layout_mla(mla_w_uq, mla_w_ukv)
    wa = jnp.pad(gla_w_a2, ((0, 0), (0, LANES - GLA_GATE_RANK), (0, 0))).astype(BF16)
    hn = jnp.tile(r3(gla_head_norm), (1, 1, GLA_HEADS))
    wax = jnp.concatenate([_block_diag(lru_w_a), _block_diag(lru_w_x)], axis=-1).astype(BF16)
    bax = jnp.concatenate([r3(lru_b_a), r3(lru_b_x)], axis=-1)
    return (r3(attn_norm), r3(mla_q_norm), wq, r3(mla_kv_norm), wkv, place,
            wa, r3(gla_b_a2), hn, lru_conv_w.reshape(depth, LRU_CONV, LRU_WIDTH), r3(lru_conv_b), wax, bax,
            r3(lru_lambda))


def kernel(x, attn_norm, w_in, mla_q_norm, mla_w_uq, mla_kv_norm, mla_w_ukv, gla_w_a2, gla_b_a2, gla_head_norm,
           lru_conv_w, lru_conv_b, lru_w_a, lru_b_a, lru_w_x, lru_b_x, lru_lambda, w_out, ffn_norm, w_ffn_in,
           w_ffn_out, final_norm):
    B, S, D = x.shape
    depth = w_in.shape[0]
    T = B * S
    tq = min(S, ATTN_TQ)
    assert D == D_MODEL and S % max(PREP_TM, tq, FFN_TM) == 0 and S // MOBA_BLOCK <= MOBA_SLOT
    tables = _tables(S)
    row = lambda v: v.reshape(1, -1).astype(F32)
    xt = x.reshape(T, D)
    wo, wi, w2 = w_out.astype(BF16), w_ffn_in.astype(BF16), w_ffn_out.astype(BF16)
    fg = ffn_norm.reshape(depth, 1, D).astype(F32)
    consts = _stacked_consts(attn_norm, mla_q_norm, mla_w_uq, mla_kv_norm, mla_w_ukv, gla_w_a2, gla_b_a2,
                             gla_head_norm, lru_conv_w, lru_conv_b, lru_w_a, lru_b_a, lru_w_x, lru_b_x, lru_lambda)
    for l in range(depth):
        mq, mk, mv, bq, bk, bv, y_b, y_d = _prep(xt, w_in, l, consts, tables, B, S, PREP_TM)
        y_a = _attn(mq, mk, mv, B, S, tq, ATTN_TK, "mla_attn")
        y_c = _attn(bq, bk, bv, B, S, tq, ATTN_TK, "moba_attn")
        xt = _ffn(xt, y_a, y_b, y_c, y_d, wo, fg, wi, w2, row(final_norm), l, l == depth - 1,
                  FFN_TM, FFN_HIDDEN // 2)
    return xt.reshape(B, S, D)
```
